```python
import math
import jax
import jax.numpy as jnp
from jax import lax
import numpy as np

D_MODEL = 2048
BATCH = 2
SEQ = 4096
DEPTH = 4

GRID_W = 64
CTX_LEN = 256
S5_GROUPS = 32
S5_GROUP_CH = 16
S5_STATE = 64
S5_WIDTH = S5_GROUPS * S5_GROUP_CH
RWKV_HEADS = 16
RWKV_HEAD = 64
RWKV_WIDTH = RWKV_HEADS * RWKV_HEAD
DECAY_LORA = 64
ICLR_LORA = 64
GATE_LORA = 160
RWKV_SHIFT_COLS = 3 * RWKV_WIDTH + 2 * DECAY_LORA + 2 * ICLR_LORA + GATE_LORA
IN_COLS = S5_WIDTH + RWKV_SHIFT_COLS + 2 * D_MODEL
D_FF = 5632
N_EXPERTS = 8
TOP_K = 2
D_FF_EXPERT = 2816
N_DENSE = (DEPTH + 1) // 2
N_MOE = DEPTH // 2
NORM_EPS = 1e-6
GN_EPS = 64e-5
F32 = jnp.float32

kernel_name = 'hybrid_s5_rwkv7_moe_flow_block'


def rmsnorm(x, gain):
    xf = x.astype(F32)
    y = xf * lax.rsqrt(jnp.mean(xf * xf, axis=-1, keepdims=True) + NORM_EPS)
    return (y * gain.astype(F32)).astype(x.dtype)


def modulate(h, shift, scale):
    return h * (1 + scale) + shift


def grid_qshift(z):
    b, l, ch = z.shape
    rows = l // GRID_W
    g = z.reshape(b, rows, GRID_W, ch // 4, 4)
    left = jnp.pad(g[:, :, :-1, :, 0], ((0, 0), (0, 0), (1, 0), (0, 0)))
    right = jnp.pad(g[:, :, 1:, :, 1], ((0, 0), (0, 0), (0, 1), (0, 0)))
    up = jnp.pad(g[:, :-1, :, :, 2], ((0, 0), (1, 0), (0, 0), (0, 0)))
    down = jnp.pad(g[:, 1:, :, :, 3], ((0, 0), (0, 1), (0, 0), (0, 0)))
    return jnp.stack([left, right, up, down], axis=-1).reshape(b, l, ch)


def seq_shift(z):
    b, l, ch = z.shape
    g = z.reshape(b, l, ch // 4, 4)
    prev = jnp.pad(g[:, :-1], ((0, 0), (1, 0), (0, 0), (0, 0)))
    nxt = jnp.pad(g[:, 1:], ((0, 0), (0, 1), (0, 0), (0, 0)))
    from_prev = (jnp.arange(4) % 2) == 0
    return jnp.where(from_prev, prev, nxt).reshape(b, l, ch)


def s5_discretize(lam_re, lam_im, log_step, b_re, b_im):
    lam = lax.complex(lam_re.astype(F32), lam_im.astype(F32))
    step = jnp.exp(log_step.astype(F32))[..., None]
    lam_bar = jnp.exp(lam * step)
    b = lax.complex(b_re.astype(F32), b_im.astype(F32))
    b_bar = ((lam_bar - 1) / lam)[..., None] * b
    return lam_bar, b_bar


def _affine_combine(e1, e2):
    a1, b1 = e1
    a2, b2 = e2
    return a2 * a1, a2 * b1 + b2


def diag_scan(lam_bar, bu, h0, reverse):
    if h0 is not None:
        first = -1 if reverse else 0
        bu = bu.at[:, first].add(lam_bar * h0)
    a = jnp.broadcast_to(lam_bar, bu.shape)
    _, h = lax.associative_scan(_affine_combine, (a, bu), reverse=reverse, axis=1)
    return h


def s5_readout(h_f, h_b, u, c_re, c_im, d_skip, w_glu):
    c = lax.complex(c_re.astype(F32), c_im.astype(F32))
    y = jnp.real(jnp.einsum('blgp,ghp->blgh', h_f + h_b, c)) + d_skip.astype(F32) * u
    y = jax.nn.gelu(y.reshape(y.shape[0], y.shape[1], S5_WIDTH))
    return y * jax.nn.sigmoid(y @ w_glu.astype(F32))


def rwkv_features(zr, w0, w_up, a0, a_up, g_up, k_k, k_a):
    bsz, l, _ = zr.shape
    cuts = np.cumsum((RWKV_WIDTH,) * 3 + (DECAY_LORA,) * 2 + (ICLR_LORA,) * 2).tolist()
    r, k, v, wd_f, wd_b, ad_f, ad_b, gd = jnp.split(zr, cuts, axis=-1)
    heads = lambda t: t.reshape(bsz, l, RWKV_HEADS, RWKV_HEAD)
    kk = heads(k * k_k)
    kk = kk / jnp.maximum(jnp.sqrt(jnp.sum(kk * kk, axis=-1, keepdims=True)), 1e-12)
    dirs = []
    for d, (wd, ad) in enumerate(((wd_f, ad_f), (wd_b, ad_b))):
        w_log = -jax.nn.softplus(-(w0[d] + jnp.tanh(wd) @ w_up[d])) - 0.5
        a = jax.nn.sigmoid(a0[d] + ad @ a_up[d])
        k_d = k * (1 + (a - 1) * k_a)
        dirs.append((heads(jnp.exp(-jnp.exp(w_log))), heads(a), heads(k_d)))
    g = jax.nn.sigmoid(gd) @ g_up
    return heads(r), heads(v), kk, dirs, g


def wkv_scan(r, decay, k, v, a, b, s0, reverse):
    def step(s, inp):
        r_t, w_t, k_t, v_t, a_t, b_t = inp
        sa = jnp.einsum('bhij,bhj->bhi', s, a_t)
        s = s * w_t[:, :, None, :] + sa[..., None] * b_t[:, :, None, :] + v_t[..., None] * k_t[:, :, None, :]
        return s, jnp.einsum('bhij,bhj->bhi', s, r_t)
    xs = tuple(jnp.moveaxis(t, 1, 0) for t in (r, decay, k, v, a, b))
    s_final, y = lax.scan(step, s0, xs, reverse=reverse)
    return jnp.moveaxis(y, 0, 1), s_final


def rwkv_scan_dir(feats, d, s0, reverse):
    r, v, kk, dirs, _ = feats
    decay, a, k_d = dirs[d]
    return wkv_scan(r, decay, k_d, v, -kk, kk * a, s0, reverse)


def rwkv_readout(y_f, y_b, feats, r_k, gn_w, gn_b):
    r, v, _, dirs, g = feats
    y = y_f + y_b
    mean = jnp.mean(y, axis=-1, keepdims=True)
    var = jnp.mean(jnp.square(y - mean), axis=-1, keepdims=True)
    y = (y - mean) * lax.rsqrt(var + GN_EPS)
    bsz, l = y.shape[:2]
    y = y.reshape(bsz, l, RWKV_WIDTH) * gn_w + gn_b
    bonus = jnp.sum(r * (dirs[0][2] + dirs[1][2]) * r_k, axis=-1, keepdims=True) * v
    return (y + bonus.reshape(bsz, l, RWKV_WIDTH)) * g


def merge_branches(y_s5, y_rw, gate_cols, w_proj_s5, w_proj_rw, w_out):
    gate_s5, gate_rw = jnp.split(gate_cols, 2, axis=-1)
    m = jax.nn.sigmoid(gate_s5) * (y_s5 @ w_proj_s5) + jax.nn.sigmoid(gate_rw) * (y_rw @ w_proj_rw)
    return m @ w_out


def mixer_layer(h_c, h_l, need_ctx_out, w_in, lam_re, lam_im, log_step, b_re, b_im, c_re, c_im,
                d_skip, w_glu, mu, w0, w_up, a0, a_up, g_up, k_k, k_a, r_k, gn_w, gn_b,
                w_proj_s5, w_proj_rw, w_out):
    bsz = h_l.shape[0]
    cut_s5, cut_rw = S5_WIDTH, S5_WIDTH + RWKV_SHIFT_COLS
    z_c = (h_c @ w_in).astype(F32)
    z_l = (h_l @ w_in).astype(F32)
    grp = lambda u: u.reshape(u.shape[0], u.shape[1], S5_GROUPS, S5_GROUP_CH)
    u_c, zr_c, gates_c = grp(z_c[..., :cut_s5]), z_c[..., cut_s5:cut_rw], z_c[..., cut_rw:]
    u_l, zr_l, gates_l = grp(z_l[..., :cut_s5]), z_l[..., cut_s5:cut_rw], z_l[..., cut_rw:]

    lam_bar, b_bar = s5_discretize(lam_re, lam_im, log_step, b_re, b_im)
    bu = lambda u, d: jnp.einsum('blgh,gph->blgp', u.astype(jnp.complex64), b_bar[d])
    h_cf = diag_scan(lam_bar[0], bu(u_c, 0), None, False)
    h_cb = diag_scan(lam_bar[1], bu(u_c, 1), None, True)
    h_lf = diag_scan(lam_bar[0], bu(u_l, 0), h_cf[:, -1], False)
    h_lb = diag_scan(lam_bar[1], bu(u_l, 1), h_cb[:, 0], True)

    mu = mu.astype(F32)
    zr_c = zr_c + mu * (seq_shift(zr_c) - zr_c)
    zr_l = zr_l + mu * (grid_qshift(zr_l) - zr_l)
    f_c = rwkv_features(zr_c, w0, w_up, a0, a_up, g_up, k_k, k_a)
    f_l = rwkv_features(zr_l, w0, w_up, a0, a_up, g_up, k_k, k_a)
    s0 = jnp.zeros((bsz, RWKV_HEADS, RWKV_HEAD, RWKV_HEAD), F32)
    y_cf, s_cf = rwkv_scan_dir(f_c, 0, s0, False)
    y_cb, s_cb = rwkv_scan_dir(f_c, 1, s0, True)
    y_lf, _ = rwkv_scan_dir(f_l, 0, s_cf, False)
    y_lb, _ = rwkv_scan_dir(f_l, 1, s_cb, True)

    def readout(h_f, h_b, u, y_f, y_b, feats, gates):
        y_s5 = s5_readout(h_f, h_b, u, c_re, c_im, d_skip, w_glu)
        y_rw = rwkv_readout(y_f, y_b, feats, r_k, gn_w, gn_b)
        return merge_branches(y_s5, y_rw, gates, w_proj_s5, w_proj_rw, w_out).astype(h_l.dtype)

    out_l = readout(h_lf, h_lb, u_l, y_lf, y_lb, f_l, gates_l)
    out_c = readout(h_cf, h_cb, u_c, y_cf, y_cb, f_c, gates_c) if need_ctx_out else None
    return out_c, out_l


def swiglu(h, w_gate, w_up, w_down):
    return (jax.nn.silu(h @ w_gate) * (h @ w_up)) @ w_down


def moe_swiglu(h, router, router_bias, w_gate, w_up, w_down):
    logits = (h @ router).astype(F32) + router_bias.astype(F32)
    top_val, top_idx = lax.top_k(logits, TOP_K)
    top_w = jax.nn.softmax(top_val, axis=-1)
    gates = jnp.sum(jax.nn.one_hot(top_idx, N_EXPERTS, dtype=F32) * top_w[..., None], axis=-2)
    out = jnp.zeros(h.shape, F32)
    for e in range(N_EXPERTS):
        out = out + gates[..., e:e + 1] * swiglu(h, w_gate[e], w_up[e], w_down[e])
    return out.astype(h.dtype)


def setup_inputs(seed: int = 0) -> dict:
    key = jax.random.key(seed)
    keys = iter(jax.random.split(key, 48))

    def nrm(shape, std):
        return std * jax.random.normal(next(keys), shape, F32)

    def unif(shape, lo, hi):
        return jax.random.uniform(next(keys), shape, F32, lo, hi)

    d = D_MODEL
    n_state = jnp.arange(S5_STATE, dtype=F32)
    chan = jnp.arange(RWKV_WIDTH, dtype=F32) / (RWKV_WIDTH - 1)
    decay_speed = -7.0 + 5.0 * chan ** 0.85 + 0.5
    return {
        'x': nrm((BATCH, SEQ, d), 1.0),
        'c': nrm((BATCH, d), 1.0),
        'ctx': nrm((BATCH, CTX_LEN, d), 1.0),
        'c_ctx': nrm((d,), 1.0),
        'w_ada': nrm((DEPTH, d, 6 * d), 0.5 * d ** -0.5),
        'b_ada': nrm((DEPTH, 6 * d), 0.01),
        'norm1': 1.0 + nrm((DEPTH, d), 0.1),
        'norm2': 1.0 + nrm((DEPTH, d), 0.1),
        'norm_f': 1.0 + nrm((d,), 0.1),
        'w_in': nrm((DEPTH, d, IN_COLS), d ** -0.5),
        's5_lambda_re': -0.5 + nrm((DEPTH, 2, S5_GROUPS, S5_STATE), 0.01),
        's5_lambda_im': math.pi * n_state + nrm((DEPTH, 2, S5_GROUPS, S5_STATE), 0.01),
        's5_log_step': unif((DEPTH, 2, S5_GROUPS), math.log(1e-3), math.log(1e-1)),
        's5_b_re': nrm((DEPTH, S5_GROUPS, S5_STATE, S5_GROUP_CH), (2 * S5_GROUP_CH) ** -0.5),
        's5_b_im': nrm((DEPTH, S5_GROUPS, S5_STATE, S5_GROUP_CH), (2 * S5_GROUP_CH) ** -0.5),
        's5_c_re': nrm((DEPTH, S5_GROUPS, S5_GROUP_CH, S5_STATE), 0.3),
        's5_c_im': nrm((DEPTH, S5_GROUPS, S5_GROUP_CH, S5_STATE), 0.3),
        's5_d': nrm((DEPTH, S5_GROUPS, S5_GROUP_CH), 1.0),
        's5_w_glu': nrm((DEPTH, S5_WIDTH, S5_WIDTH), S5_WIDTH ** -0.5),
        'rw_mu': unif((DEPTH, RWKV_SHIFT_COLS), 0.0, 1.0),
        'rw_w0': decay_speed + nrm((DEPTH, 2, RWKV_WIDTH), 0.1),
        'rw_w_up': nrm((DEPTH, 2, DECAY_LORA, RWKV_WIDTH), 0.5 * DECAY_LORA ** -0.5),
        'rw_a0': nrm((DEPTH, 2, RWKV_WIDTH), 0.1),
        'rw_a_up': nrm((DEPTH, 2, ICLR_LORA, RWKV_WIDTH), 0.5 * ICLR_LORA ** -0.5),
        'rw_g_up': nrm((DEPTH, GATE_LORA, RWKV_WIDTH), GATE_LORA ** -0.5),
        'rw_k_k': 0.85 + nrm((DEPTH, RWKV_WIDTH), 0.05),
        'rw_k_a': 1.0 + nrm((DEPTH, RWKV_WIDTH), 0.05),
        'rw_r_k': nrm((DEPTH, RWKV_HEADS, RWKV_HEAD), 0.1),
        'rw_gn_w': 1.0 + nrm((DEPTH, RWKV_WIDTH), 0.1),
        'rw_gn_b': nrm((DEPTH, RWKV_WIDTH), 0.01),
        'w_proj_s5': nrm((DEPTH, S5_WIDTH, d), S5_WIDTH ** -0.5),
        'w_proj_rw': nrm((DEPTH, RWKV_WIDTH, d), RWKV_WIDTH ** -0.5),
        'w_out': nrm((DEPTH, d, d), d ** -0.5),
        'ffn_w_gate': nrm((N_DENSE, d, D_FF), d ** -0.5),
        'ffn_w_up': nrm((N_DENSE, d, D_FF), d ** -0.5),
        'ffn_w_down': nrm((N_DENSE, D_FF, d), D_FF ** -0.5),
        'moe_router': nrm((N_MOE, d, N_EXPERTS), d ** -0.5),
        'moe_router_bias': nrm((N_MOE, N_EXPERTS), 0.01),
        'moe_w_gate': nrm((N_MOE, N_EXPERTS, d, D_FF_EXPERT), d ** -0.5),
        'moe_w_up': nrm((N_MOE, N_EXPERTS, d, D_FF_EXPERT), d ** -0.5),
        'moe_w_down': nrm((N_MOE, N_EXPERTS, D_FF_EXPERT, d), D_FF_EXPERT ** -0.5),
    }


def reference(x, c, ctx, c_ctx, w_ada, b_ada, norm1, norm2, norm_f, w_in,
              s5_lambda_re, s5_lambda_im, s5_log_step, s5_b_re, s5_b_im, s5_c_re, s5_c_im, s5_d, s5_w_glu,
              rw_mu, rw_w0, rw_w_up, rw_a0, rw_a_up, rw_g_up, rw_k_k, rw_k_a, rw_r_k, rw_gn_w, rw_gn_b,
              w_proj_s5, w_proj_rw, w_out, ffn_w_gate, ffn_w_up, ffn_w_down,
              moe_router, moe_router_bias, moe_w_gate, moe_w_up, moe_w_down):
    silu_c = jax.nn.silu(c)
    silu_cc = jax.nn.silu(c_ctx)
    for i in range(DEPTH):
        last = i == DEPTH - 1
        mod_l = (silu_c @ w_ada[i] + b_ada[i])[:, None, :]
        mod_c = (silu_cc @ w_ada[i] + b_ada[i])[None, None, :]
        sh1_l, sc1_l, g1_l, sh2_l, sc2_l, g2_l = jnp.split(mod_l, 6, axis=-1)
        sh1_c, sc1_c, g1_c, sh2_c, sc2_c, g2_c = jnp.split(mod_c, 6, axis=-1)

        h_l = modulate(rmsnorm(x, norm1[i]), sh1_l, sc1_l)
        h_c = modulate(rmsnorm(ctx, norm1[i]), sh1_c, sc1_c)
        out_c, out_l = mixer_layer(
            h_c, h_l, not last, w_in[i], s5_lambda_re[i], s5_lambda_im[i], s5_log_step[i],
            s5_b_re[i], s5_b_im[i], s5_c_re[i], s5_c_im[i], s5_d[i], s5_w_glu[i],
            rw_mu[i], rw_w0[i], rw_w_up[i], rw_a0[i], rw_a_up[i], rw_g_up[i], rw_k_k[i], rw_k_a[i],
            rw_r_k[i], rw_gn_w[i], rw_gn_b[i], w_proj_s5[i], w_proj_rw[i], w_out[i])
        x = x + g1_l * out_l

        if i % 2 == 0:
            j = i // 2
            ffn = lambda h: swiglu(h, ffn_w_gate[j], ffn_w_up[j], ffn_w_down[j])
        else:
            j = i // 2
            ffn = lambda h: moe_swiglu(h, moe_router[j], moe_router_bias[j], moe_w_gate[j], moe_w_up[j], moe_w_down[j])
        x = x + g2_l * ffn(modulate(rmsnorm(x, norm2[i]), sh2_l, sc2_l))

        if not last:
            ctx = ctx + g1_c * out_c
            ctx = ctx + g2_c * ffn(modulate(rmsnorm(ctx, norm2[i]), sh2_c, sc2_c))
    return rmsnorm(x, norm_f)
```

```python
import functools
import math

import jax
import jax.numpy as jnp
from jax import lax
from jax.experimental import pallas as pl
from jax.experimental.pallas import tpu as pltpu

F32 = jnp.float32
BF16 = jnp.bfloat16

GRID_W = 64
HEADS = 16
HEAD = 64
RW = HEADS * HEAD
S5_G = 32
S5_H = 16
S5_P = 64
S5_W = S5_G * S5_H
S5_T = 32
DECAY_LORA = 64
ICLR_LORA = 64
GATE_LORA = 160
LORA_COLS = 2 * DECAY_LORA + 2 * ICLR_LORA + GATE_LORA
LORA_PAD = 512
ZR_COLS = 3 * RW + LORA_PAD
NORM_EPS = 1e-6
GN_EPS = 64e-5
N_EXPERTS = 8
LANES = 128
VMEM_LIMIT = 56 * 1024 * 1024


def _cp(sem):
    return pltpu.CompilerParams(dimension_semantics=sem, vmem_limit_bytes=VMEM_LIMIT)


def _sigmoid(x):
    return 1.0 / (1.0 + jnp.exp(-x))


def _gelu_tanh(x):
    return 0.5 * x * (1.0 + jnp.tanh(math.sqrt(2.0 / math.pi) * (x + 0.044715 * (x * x * x))))


def _row_mod(tab_ref, tile, tm, tb, lc):
    start = tile * tm
    b = start // tb
    pos = (start - b * tb) + lax.broadcasted_iota(jnp.int32, (tm, 1), 0)
    ctx_row = tab_ref[0:1, :]
    lat_row = tab_ref[pl.ds(1 + b, 1), :]
    return jnp.where(pos < lc, ctx_row, lat_row)


def _split_dot(x, w_bf16):
    hi = x.astype(BF16)
    lo = (x - hi.astype(F32)).astype(BF16)
    return (jnp.dot(hi, w_bf16, preferred_element_type=F32)
            + jnp.dot(lo, w_bf16, preferred_element_type=F32))


def _lead_spec(lead, shape, fn):
    return pl.BlockSpec((None,) * len(lead) + shape, lambda j, i: tuple(lead) + fn(j, i))


def _mm_body(x_ref, w_ref, *rest, has_bias):
    if has_bias:
        b_ref, o_ref, wb_ref = rest
    else:
        o_ref, wb_ref = rest

    @pl.when(pl.program_id(1) == 0)
    def _():
        wb_ref[...] = w_ref[...].astype(BF16)

    acc = jnp.dot(x_ref[...], wb_ref[...], preferred_element_type=F32)
    if has_bias:
        acc = acc + b_ref[...]
    o_ref[...] = acc.astype(o_ref.dtype)


def _matmul(x, w, lead, n_out, *, col_off=0, tm, tn, out_dtype=F32, bias=None):
    m, k = x.shape
    off = col_off // tn
    in_specs = [pl.BlockSpec((tm, k), lambda j, i: (i, 0)),
                _lead_spec(lead, (k, tn), lambda j, i: (0, j + off))]
    args = [x, w]
    if bias is not None:
        in_specs.append(pl.BlockSpec((1, tn), lambda j, i: (0, j)))
        args.append(bias)
    return pl.pallas_call(
        functools.partial(_mm_body, has_bias=bias is not None),
        grid=(n_out // tn, m // tm),
        in_specs=in_specs,
        out_specs=pl.BlockSpec((tm, tn), lambda j, i: (i, j)),
        out_shape=jax.ShapeDtypeStruct((m, n_out), out_dtype),
        scratch_shapes=[pltpu.VMEM((k, tn), BF16)],
        compiler_params=_cp(("arbitrary", "arbitrary")),
    )(*args)


def _swiglu_up_body(x_ref, wg_ref, wu_ref, o_ref, wgb_ref, wub_ref):
    @pl.when(pl.program_id(1) == 0)
    def _():
        wgb_ref[...] = wg_ref[...].astype(BF16)
        wub_ref[...] = wu_ref[...].astype(BF16)

    x = x_ref[...]
    g = jnp.dot(x, wgb_ref[...], preferred_element_type=F32)
    u = jnp.dot(x, wub_ref[...], preferred_element_type=F32)
    o_ref[...] = (g * _sigmoid(g) * u).astype(o_ref.dtype)


def _swiglu_up(x, wg, wu, lead, *, tm, tn):
    m, k = x.shape
    f = wg.shape[-1]
    wspec = _lead_spec(lead, (k, tn), lambda j, i: (0, j))
    return pl.pallas_call(
        _swiglu_up_body,
        grid=(f // tn, m // tm),
        in_specs=[pl.BlockSpec((tm, k), lambda j, i: (i, 0)), wspec, wspec],
        out_specs=pl.BlockSpec((tm, tn), lambda j, i: (i, j)),
        out_shape=jax.ShapeDtypeStruct((m, f), BF16),
        scratch_shapes=[pltpu.VMEM((k, tn), BF16), pltpu.VMEM((k, tn), BF16)],
        compiler_params=_cp(("arbitrary", "arbitrary")),
    )(x, wg, wu)


def _down_resid_body(a_ref, w_ref, x_ref, tab_ref, *rest, tm, tb, lc, expert):
    if expert is None:
        o_ref, wb_ref = rest
    else:
        rs_ref, o_ref, wb_ref = rest

    @pl.when(pl.program_id(1) == 0)
    def _():
        wb_ref[...] = w_ref[...].astype(BF16)

    y = jnp.dot(a_ref[...], wb_ref[...], preferred_element_type=F32)
    gate = _row_mod(tab_ref, pl.program_id(1), tm, tb, lc)
    if expert is not None:
        gate = gate * rs_ref[:, expert:expert + 1]
    o_ref[...] = x_ref[...] + gate * y


def _down_resid(a, w, lead, x, mod, gate_idx, *, tb, lc, tm, tn, rowscale=None, expert=None):
    m, k = a.shape
    d = x.shape[1]
    goff = gate_idx * (d // tn)
    in_specs = [pl.BlockSpec((tm, k), lambda j, i: (i, 0)),
                _lead_spec(lead, (k, tn), lambda j, i: (0, j)),
                pl.BlockSpec((tm, tn), lambda j, i: (i, j)),
                pl.BlockSpec((8, tn), lambda j, i: (0, goff + j))]
    args = [a, w, x, mod]
    if rowscale is not None:
        in_specs.append(pl.BlockSpec((tm, LANES), lambda j, i: (i, 0)))
        args.append(rowscale)
    return pl.pallas_call(
        functools.partial(_down_resid_body, tm=tm, tb=tb, lc=lc, expert=expert),
        grid=(d // tn, m // tm),
        in_specs=in_specs,
        out_specs=pl.BlockSpec((tm, tn), lambda j, i: (i, j)),
        out_shape=jax.ShapeDtypeStruct((m, d), F32),
        scratch_shapes=[pltpu.VMEM((k, tn), BF16)],
        input_output_aliases={2: 0},
        compiler_params=_cp(("arbitrary", "arbitrary")),
    )(*args)


def _merge_body(ys_ref, yr_ref, ws_ref, wr_ref, gs_ref, gr_ref, o_ref, wsb_ref, wrb_ref):
    @pl.when(pl.program_id(1) == 0)
    def _():
        wsb_ref[...] = ws_ref[...].astype(BF16)
        wrb_ref[...] = wr_ref[...].astype(BF16)

    ps = jnp.dot(ys_ref[...], wsb_ref[...], preferred_element_type=F32)
    pr = jnp.dot(yr_ref[...], wrb_ref[...], preferred_element_type=F32)
    o_ref[...] = (_sigmoid(gs_ref[...]) * ps + _sigmoid(gr_ref[...]) * pr).astype(o_ref.dtype)


def _merge(ys, yr, w_ps, w_pr, layer, zg, *, tm, tn):
    m = ys.shape[0]
    d = w_ps.shape[-1]
    nb = d // tn
    return pl.pallas_call(
        _merge_body,
        grid=(nb, m // tm),
        in_specs=[pl.BlockSpec((tm, S5_W), lambda j, i: (i, 0)),
                  pl.BlockSpec((tm, RW), lambda j, i: (i, 0)),
                  _lead_spec((layer,), (S5_W, tn), lambda j, i: (0, j)),
                  _lead_spec((layer,), (RW, tn), lambda j, i: (0, j)),
                  pl.BlockSpec((tm, tn), lambda j, i: (i, j)),
                  pl.BlockSpec((tm, tn), lambda j, i: (i, nb + j))],
        out_specs=pl.BlockSpec((tm, tn), lambda j, i: (i, j)),
        out_shape=jax.ShapeDtypeStruct((m, d), BF16),
        scratch_shapes=[pltpu.VMEM((S5_W, tn), BF16), pltpu.VMEM((RW, tn), BF16)],
        compiler_params=_cp(("arbitrary", "arbitrary")),
    )(ys, yr, w_ps, w_pr, zg, zg)


def _norm_mod_body(x_ref, gain_ref, sh_ref, sc_ref, o_ref, *, tm, tb, lc):
    x = x_ref[...]
    y = x * lax.rsqrt(jnp.mean(x * x, axis=-1, keepdims=True) + NORM_EPS) * gain_ref[...]
    i = pl.program_id(0)
    shift = _row_mod(sh_ref, i, tm, tb, lc)
    scale = _row_mod(sc_ref, i, tm, tb, lc)
    o_ref[...] = (y * (1.0 + scale) + shift).astype(o_ref.dtype)


def _norm_mod(x, gain, layer, mod, shift_idx, *, tb, lc, tm, out_dtype=BF16):
    m, d = x.shape
    return pl.pallas_call(
        functools.partial(_norm_mod_body, tm=tm, tb=tb, lc=lc),
        grid=(m // tm,),
        in_specs=[pl.BlockSpec((tm, d), lambda i: (i, 0)),
                  pl.BlockSpec((None, 1, d), lambda i: (layer, 0, 0)),
                  pl.BlockSpec((8, d), lambda i: (0, shift_idx)),
                  pl.BlockSpec((8, d), lambda i: (0, shift_idx + 1))],
        out_specs=pl.BlockSpec((tm, d), lambda i: (i, 0)),
        out_shape=jax.ShapeDtypeStruct((m, d), out_dtype),
        compiler_params=_cp(("arbitrary",)),
    )(x, gain, mod, mod)


def _final_norm_body(x_ref, gain_ref, o_ref):
    x = x_ref[...]
    o_ref[...] = x * lax.rsqrt(jnp.mean(x * x, axis=-1, keepdims=True) + NORM_EPS) * gain_ref[...]


def _final_norm(x3, gain, *, lc, tm):
    b, tb, d = x3.shape
    l = tb - lc
    skip = lc // tm
    return pl.pallas_call(
        _final_norm_body,
        grid=(b, l // tm),
        in_specs=[pl.BlockSpec((None, tm, d), lambda bi, t: (bi, skip + t, 0)),
                  pl.BlockSpec((1, d), lambda bi, t: (0, 0))],
        out_specs=pl.BlockSpec((None, tm, d), lambda bi, t: (bi, t, 0)),
        out_shape=jax.ShapeDtypeStruct((b, l, d), F32),
        compiler_params=_cp(("arbitrary", "arbitrary")),
    )(x3, gain)


def _router_body(x_ref, gain_ref, sh_ref, sc_ref, w_ref, b_ref, o_ref, *, tm, tb, lc):
    x = x_ref[...]
    y = x * lax.rsqrt(jnp.mean(x * x, axis=-1, keepdims=True) + NORM_EPS) * gain_ref[...]
    i = pl.program_id(0)
    h = y * (1.0 + _row_mod(sc_ref, i, tm, tb, lc)) + _row_mod(sh_ref, i, tm, tb, lc)
    logits = jnp.dot(h, w_ref[...], preferred_element_type=F32,
                     precision=lax.Precision.HIGHEST) + b_ref[...]
    lane = lax.broadcasted_iota(jnp.int32, logits.shape, 1).astype(F32)
    neg = jnp.float32(-jnp.inf)
    logits = jnp.where(lane < N_EXPERTS, logits, neg)
    v1 = jnp.max(logits, axis=-1, keepdims=True)
    i1 = jnp.min(jnp.where(logits == v1, lane, float(LANES)), axis=-1, keepdims=True)
    rest = jnp.where(lane == i1, neg, logits)
    v2 = jnp.max(rest, axis=-1, keepdims=True)
    i2 = jnp.min(jnp.where(rest == v2, lane, float(LANES)), axis=-1, keepdims=True)
    e2 = jnp.exp(v2 - v1)
    w1 = 1.0 / (1.0 + e2)
    w2 = e2 / (1.0 + e2)
    o_ref[...] = jnp.where(lane == i1, w1, 0.0) + jnp.where(lane == i2, w2, 0.0)


def _router_gates(x, gain, layer, mod, router_pad, bias_pad, *, tb, lc, tm):
    m, d = x.shape
    return pl.pallas_call(
        functools.partial(_router_body, tm=tm, tb=tb, lc=lc),
        grid=(m // tm,),
        in_specs=[pl.BlockSpec((tm, d), lambda i: (i, 0)),
                  pl.BlockSpec((None, 1, d), lambda i: (layer, 0, 0)),
                  pl.BlockSpec((8, d), lambda i: (0, 3)),
                  pl.BlockSpec((8, d), lambda i: (0, 4)),
                  pl.BlockSpec((d, LANES), lambda i: (0, 0)),
                  pl.BlockSpec((1, LANES), lambda i: (0, 0))],
        out_specs=pl.BlockSpec((tm, LANES), lambda i: (i, 0)),
        out_shape=jax.ShapeDtypeStruct((m, LANES), F32),
        compiler_params=_cp(("arbitrary",)),
    )(x, gain, mod, mod, router_pad, bias_pad)


def _s5_tables(lam_re, lam_im, log_step, b_re, b_im, c_re, c_im):
    hp = lax.Precision.HIGHEST
    t = S5_T
    lam = lax.complex(lam_re.astype(F32), lam_im.astype(F32))
    step = jnp.exp(log_step.astype(F32))[..., None]
    lam_bar = jnp.exp(lam * step)
    bbar = ((lam_bar - 1) / lam)[..., None] * lax.complex(b_re.astype(F32), b_im.astype(F32))
    tau = jnp.arange(t + 1, dtype=F32)[None, :, None, None]
    pw = jnp.exp((lam * step)[:, None] * tau)
    c = lax.complex(c_re.astype(F32), c_im.astype(F32))

    x = pw[:, :t, :, :, None] * bbar[:, None]
    kd = (jnp.einsum('ghp,dtgpk->dtghk', c_re.astype(F32), jnp.real(x), precision=hp)
          - jnp.einsum('ghp,dtgpk->dtghk', c_im.astype(F32), jnp.imag(x), precision=hp))
    kf, kb = kd[0], kd[1]
    kall = jnp.concatenate([kb[1:][::-1], (kf[0] + kb[0])[None], kf[1:]], axis=0)
    idx = (jnp.arange(t)[None, :] - jnp.arange(t)[:, None]) + (t - 1)
    kbig = kall[idx]
    kbig = jnp.transpose(kbig, (2, 0, 4, 1, 3)).reshape(S5_G, t * S5_H, t * S5_H)

    def chunk_in(d, powers):
        gm = pw[d][powers][..., None] * bbar[d][None]
        return jnp.transpose(gm, (1, 0, 3, 2)).reshape(S5_G, t * S5_H, S5_P)

    g_f = chunk_in(0, t - 1 - jnp.arange(t))
    g_b = chunk_in(1, jnp.arange(t))

    def chunk_out(d, powers):
        cm = c[None] * jnp.transpose(pw[d][powers], (0, 1, 2))[:, :, None, :]
        return jnp.transpose(cm, (1, 3, 0, 2)).reshape(S5_G, S5_P, t * S5_H)

    c_f = chunk_out(0, 1 + jnp.arange(t))
    c_b = chunk_out(1, t - jnp.arange(t))
    gin = jnp.stack([jnp.real(g_f), jnp.imag(g_f), jnp.real(g_b), jnp.imag(g_b)], axis=1)
    cout = jnp.stack([jnp.real(c_f), -jnp.imag(c_f), jnp.real(c_b), -jnp.imag(c_b)], axis=1)
    lam_t = jnp.stack([jnp.real(pw[0, t]), jnp.imag(pw[0, t]),
                       jnp.real(pw[1, t]), jnp.imag(pw[1, t])], axis=1)
    return kbig.astype(BF16), gin.astype(BF16), cout.astype(BF16), lam_t[:, :, None, :]


def _s5_body(u_ref, k_ref, gin_ref, cout_ref, lam_ref, o_ref, g_s, h_s, *, nb, ncb, ncc):
    u = u_ref[...]
    for q in range(4):
        g_s[q] = jnp.dot(u, gin_ref[q], preferred_element_type=F32)
    lfr, lfi, lbr, lbi = lam_ref[0], lam_ref[1], lam_ref[2], lam_ref[3]
    nlc = ncb - ncc

    def step(n, carry):
        out = []
        for b in range(nb):
            fr, fi, br, bi = carry[4 * b:4 * b + 4]
            rf = b * ncb + n
            nb_idx = jnp.where(n < ncc, ncc - 1 - n, ncc + (nlc - 1) - (n - ncc))
            rb = b * ncb + nb_idx
            h_s[0, pl.ds(rf, 1), :] = fr
            h_s[1, pl.ds(rf, 1), :] = fi
            h_s[2, pl.ds(rb, 1), :] = br
            h_s[3, pl.ds(rb, 1), :] = bi
            out += [lfr * fr - lfi * fi + g_s[0, pl.ds(rf, 1), :],
                    lfr * fi + lfi * fr + g_s[1, pl.ds(rf, 1), :],
                    lbr * br - lbi * bi + g_s[2, pl.ds(rb, 1), :],
                    lbr * bi + lbi * br + g_s[3, pl.ds(rb, 1), :]]
        return tuple(out)

    zero = jnp.zeros((1, S5_P), F32)
    lax.fori_loop(0, ncb, step, (zero,) * (4 * nb))
    y = jnp.dot(u, k_ref[...], preferred_element_type=F32)
    for q in range(4):
        y = y + jnp.dot(h_s[q].astype(BF16), cout_ref[q], preferred_element_type=F32)
    o_ref[...] = y


def _s5_mix(ug, tables, *, nb, ncb, ncc):
    kbig, gin, cout, lam_t = tables
    g, nc, tw = ug.shape
    return pl.pallas_call(
        functools.partial(_s5_body, nb=nb, ncb=ncb, ncc=ncc),
        grid=(g,),
        in_specs=[pl.BlockSpec((None, nc, tw), lambda i: (i, 0, 0)),
                  pl.BlockSpec((None, tw, tw), lambda i: (i, 0, 0)),
                  pl.BlockSpec((None, 4, tw, S5_P), lambda i: (i, 0, 0, 0)),
                  pl.BlockSpec((None, 4, S5_P, tw), lambda i: (i, 0, 0, 0)),
                  pl.BlockSpec((None, 4, 1, S5_P), lambda i: (i, 0, 0, 0))],
        out_specs=pl.BlockSpec((None, nc, tw), lambda i: (i, 0, 0)),
        out_shape=jax.ShapeDtypeStruct((g, nc, tw), F32),
        scratch_shapes=[pltpu.VMEM((4, nc, S5_P), F32), pltpu.VMEM((4, nc, S5_P), F32)],
        compiler_params=_cp(("arbitrary",)),
    )(ug, kbig, gin, cout, lam_t)


def _s5_readout_body(y_ref, u_ref, d_ref, w_ref, o_ref, wb_ref):
    @pl.when(pl.program_id(0) == 0)
    def _():
        wb_ref[...] = w_ref[...].astype(BF16)

    y = _gelu_tanh(y_ref[...] + d_ref[...] * u_ref[...])
    z = jnp.dot(y.astype(BF16), wb_ref[...], preferred_element_type=F32)
    o_ref[...] = (y * _sigmoid(z)).astype(o_ref.dtype)


def _s5_readout(y, u, d_skip, w_glu, layer, *, tm):
    m = y.shape[0]
    return pl.pallas_call(
        _s5_readout_body,
        grid=(m // tm,),
        in_specs=[pl.BlockSpec((tm, S5_W), lambda i: (i, 0)),
                  pl.BlockSpec((tm, S5_W), lambda i: (i, 0)),
                  pl.BlockSpec((None, 1, S5_W), lambda i: (layer, 0, 0)),
                  pl.BlockSpec((None, S5_W, S5_W), lambda i: (layer, 0, 0))],
        out_specs=pl.BlockSpec((tm, S5_W), lambda i: (i, 0)),
        out_shape=jax.ShapeDtypeStruct((m, S5_W), BF16),
        scratch_shapes=[pltpu.VMEM((S5_W, S5_W), BF16)],
        compiler_params=_cp(("arbitrary",)),
    )(y, u, d_skip, w_glu)


def _rw_feat_body(z_ref, up_ref, dn_ref, mu_ref, wup_ref, aup_ref, gup_ref, vec_ref, ones_ref,
                  r_ref, kk_ref, v_ref, g_ref, bonus_ref, w_ref, kd_ref, bb_ref, *, tt, tb, lc):
    i = pl.program_id(0)
    start = i * tt
    b = start // tb
    pos = (start - b * tb) + lax.broadcasted_iota(jnp.int32, (tt, 1), 0)
    is_ctx = pos < lc
    tl = pos - lc
    col = tl & (GRID_W - 1)
    l_lat = tb - lc

    z = z_ref[...]
    prev = pltpu.roll(z, 1, axis=0)
    nxt = pltpu.roll(z, tt - 1, axis=0)
    if tt > GRID_W:
        up = jnp.concatenate([up_ref[...], z[:tt - GRID_W]], axis=0)
        down = jnp.concatenate([z[GRID_W:], dn_ref[...]], axis=0)
    else:
        up, down = up_ref[...], dn_ref[...]
    m_prev = jnp.logical_or(jnp.logical_and(is_ctx, pos >= 1), jnp.logical_and(tl >= 0, col >= 1))
    m_next = jnp.logical_or(jnp.logical_and(is_ctx, pos <= lc - 2),
                            jnp.logical_and(tl >= 0, col <= GRID_W - 2))
    m_up = tl >= GRID_W
    m_down = jnp.logical_and(tl >= 0, tl < l_lat - GRID_W)
    prev = jnp.where(m_prev, prev, 0.0)
    nxt = jnp.where(m_next, nxt, 0.0)
    up = jnp.where(is_ctx, prev, jnp.where(m_up, up, 0.0))
    down = jnp.where(is_ctx, nxt, jnp.where(m_down, down, 0.0))
    l4 = lax.broadcasted_iota(jnp.int32, (1, ZR_COLS), 1) & 3
    shifted = jnp.where(l4 == 0, prev, jnp.where(l4 == 1, nxt, jnp.where(l4 == 2, up, down)))
    z = z + mu_ref[...] * (shifted - z)

    r = z[:, 0:RW]
    k = z[:, RW:2 * RW]
    v = z[:, 2 * RW:3 * RW]
    lora_w = jnp.tanh(z[:, 3 * RW:3 * RW + LANES]).astype(BF16)
    lora_a = z[:, 3 * RW + LANES:3 * RW + 2 * LANES].astype(BF16)
    lora_g = _sigmoid(z[:, 3 * RW + 2 * LANES:3 * RW + 4 * LANES]).astype(BF16)
    k_k, k_a, r_k = vec_ref[0:1, :], vec_ref[1:2, :], vec_ref[2:3, :]
    ones = ones_ref[...]

    kk = k * k_k
    kk = kk / jnp.maximum(jnp.sqrt(_split_dot(kk * kk, ones)), 1e-12)
    ksum = jnp.zeros_like(k)
    for d in range(2):
        w0 = vec_ref[3 + d:4 + d, :]
        a0 = vec_ref[5 + d:6 + d, :]
        wl = w0 + jnp.dot(lora_w, wup_ref[d], preferred_element_type=F32)
        sp = jnp.maximum(-wl, 0.0) + jnp.log(1.0 + jnp.exp(-jnp.abs(wl)))
        w_ref[d] = jnp.exp(-jnp.exp(-sp - 0.5))
        a = _sigmoid(a0 + jnp.dot(lora_a, aup_ref[d], preferred_element_type=F32))
        k_d = k * (1.0 + (a - 1.0) * k_a)
        kd_ref[d] = k_d
        bb_ref[d] = kk * a
        ksum = ksum + k_d
    r_ref[...] = r
    kk_ref[...] = kk
    v_ref[...] = v
    g_ref[...] = jnp.dot(lora_g, gup_ref[...], preferred_element_type=F32)
    bonus_ref[...] = _split_dot(r * ksum * r_k, ones) * v


def _rw_features(zr, layer, mu_pad, wup, aup, gup, vecs, ones_blk, *, tb, lc, tt):
    m = zr.shape[0]
    nh = tt // GRID_W
    nblk64 = m // GRID_W
    full = pl.BlockSpec((tt, RW), lambda i: (i, 0))
    per_dir = pl.BlockSpec((2, tt, RW), lambda i: (0, i, 0))
    one = jax.ShapeDtypeStruct((m, RW), F32)
    two = jax.ShapeDtypeStruct((2, m, RW), F32)
    return pl.pallas_call(
        functools.partial(_rw_feat_body, tt=tt, tb=tb, lc=lc),
        grid=(m // tt,),
        in_specs=[pl.BlockSpec((tt, ZR_COLS), lambda i: (i, 0)),
                  pl.BlockSpec((GRID_W, ZR_COLS), lambda i: (jnp.maximum(i * nh - 1, 0), 0)),
                  pl.BlockSpec((GRID_W, ZR_COLS), lambda i: (jnp.minimum((i + 1) * nh, nblk64 - 1), 0)),
                  pl.BlockSpec((None, 1, ZR_COLS), lambda i: (layer, 0, 0)),
                  pl.BlockSpec((None, 2, LANES, RW), lambda i: (layer, 0, 0, 0)),
                  pl.BlockSpec((None, 2, LANES, RW), lambda i: (layer, 0, 0, 0)),
                  pl.BlockSpec((None, 2 * LANES, RW), lambda i: (layer, 0, 0)),
                  pl.BlockSpec((None, 8, RW), lambda i: (layer, 0, 0)),
                  pl.BlockSpec((RW, RW), lambda i: (0, 0))],
        out_specs=[full, full, full, full, full, per_dir, per_dir, per_dir],
        out_shape=[one, one, one, one, one, two, two, two],
        compiler_params=_cp(("arbitrary",)),
    )(zr, zr, zr, mu_pad, wup, aup, gup, vecs, ones_blk)


WKV_STEPS = 32
J_LO = HEAD // 2


def _wkv_body(jv_ref, v_ref, y_ref, s_ref):
    @pl.when(pl.program_id(0) == 0)
    def _():
        s_ref[...] = jnp.zeros_like(s_ref)

    def step(n, _):
        def row(q, jl):
            return jnp.broadcast_to(jv_ref[n, q, pl.ds(jl, 1), :], (HEAD, LANES))

        acc = jnp.zeros((HEAD, LANES), F32)
        for jl in range(J_LO):
            acc = acc + s_ref[jl] * row(0, jl)
        sa = acc + pltpu.roll(acc, LANES // 2, axis=1)
        v = v_ref[n]
        yacc = jnp.zeros((HEAD, LANES), F32)
        for jl in range(J_LO):
            s = s_ref[jl] * row(1, jl) + sa * row(2, jl) + v * row(3, jl)
            s_ref[jl] = s
            yacc = yacc + s * row(4, jl)
        y_ref[n] = yacc + pltpu.roll(yacc, LANES // 2, axis=1)
        return 0

    lax.fori_loop(0, WKV_STEPS, step, 0)


def _wkv_scan(jv, v):
    t = jv.shape[0]
    return pl.pallas_call(
        _wkv_body,
        grid=(t // WKV_STEPS,),
        in_specs=[pl.BlockSpec((WKV_STEPS, 5, J_LO, LANES), lambda i: (i, 0, 0, 0)),
                  pl.BlockSpec((WKV_STEPS, HEAD, LANES), lambda i: (i, 0, 0))],
        out_specs=pl.BlockSpec((WKV_STEPS, HEAD, LANES), lambda i: (i, 0, 0)),
        out_shape=jax.ShapeDtypeStruct((t, HEAD, LANES), F32),
        scratch_shapes=[pltpu.VMEM((J_LO, HEAD, LANES), F32)],
        compiler_params=_cp(("arbitrary",)),
    )(jv, v)


def _scan_order(x, lc, reverse):
    if not reverse:
        return x
    return jnp.concatenate([x[:, :lc][:, ::-1], x[:, lc:][:, ::-1]], axis=1)


def _wkv_pack(r, kk, v, w, kd, bb, *, nb, tb, lc):
    def chains(x, reverse):
        x = _scan_order(x.reshape(nb, tb, HEADS, HEAD), lc, reverse)
        return jnp.transpose(x, (1, 3, 0, 2)).reshape(tb, HEAD, nb * HEADS)

    def both(xf, xb):
        return jnp.concatenate([chains(xf, False), chains(xb, True)], axis=-1)

    def key_side(x):
        x = x.reshape(tb, 2, J_LO, LANES // 2)
        return jnp.transpose(x, (0, 2, 1, 3)).reshape(tb, J_LO, LANES)

    jv = jnp.stack([key_side(both(-kk, -kk)), key_side(both(w[0], w[1])), key_side(both(bb[0], bb[1])),
                    key_side(both(kd[0], kd[1])), key_side(both(r, r))], axis=1)
    vv = both(v, v)
    return jv, jnp.concatenate([vv, vv], axis=-1)


def _wkv_unpack(y, *, nb, tb, lc):
    c = nb * HEADS

    def nat(x, reverse):
        x = jnp.transpose(x.reshape(tb, HEAD, nb, HEADS), (2, 0, 3, 1))
        return _scan_order(x, lc, reverse).reshape(nb * tb, RW)

    return nat(y[:, :, :c], False) + nat(y[:, :, c:2 * c], True)


def _rw_readout_body(y_ref, bonus_ref, g_ref, gn_ref, ones_ref, o_ref):
    y = y_ref[...]
    ones = ones_ref[...]
    mean = _split_dot(y, ones) * (1.0 / HEAD)
    yc = y - mean
    var = _split_dot(yc * yc, ones) * (1.0 / HEAD)
    yn = yc * lax.rsqrt(var + GN_EPS) * gn_ref[0:1, :] + gn_ref[1:2, :]
    o_ref[...] = ((yn + bonus_ref[...]) * g_ref[...]).astype(o_ref.dtype)


def _rw_readout(y, bonus, g, gn, layer, ones_blk, *, tm):
    m = y.shape[0]
    full = pl.BlockSpec((tm, RW), lambda i: (i, 0))
    return pl.pallas_call(
        _rw_readout_body,
        grid=(m // tm,),
        in_specs=[full, full, full,
                  pl.BlockSpec((None, 8, RW), lambda i: (layer, 0, 0)),
                  pl.BlockSpec((RW, RW), lambda i: (0, 0))],
        out_specs=full,
        out_shape=jax.ShapeDtypeStruct((m, RW), BF16),
        compiler_params=_cp(("arbitrary",)),
    )(y, bonus, g, gn, ones_blk)


def _pad_rows(x, rows):
    return jnp.concatenate([x, jnp.zeros((rows - x.shape[0],) + x.shape[1:], x.dtype)], axis=0)


def _pick_tile(n, target):
    best = 8
    for t in range(8, target + 1, 8):
        if n % t == 0:
            best = t
    return best


def kernel(x, c, ctx, c_ctx, w_ada, b_ada, norm1, norm2, norm_f, w_in, s5_lambda_re, s5_lambda_im, s5_log_step, s5_b_re, s5_b_im, s5_c_re, s5_c_im, s5_d, s5_w_glu, rw_mu, rw_w0, rw_w_up, rw_a0, rw_a_up, rw_g_up, rw_k_k, rw_k_a, rw_r_k, rw_gn_w, rw_gn_b, w_proj_s5, w_proj_rw, w_out, ffn_w_gate, ffn_w_up, ffn_w_down, moe_router, moe_router_bias, moe_w_gate, moe_w_up, moe_w_down):
    nb, l, d = x.shape
    lc = ctx.shape[1]
    tb = lc + l
    m = nb * tb
    depth = w_in.shape[0]
    tt = lc
    assert lc % GRID_W == 0 and l % tt == 0 and l % GRID_W == 0 and nb * HEADS * 4 == LANES
    assert tb % S5_T == 0 and lc % S5_T == 0 and tb % WKV_STEPS == 0
    tm = _pick_tile(tb, 1088)
    ncb, ncc = tb // S5_T, lc // S5_T

    xs = jnp.concatenate([ctx, x], axis=1).reshape(m, d)
    silu_rows = _pad_rows(jnp.concatenate([c_ctx[None], c], axis=0), 8)
    silu_rows = (silu_rows * _sigmoid(silu_rows)).astype(BF16)

    ones_blk = (jnp.arange(RW)[:, None] // HEAD == jnp.arange(RW)[None, :] // HEAD).astype(BF16)
    mu_pad = jnp.pad(rw_mu, ((0, 0), (0, ZR_COLS - rw_mu.shape[1])))[:, None, :]
    zeros64 = jnp.zeros((depth, 1, DECAY_LORA, RW), F32)
    wup = jnp.concatenate([jnp.concatenate([rw_w_up[:, :1], zeros64], axis=2),
                           jnp.concatenate([zeros64, rw_w_up[:, 1:]], axis=2)], axis=1).astype(BF16)
    aup = jnp.concatenate([jnp.concatenate([rw_a_up[:, :1], zeros64], axis=2),
                           jnp.concatenate([zeros64, rw_a_up[:, 1:]], axis=2)], axis=1).astype(BF16)
    gup = jnp.pad(rw_g_up, ((0, 0), (0, 2 * LANES - GATE_LORA), (0, 0))).astype(BF16)
    vecs = jnp.stack([rw_k_k, rw_k_a, rw_r_k.reshape(depth, RW), rw_w0[:, 0], rw_w0[:, 1],
                      rw_a0[:, 0], rw_a0[:, 1], jnp.zeros_like(rw_k_k)], axis=1)
    gn = jnp.pad(jnp.stack([rw_gn_w, rw_gn_b], axis=1), ((0, 0), (0, 6), (0, 0)))
    d_skip = s5_d.reshape(depth, 1, S5_W)
    norm1, norm2 = norm1[:, None, :], norm2[:, None, :]
    w_gates = w_in[:, :, S5_W + 3 * RW + LORA_COLS:]

    for i in range(depth):
        mod = _matmul(silu_rows, w_ada, (i,), 6 * d, tm=8, tn=1024, bias=b_ada[i][None])

        h = _norm_mod(xs, norm1, i, mod, 0, tb=tb, lc=lc, tm=tt)
        u = _matmul(h, w_in, (i,), S5_W, tm=tm, tn=512)
        zr = _matmul(h, w_in, (i,), ZR_COLS, col_off=S5_W, tm=tm, tn=512)
        zg = _matmul(h, w_gates, (i,), 2 * d, tm=tm, tn=512)

        tables = _s5_tables(s5_lambda_re[i], s5_lambda_im[i], s5_log_step[i], s5_b_re[i], s5_b_im[i],
                            s5_c_re[i], s5_c_im[i])
        ug = jnp.transpose(u.reshape(nb * ncb, S5_T, S5_G, S5_H), (2, 0, 1, 3))
        ug = ug.reshape(S5_G, nb * ncb, S5_T * S5_H).astype(BF16)
        yg = _s5_mix(ug, tables, nb=nb, ncb=ncb, ncc=ncc)
        ys = jnp.transpose(yg.reshape(S5_G, nb * ncb, S5_T, S5_H), (1, 2, 0, 3)).reshape(m, S5_W)
        y_s5 = _s5_readout(ys, u, d_skip, s5_w_glu, i, tm=tt)

        r, kk, v, g, bonus, w, kd, bb = _rw_features(zr, i, mu_pad, wup, aup, gup, vecs, ones_blk,
                                                     tb=tb, lc=lc, tt=tt)
        jv, vv = _wkv_pack(r, kk, v, w, kd, bb, nb=nb, tb=tb, lc=lc)
        y = _wkv_unpack(_wkv_scan(jv, vv), nb=nb, tb=tb, lc=lc)
        y_rw = _rw_readout(y, bonus, g, gn, i, ones_blk, tm=tt)

        mrg = _merge(y_s5, y_rw, w_proj_s5, w_proj_rw, i, zg, tm=tm, tn=512)
        xs = _down_resid(mrg, w_out, (i,), xs, mod, 2, tb=tb, lc=lc, tm=tm, tn=512)

        j = i // 2
        if i % 2 == 0:
            h2 = _norm_mod(xs, norm2, i, mod, 3, tb=tb, lc=lc, tm=tt)
            act = _swiglu_up(h2, ffn_w_gate, ffn_w_up, (j,), tm=tm, tn=512)
            xs = _down_resid(act, ffn_w_down, (j,), xs, mod, 5, tb=tb, lc=lc, tm=_pick_tile(tb, 544), tn=512)
        else:
            h2 = _norm_mod(xs, norm2, i, mod, 3, tb=tb, lc=lc, tm=tt)
            router_pad = jnp.pad(moe_router[j], ((0, 0), (0, LANES - N_EXPERTS)))
            bias_pad = jnp.pad(moe_router_bias[j], (0, LANES - N_EXPERTS))[None]
            gates = _router_gates(xs, norm2, i, mod, router_pad, bias_pad, tb=tb, lc=lc, tm=tt)
            x_in = xs
            acc = xs
            for e in range(N_EXPERTS):
                act = _swiglu_up(h2, moe_w_gate, moe_w_up, (j, e), tm=tm, tn=256)
                acc = _down_resid(act, moe_w_down, (j, e), acc, mod, 5, tb=tb, lc=lc, tm=tm, tn=512,
                                  rowscale=gates, expert=e)
            del x_in
            xs = acc

    return _final_norm(xs.reshape(nb, tb, d), norm_f[None], lc=lc, tm=tt)
```

```python
import functools
import math

import jax
import jax.numpy as jnp
from jax import lax
from jax.experimental import pallas as pl
from jax.experimental.pallas import tpu as pltpu

F32 = jnp.float32
BF16 = jnp.bfloat16

GRID_W = 64
HEADS = 16
HEAD = 64
RW = HEADS * HEAD
S5_G = 32
S5_H = 16
S5_P = 64
S5_W = S5_G * S5_H
S5_T = 32
WKV_CHUNK = 64
DECAY_LORA = 64
ICLR_LORA = 64
GATE_LORA = 160
LORA_COLS = 2 * DECAY_LORA + 2 * ICLR_LORA + GATE_LORA
LORA_PAD = 512
ZR_COLS = 3 * RW + LORA_PAD
NORM_EPS = 1e-6
GN_EPS = 64e-5
N_EXPERTS = 8
LANES = 128
VMEM_LIMIT = 56 * 1024 * 1024


def _cp(sem):
    return pltpu.CompilerParams(dimension_semantics=sem, vmem_limit_bytes=VMEM_LIMIT)


def _sigmoid(x):
    return 1.0 / (1.0 + jnp.exp(-x))


def _gelu_tanh(x):
    return 0.5 * x * (1.0 + jnp.tanh(math.sqrt(2.0 / math.pi) * (x + 0.044715 * (x * x * x))))


def _row_mod(tab_ref, tile, tm, tb, lc):
    start = tile * tm
    b = start // tb
    pos = (start - b * tb) + lax.broadcasted_iota(jnp.int32, (tm, 1), 0)
    ctx_row = tab_ref[0:1, :]
    lat_row = tab_ref[pl.ds(1 + b, 1), :]
    return jnp.where(pos < lc, ctx_row, lat_row)


def _split_dot(x, w_bf16):
    hi = x.astype(BF16)
    lo = (x - hi.astype(F32)).astype(BF16)
    return (jnp.dot(hi, w_bf16, preferred_element_type=F32)
            + jnp.dot(lo, w_bf16, preferred_element_type=F32))


def _lead_spec(lead, shape, fn):
    return pl.BlockSpec((None,) * len(lead) + shape, lambda j, i: tuple(lead) + fn(j, i))


def _mm_body(x_ref, w_ref, *rest, has_bias):
    if has_bias:
        b_ref, o_ref, wb_ref = rest
    else:
        o_ref, wb_ref = rest

    @pl.when(pl.program_id(1) == 0)
    def _():
        wb_ref[...] = w_ref[...].astype(BF16)

    acc = jnp.dot(x_ref[...], wb_ref[...], preferred_element_type=F32)
    if has_bias:
        acc = acc + b_ref[...]
    o_ref[...] = acc.astype(o_ref.dtype)


def _matmul(x, w, lead, n_out, *, col_off=0, tm, tn, out_dtype=F32, bias=None):
    m, k = x.shape
    off = col_off // tn
    in_specs = [pl.BlockSpec((tm, k), lambda j, i: (i, 0)),
                _lead_spec(lead, (k, tn), lambda j, i: (0, j + off))]
    args = [x, w]
    if bias is not None:
        in_specs.append(pl.BlockSpec((1, tn), lambda j, i: (0, j)))
        args.append(bias)
    return pl.pallas_call(
        functools.partial(_mm_body, has_bias=bias is not None),
        grid=(n_out // tn, m // tm),
        in_specs=in_specs,
        out_specs=pl.BlockSpec((tm, tn), lambda j, i: (i, j)),
        out_shape=jax.ShapeDtypeStruct((m, n_out), out_dtype),
        scratch_shapes=[pltpu.VMEM((k, tn), BF16)],
        compiler_params=_cp(("arbitrary", "arbitrary")),
    )(*args)


def _swiglu_up_body(x_ref, wg_ref, wu_ref, o_ref, wgb_ref, wub_ref):
    @pl.when(pl.program_id(1) == 0)
    def _():
        wgb_ref[...] = wg_ref[...].astype(BF16)
        wub_ref[...] = wu_ref[...].astype(BF16)

    x = x_ref[...]
    g = jnp.dot(x, wgb_ref[...], preferred_element_type=F32)
    u = jnp.dot(x, wub_ref[...], preferred_element_type=F32)
    o_ref[...] = (g * _sigmoid(g) * u).astype(o_ref.dtype)


def _swiglu_up(x, wg, wu, lead, *, tm, tn):
    m, k = x.shape
    f = wg.shape[-1]
    wspec = _lead_spec(lead, (k, tn), lambda j, i: (0, j))
    return pl.pallas_call(
        _swiglu_up_body,
        grid=(f // tn, m // tm),
        in_specs=[pl.BlockSpec((tm, k), lambda j, i: (i, 0)), wspec, wspec],
        out_specs=pl.BlockSpec((tm, tn), lambda j, i: (i, j)),
        out_shape=jax.ShapeDtypeStruct((m, f), BF16),
        scratch_shapes=[pltpu.VMEM((k, tn), BF16), pltpu.VMEM((k, tn), BF16)],
        compiler_params=_cp(("arbitrary", "arbitrary")),
    )(x, wg, wu)


def _down_resid_body(a_ref, w_ref, x_ref, tab_ref, *rest, tm, tb, lc, expert):
    if expert is None:
        o_ref, wb_ref = rest
    else:
        rs_ref, o_ref, wb_ref = rest

    @pl.when(pl.program_id(1) == 0)
    def _():
        wb_ref[...] = w_ref[...].astype(BF16)

    y = jnp.dot(a_ref[...], wb_ref[...], preferred_element_type=F32)
    gate = _row_mod(tab_ref, pl.program_id(1), tm, tb, lc)
    if expert is not None:
        gate = gate * rs_ref[:, expert:expert + 1]
    o_ref[...] = x_ref[...] + gate * y


def _down_resid(a, w, lead, x, mod, gate_idx, *, tb, lc, tm, tn, rowscale=None, expert=None):
    m, k = a.shape
    d = x.shape[1]
    goff = gate_idx * (d // tn)
    in_specs = [pl.BlockSpec((tm, k), lambda j, i: (i, 0)),
                _lead_spec(lead, (k, tn), lambda j, i: (0, j)),
                pl.BlockSpec((tm, tn), lambda j, i: (i, j)),
                pl.BlockSpec((8, tn), lambda j, i: (0, goff + j))]
    args = [a, w, x, mod]
    if rowscale is not None:
        in_specs.append(pl.BlockSpec((tm, LANES), lambda j, i: (i, 0)))
        args.append(rowscale)
    return pl.pallas_call(
        functools.partial(_down_resid_body, tm=tm, tb=tb, lc=lc, expert=expert),
        grid=(d // tn, m // tm),
        in_specs=in_specs,
        out_specs=pl.BlockSpec((tm, tn), lambda j, i: (i, j)),
        out_shape=jax.ShapeDtypeStruct((m, d), F32),
        scratch_shapes=[pltpu.VMEM((k, tn), BF16)],
        input_output_aliases={2: 0},
        compiler_params=_cp(("arbitrary", "arbitrary")),
    )(*args)


def _merge_body(ys_ref, yr_ref, ws_ref, wr_ref, gs_ref, gr_ref, o_ref, wsb_ref, wrb_ref):
    @pl.when(pl.program_id(1) == 0)
    def _():
        wsb_ref[...] = ws_ref[...].astype(BF16)
        wrb_ref[...] = wr_ref[...].astype(BF16)

    ps = jnp.dot(ys_ref[...], wsb_ref[...], preferred_element_type=F32)
    pr = jnp.dot(yr_ref[...], wrb_ref[...], preferred_element_type=F32)
    o_ref[...] = (_sigmoid(gs_ref[...]) * ps + _sigmoid(gr_ref[...]) * pr).astype(o_ref.dtype)


def _merge(ys, yr, w_ps, w_pr, layer, zg, *, tm, tn):
    m = ys.shape[0]
    d = w_ps.shape[-1]
    nb = d // tn
    return pl.pallas_call(
        _merge_body,
        grid=(nb, m // tm),
        in_specs=[pl.BlockSpec((tm, S5_W), lambda j, i: (i, 0)),
                  pl.BlockSpec((tm, RW), lambda j, i: (i, 0)),
                  _lead_spec((layer,), (S5_W, tn), lambda j, i: (0, j)),
                  _lead_spec((layer,), (RW, tn), lambda j, i: (0, j)),
                  pl.BlockSpec((tm, tn), lambda j, i: (i, j)),
                  pl.BlockSpec((tm, tn), lambda j, i: (i, nb + j))],
        out_specs=pl.BlockSpec((tm, tn), lambda j, i: (i, j)),
        out_shape=jax.ShapeDtypeStruct((m, d), BF16),
        scratch_shapes=[pltpu.VMEM((S5_W, tn), BF16), pltpu.VMEM((RW, tn), BF16)],
        compiler_params=_cp(("arbitrary", "arbitrary")),
    )(ys, yr, w_ps, w_pr, zg, zg)


def _norm_mod_body(x_ref, gain_ref, sh_ref, sc_ref, o_ref, *, tm, tb, lc):
    x = x_ref[...]
    y = x * lax.rsqrt(jnp.mean(x * x, axis=-1, keepdims=True) + NORM_EPS) * gain_ref[...]
    i = pl.program_id(0)
    shift = _row_mod(sh_ref, i, tm, tb, lc)
    scale = _row_mod(sc_ref, i, tm, tb, lc)
    o_ref[...] = (y * (1.0 + scale) + shift).astype(o_ref.dtype)


def _norm_mod(x, gain, layer, mod, shift_idx, *, tb, lc, tm, out_dtype=BF16):
    m, d = x.shape
    return pl.pallas_call(
        functools.partial(_norm_mod_body, tm=tm, tb=tb, lc=lc),
        grid=(m // tm,),
        in_specs=[pl.BlockSpec((tm, d), lambda i: (i, 0)),
                  pl.BlockSpec((None, 1, d), lambda i: (layer, 0, 0)),
                  pl.BlockSpec((8, d), lambda i: (0, shift_idx)),
                  pl.BlockSpec((8, d), lambda i: (0, shift_idx + 1))],
        out_specs=pl.BlockSpec((tm, d), lambda i: (i, 0)),
        out_shape=jax.ShapeDtypeStruct((m, d), out_dtype),
        compiler_params=_cp(("arbitrary",)),
    )(x, gain, mod, mod)


def _final_norm_body(x_ref, gain_ref, o_ref):
    x = x_ref[...]
    o_ref[...] = x * lax.rsqrt(jnp.mean(x * x, axis=-1, keepdims=True) + NORM_EPS) * gain_ref[...]


def _final_norm(x3, gain, *, lc, tm):
    b, tb, d = x3.shape
    l = tb - lc
    skip = lc // tm
    return pl.pallas_call(
        _final_norm_body,
        grid=(b, l // tm),
        in_specs=[pl.BlockSpec((None, tm, d), lambda bi, t: (bi, skip + t, 0)),
                  pl.BlockSpec((1, d), lambda bi, t: (0, 0))],
        out_specs=pl.BlockSpec((None, tm, d), lambda bi, t: (bi, t, 0)),
        out_shape=jax.ShapeDtypeStruct((b, l, d), F32),
        compiler_params=_cp(("arbitrary", "arbitrary")),
    )(x3, gain)


def _router_body(x_ref, gain_ref, sh_ref, sc_ref, w_ref, b_ref, o_ref, *, tm, tb, lc):
    x = x_ref[...]
    y = x * lax.rsqrt(jnp.mean(x * x, axis=-1, keepdims=True) + NORM_EPS) * gain_ref[...]
    i = pl.program_id(0)
    h = y * (1.0 + _row_mod(sc_ref, i, tm, tb, lc)) + _row_mod(sh_ref, i, tm, tb, lc)
    logits = jnp.dot(h, w_ref[...], preferred_element_type=F32,
                     precision=lax.Precision.HIGHEST) + b_ref[...]
    lane = lax.broadcasted_iota(jnp.int32, logits.shape, 1).astype(F32)
    neg = jnp.float32(-jnp.inf)
    logits = jnp.where(lane < N_EXPERTS, logits, neg)
    v1 = jnp.max(logits, axis=-1, keepdims=True)
    i1 = jnp.min(jnp.where(logits == v1, lane, float(LANES)), axis=-1, keepdims=True)
    rest = jnp.where(lane == i1, neg, logits)
    v2 = jnp.max(rest, axis=-1, keepdims=True)
    i2 = jnp.min(jnp.where(rest == v2, lane, float(LANES)), axis=-1, keepdims=True)
    e2 = jnp.exp(v2 - v1)
    w1 = 1.0 / (1.0 + e2)
    w2 = e2 / (1.0 + e2)
    o_ref[...] = jnp.where(lane == i1, w1, 0.0) + jnp.where(lane == i2, w2, 0.0)


def _router_gates(x, gain, layer, mod, router_pad, bias_pad, *, tb, lc, tm):
    m, d = x.shape
    return pl.pallas_call(
        functools.partial(_router_body, tm=tm, tb=tb, lc=lc),
        grid=(m // tm,),
        in_specs=[pl.BlockSpec((tm, d), lambda i: (i, 0)),
                  pl.BlockSpec((None, 1, d), lambda i: (layer, 0, 0)),
                  pl.BlockSpec((8, d), lambda i: (0, 3)),
                  pl.BlockSpec((8, d), lambda i: (0, 4)),
                  pl.BlockSpec((d, LANES), lambda i: (0, 0)),
                  pl.BlockSpec((1, LANES), lambda i: (0, 0))],
        out_specs=pl.BlockSpec((tm, LANES), lambda i: (i, 0)),
        out_shape=jax.ShapeDtypeStruct((m, LANES), F32),
        compiler_params=_cp(("arbitrary",)),
    )(x, gain, mod, mod, router_pad, bias_pad)


def _s5_tables(lam_re, lam_im, log_step, b_re, b_im, c_re, c_im):
    hp = lax.Precision.HIGHEST
    t = S5_T
    lam_re, lam_im = lam_re.astype(F32), lam_im.astype(F32)
    b_re, b_im = b_re.astype(F32), b_im.astype(F32)
    c_re, c_im = c_re.astype(F32), c_im.astype(F32)
    step = jnp.exp(log_step.astype(F32))[..., None]
    tau = jnp.arange(t + 1, dtype=F32)[None, :, None, None]
    mag = jnp.exp((lam_re * step)[:, None] * tau)
    ang = (lam_im * step)[:, None] * tau
    pw_re, pw_im = mag * jnp.cos(ang), mag * jnp.sin(ang)
    nr, ni = pw_re[:, 1] - 1.0, pw_im[:, 1]
    den = lam_re * lam_re + lam_im * lam_im
    q_re = (nr * lam_re + ni * lam_im) / den
    q_im = (ni * lam_re - nr * lam_im) / den
    bb_re = q_re[..., None] * b_re - q_im[..., None] * b_im
    bb_im = q_re[..., None] * b_im + q_im[..., None] * b_re

    def cmul(ar, ai, br, bi):
        return ar * br - ai * bi, ar * bi + ai * br

    x_re, x_im = cmul(pw_re[:, :t, :, :, None], pw_im[:, :t, :, :, None], bb_re[:, None], bb_im[:, None])
    kd = (jnp.einsum('ghp,dtgpk->dtghk', c_re, x_re, precision=hp)
          - jnp.einsum('ghp,dtgpk->dtghk', c_im, x_im, precision=hp))
    kf, kb = kd[0], kd[1]
    kall = jnp.concatenate([kb[1:][::-1], (kf[0] + kb[0])[None], kf[1:]], axis=0)
    idx = (jnp.arange(t)[None, :] - jnp.arange(t)[:, None]) + (t - 1)
    kbig = jnp.transpose(kall[idx], (2, 0, 4, 1, 3)).reshape(S5_G, t * S5_H, t * S5_H)

    def chunk_in(d, powers):
        gr, gi = cmul(pw_re[d][powers][..., None], pw_im[d][powers][..., None], bb_re[d][None], bb_im[d][None])
        pack = lambda g: jnp.transpose(g, (1, 0, 3, 2)).reshape(S5_G, t * S5_H, S5_P)
        return pack(gr), pack(gi)

    def chunk_out(d, powers):
        cr, ci = cmul(c_re[None], c_im[None], pw_re[d][powers][:, :, None, :], pw_im[d][powers][:, :, None, :])
        pack = lambda g: jnp.transpose(g, (1, 3, 0, 2)).reshape(S5_G, S5_P, t * S5_H)
        return pack(cr), -pack(ci)

    gin = jnp.stack(chunk_in(0, t - 1 - jnp.arange(t)) + chunk_in(1, jnp.arange(t)), axis=1)
    cout = jnp.stack(chunk_out(0, 1 + jnp.arange(t)) + chunk_out(1, t - jnp.arange(t)), axis=1)
    lam_t = jnp.stack([pw_re[0, t], pw_im[0, t], pw_re[1, t], pw_im[1, t]], axis=1)
    return kbig.astype(BF16), gin.astype(BF16), cout.astype(BF16), lam_t[:, :, None, :]


def _s5_body(u_ref, k_ref, gin_ref, cout_ref, lam_ref, o_ref, g_s, h_s, *, nb, ncb, ncc):
    u = u_ref[...]
    for q in range(4):
        g_s[q] = jnp.dot(u, gin_ref[q], preferred_element_type=F32)
    lfr, lfi, lbr, lbi = lam_ref[0], lam_ref[1], lam_ref[2], lam_ref[3]
    nlc = ncb - ncc

    def step(n, carry):
        out = []
        for b in range(nb):
            fr, fi, br, bi = carry[4 * b:4 * b + 4]
            rf = b * ncb + n
            nb_idx = jnp.where(n < ncc, ncc - 1 - n, ncc + (nlc - 1) - (n - ncc))
            rb = b * ncb + nb_idx
            h_s[0, pl.ds(rf, 1), :] = fr
            h_s[1, pl.ds(rf, 1), :] = fi
            h_s[2, pl.ds(rb, 1), :] = br
            h_s[3, pl.ds(rb, 1), :] = bi
            out += [lfr * fr - lfi * fi + g_s[0, pl.ds(rf, 1), :],
                    lfr * fi + lfi * fr + g_s[1, pl.ds(rf, 1), :],
                    lbr * br - lbi * bi + g_s[2, pl.ds(rb, 1), :],
                    lbr * bi + lbi * br + g_s[3, pl.ds(rb, 1), :]]
        return tuple(out)

    zero = jnp.zeros((1, S5_P), F32)
    lax.fori_loop(0, ncb, step, (zero,) * (4 * nb))
    y = jnp.dot(u, k_ref[...], preferred_element_type=F32)
    for q in range(4):
        y = y + jnp.dot(h_s[q].astype(BF16), cout_ref[q], preferred_element_type=F32)
    o_ref[...] = y


def _s5_mix(ug, tables, *, nb, ncb, ncc):
    kbig, gin, cout, lam_t = tables
    g, nc, tw = ug.shape
    return pl.pallas_call(
        functools.partial(_s5_body, nb=nb, ncb=ncb, ncc=ncc),
        grid=(g,),
        in_specs=[pl.BlockSpec((None, nc, tw), lambda i: (i, 0, 0)),
                  pl.BlockSpec((None, tw, tw), lambda i: (i, 0, 0)),
                  pl.BlockSpec((None, 4, tw, S5_P), lambda i: (i, 0, 0, 0)),
                  pl.BlockSpec((None, 4, S5_P, tw), lambda i: (i, 0, 0, 0)),
                  pl.BlockSpec((None, 4, 1, S5_P), lambda i: (i, 0, 0, 0))],
        out_specs=pl.BlockSpec((None, nc, tw), lambda i: (i, 0, 0)),
        out_shape=jax.ShapeDtypeStruct((g, nc, tw), F32),
        scratch_shapes=[pltpu.VMEM((4, nc, S5_P), F32), pltpu.VMEM((4, nc, S5_P), F32)],
        compiler_params=_cp(("arbitrary",)),
    )(ug, kbig, gin, cout, lam_t)


def _s5_readout_body(y_ref, u_ref, d_ref, w_ref, o_ref, wb_ref):
    @pl.when(pl.program_id(0) == 0)
    def _():
        wb_ref[...] = w_ref[...].astype(BF16)

    y = _gelu_tanh(y_ref[...] + d_ref[...] * u_ref[...])
    z = jnp.dot(y.astype(BF16), wb_ref[...], preferred_element_type=F32)
    o_ref[...] = (y * _sigmoid(z)).astype(o_ref.dtype)


def _s5_readout(y, u, d_skip, w_glu, layer, *, tm):
    m = y.shape[0]
    return pl.pallas_call(
        _s5_readout_body,
        grid=(m // tm,),
        in_specs=[pl.BlockSpec((tm, S5_W), lambda i: (i, 0)),
                  pl.BlockSpec((tm, S5_W), lambda i: (i, 0)),
                  pl.BlockSpec((None, 1, S5_W), lambda i: (layer, 0, 0)),
                  pl.BlockSpec((None, S5_W, S5_W), lambda i: (layer, 0, 0))],
        out_specs=pl.BlockSpec((tm, S5_W), lambda i: (i, 0)),
        out_shape=jax.ShapeDtypeStruct((m, S5_W), BF16),
        scratch_shapes=[pltpu.VMEM((S5_W, S5_W), BF16)],
        compiler_params=_cp(("arbitrary",)),
    )(y, u, d_skip, w_glu)


def _rw_feat_body(z_ref, up_ref, dn_ref, mu_ref, wup_ref, aup_ref, gup_ref, vec_ref, ones_ref, tri_ref,
                  v_ref, g_ref, bonus_ref, ah_ref, rh_ref, bh_ref, kh_ref, bt_ref, kt_ref, pt_ref,
                  *, tt, tb, lc):
    i = pl.program_id(0)
    start = i * tt
    b = start // tb
    pos = (start - b * tb) + lax.broadcasted_iota(jnp.int32, (tt, 1), 0)
    is_ctx = pos < lc
    tl = pos - lc
    col = tl & (GRID_W - 1)
    l_lat = tb - lc

    z = z_ref[...]
    prev = pltpu.roll(z, 1, axis=0)
    nxt = pltpu.roll(z, tt - 1, axis=0)
    if tt > GRID_W:
        up = jnp.concatenate([up_ref[...], z[:tt - GRID_W]], axis=0)
        down = jnp.concatenate([z[GRID_W:], dn_ref[...]], axis=0)
    else:
        up, down = up_ref[...], dn_ref[...]
    m_prev = jnp.logical_or(jnp.logical_and(is_ctx, pos >= 1), jnp.logical_and(tl >= 0, col >= 1))
    m_next = jnp.logical_or(jnp.logical_and(is_ctx, pos <= lc - 2),
                            jnp.logical_and(tl >= 0, col <= GRID_W - 2))
    m_up = tl >= GRID_W
    m_down = jnp.logical_and(tl >= 0, tl < l_lat - GRID_W)
    prev = jnp.where(m_prev, prev, 0.0)
    nxt = jnp.where(m_next, nxt, 0.0)
    up = jnp.where(is_ctx, prev, jnp.where(m_up, up, 0.0))
    down = jnp.where(is_ctx, nxt, jnp.where(m_down, down, 0.0))
    l4 = lax.broadcasted_iota(jnp.int32, (1, ZR_COLS), 1) & 3
    shifted = jnp.where(l4 == 0, prev, jnp.where(l4 == 1, nxt, jnp.where(l4 == 2, up, down)))
    z = z + mu_ref[...] * (shifted - z)

    r = z[:, 0:RW]
    k = z[:, RW:2 * RW]
    v = z[:, 2 * RW:3 * RW]
    lora_w = jnp.tanh(z[:, 3 * RW:3 * RW + LANES]).astype(BF16)
    lora_a = z[:, 3 * RW + LANES:3 * RW + 2 * LANES].astype(BF16)
    lora_g = _sigmoid(z[:, 3 * RW + 2 * LANES:3 * RW + 4 * LANES]).astype(BF16)
    k_k, k_a, r_k = vec_ref[0:1, :], vec_ref[1:2, :], vec_ref[2:3, :]
    ones = ones_ref[...]

    kk = k * k_k
    kk = kk / jnp.maximum(jnp.sqrt(_split_dot(kk * kk, ones)), 1e-12)
    ksum = jnp.zeros_like(k)
    for d in range(2):
        w0 = vec_ref[3 + d:4 + d, :]
        a0 = vec_ref[5 + d:6 + d, :]
        wl = w0 + jnp.dot(lora_w, wup_ref[d], preferred_element_type=F32)
        sp = jnp.maximum(-wl, 0.0) + jnp.log(1.0 + jnp.exp(-jnp.abs(wl)))
        lw = -jnp.exp(-sp - 0.5)
        a = _sigmoid(a0 + jnp.dot(lora_a, aup_ref[d], preferred_element_type=F32))
        k_d = k * (1.0 + (a - 1.0) * k_a)
        bb = kk * a
        ksum = ksum + k_d
        lw_hi = lw.astype(BF16)
        lw_lo = (lw - lw_hi.astype(F32)).astype(BF16)
        cl = (jnp.dot(tri_ref[d, 0], lw_hi, preferred_element_type=F32)
              + jnp.dot(tri_ref[d, 0], lw_lo, preferred_element_type=F32))
        rem = (jnp.dot(tri_ref[d, 1], lw_hi, preferred_element_type=F32)
               + jnp.dot(tri_ref[d, 1], lw_lo, preferred_element_type=F32))
        p_in = jnp.exp(cl)
        p_inv = jnp.exp(-cl)
        p_rem = jnp.exp(rem)
        ah_ref[d] = (-kk * jnp.exp(cl - lw)).astype(BF16)
        rh_ref[d] = (r * p_in).astype(BF16)
        bh_ref[d] = (bb * p_inv).astype(BF16)
        kh_ref[d] = (k_d * p_inv).astype(BF16)
        bt_ref[d] = (bb * p_rem).astype(BF16)
        kt_ref[d] = (k_d * p_rem).astype(BF16)
        last = 0 if d == 1 else WKV_CHUNK - 1
        for cidx in range(tt // WKV_CHUNK):
            row = cidx * WKV_CHUNK + last
            pt_ref[d, cidx] = p_in[row:row + 1, :]
    v_ref[...] = v.astype(BF16)
    g_ref[...] = jnp.dot(lora_g, gup_ref[...], preferred_element_type=F32)
    bonus_ref[...] = _split_dot(r * ksum * r_k, ones) * v


def _rw_features(zr, layer, mu_pad, wup, aup, gup, vecs, ones_blk, tri, *, tb, lc, tt):
    m = zr.shape[0]
    nh = tt // GRID_W
    nblk64 = m // GRID_W
    ncht = tt // WKV_CHUNK
    full = pl.BlockSpec((tt, RW), lambda i: (i, 0))
    per_dir = pl.BlockSpec((2, tt, RW), lambda i: (0, i, 0))
    one = jax.ShapeDtypeStruct((m, RW), F32)
    two = jax.ShapeDtypeStruct((2, m, RW), BF16)
    return pl.pallas_call(
        functools.partial(_rw_feat_body, tt=tt, tb=tb, lc=lc),
        grid=(m // tt,),
        in_specs=[pl.BlockSpec((tt, ZR_COLS), lambda i: (i, 0)),
                  pl.BlockSpec((GRID_W, ZR_COLS), lambda i: (jnp.maximum(i * nh - 1, 0), 0)),
                  pl.BlockSpec((GRID_W, ZR_COLS), lambda i: (jnp.minimum((i + 1) * nh, nblk64 - 1), 0)),
                  pl.BlockSpec((None, 1, ZR_COLS), lambda i: (layer, 0, 0)),
                  pl.BlockSpec((None, 2, LANES, RW), lambda i: (layer, 0, 0, 0)),
                  pl.BlockSpec((None, 2, LANES, RW), lambda i: (layer, 0, 0, 0)),
                  pl.BlockSpec((None, 2 * LANES, RW), lambda i: (layer, 0, 0)),
                  pl.BlockSpec((None, 8, RW), lambda i: (layer, 0, 0)),
                  pl.BlockSpec((RW, RW), lambda i: (0, 0)),
                  pl.BlockSpec((2, 2, tt, tt), lambda i: (0, 0, 0, 0))],
        out_specs=[full, full, full, per_dir, per_dir, per_dir, per_dir, per_dir, per_dir,
                   pl.BlockSpec((2, ncht, 1, RW), lambda i: (0, i, 0, 0))],
        out_shape=[jax.ShapeDtypeStruct((m, RW), BF16), one, one, two, two, two, two, two, two,
                   jax.ShapeDtypeStruct((2, m // WKV_CHUNK, 1, RW), F32)],
        compiler_params=_cp(("arbitrary",)),
    )(zr, zr, zr, mu_pad, wup, aup, gup, vecs, ones_blk, tri)


def _chunk_tri(tt):
    t = jnp.arange(tt)[:, None]
    s = jnp.arange(tt)[None, :]
    same = (t // WKV_CHUNK) == (s // WKV_CHUNK)
    fwd = jnp.stack([same & (s <= t), same & (s > t)])
    bwd = jnp.stack([same & (s >= t), same & (s < t)])
    return jnp.stack([fwd, bwd]).astype(BF16)


def _dot_nt(a, b):
    return lax.dot_general(a, b, (((1,), (1,)), ((), ())), preferred_element_type=F32)


def _dot_tn(a, b):
    return lax.dot_general(a, b, (((0,), (0,)), ((), ())), preferred_element_type=F32)


def _wkv_body(*refs):
    c = WKV_CHUNK
    ins = (refs[0:8], refs[8:16])
    y_refs = refs[16:18]
    s_ref = refs[18]

    @pl.when(pl.program_id(1) == 0)
    def _():
        s_ref[...] = jnp.zeros_like(s_ref)

    t_idx = lax.broadcasted_iota(jnp.int32, (2 * c, LANES), 0)
    lane = lax.broadcasted_iota(jnp.int32, (2 * c, LANES), 1)
    diff = (t_idx & (c - 1)) - (lane & (c - 1))
    tri = (jnp.where(t_idx < c, diff, diff + 1) > 0, jnp.where(t_idx < c, -diff, 1 - diff) > 0)
    same_head = (t_idx // c) == (lane // c)
    lo = lax.broadcasted_iota(jnp.int32, (c, LANES), 1) < c

    def bd(x):
        zero = jnp.zeros_like(x)
        return jnp.concatenate([jnp.where(lo, x, zero), jnp.where(lo, zero, x)], axis=0)

    chains = [(d, p) for d in range(2) for p in range(HEADS // 2)]
    n = range(len(chains))
    sl = [slice(p * LANES, (p + 1) * LANES) for _, p in chains]
    ah, rh, bh, kh, bt, kt, vv, pt = ([ins[d][q] for d, _ in chains] for q in range(8))
    ar = [jnp.concatenate([ah[i][:, sl[i]], rh[i][:, sl[i]]], axis=0) for i in n]
    s_old = [s_ref[d, p] for d, p in chains]
    pb = [jnp.where(tri[chains[i][0]], _dot_nt(ar[i], bd(bh[i][:, sl[i]])), 0.0) for i in n]
    pk = [jnp.where(tri[chains[i][0]], _dot_nt(ar[i], bd(kh[i][:, sl[i]])), 0.0) for i in n]
    ps = [_dot_nt(ar[i], s_old[i].astype(BF16)) for i in n]
    bdv = [bd(vv[i][:, sl[i]]) for i in n]
    x = [ps[i][:c] + jnp.dot(pk[i][:c].astype(BF16), bdv[i], preferred_element_type=F32) for i in n]
    lp = [pb[i][:c].astype(BF16) for i in n]
    for it in range(6):
        x = [x[i] + jnp.dot(lp[i], bd(x[i].astype(BF16)), preferred_element_type=F32) for i in n]
        if it < 5:
            lp = [jnp.dot(lp[i], bd(lp[i]), preferred_element_type=F32).astype(BF16) for i in n]
    u = [x[i].astype(BF16) for i in n]
    for i in n:
        y_refs[chains[i][0]][:, sl[i]] = (
            ps[i][c:] + jnp.dot(pb[i][c:].astype(BF16), bd(u[i]), preferred_element_type=F32)
            + jnp.dot(pk[i][c:].astype(BF16), bdv[i], preferred_element_type=F32))
    upd = [_dot_tn(jnp.concatenate([u[i], vv[i][:, sl[i]]], axis=0),
                   jnp.concatenate([bt[i][:, sl[i]], kt[i][:, sl[i]]], axis=0)) for i in n]
    for i, (d, p) in enumerate(chains):
        s_ref[d, p] = s_old[i] * pt[i][:, sl[i]] + jnp.where(same_head, upd[i], 0.0)


def _wkv_scan(ah, rh, bh, kh, bt, kt, v, pt, *, nb, tb, lc):
    m = v.shape[0]
    c = WKV_CHUNK
    ncb, ncc = tb // c, lc // c
    nlc = ncb - ncc

    def chunk(d, b, n):
        if d == 0:
            return b * ncb + n
        return b * ncb + jnp.where(n < ncc, ncc - 1 - n, ncc + (nlc - 1) - (n - ncc))

    def specs(d):
        per_dir = pl.BlockSpec((None, c, RW), lambda b, n: (d, chunk(d, b, n), 0))
        return [per_dir] * 6 + [pl.BlockSpec((c, RW), lambda b, n: (chunk(d, b, n), 0)),
                                pl.BlockSpec((None, None, 1, RW), lambda b, n: (d, chunk(d, b, n), 0, 0))]

    out = jax.ShapeDtypeStruct((m, RW), F32)
    args = (ah, rh, bh, kh, bt, kt, v, pt)
    return pl.pallas_call(
        _wkv_body,
        grid=(nb, ncb),
        in_specs=specs(0) + specs(1),
        out_specs=[pl.BlockSpec((c, RW), lambda b, n: (chunk(0, b, n), 0)),
                   pl.BlockSpec((c, RW), lambda b, n: (chunk(1, b, n), 0))],
        out_shape=[out, out],
        scratch_shapes=[pltpu.VMEM((2, HEADS // 2, LANES, LANES), F32)],
        compiler_params=_cp(("arbitrary", "arbitrary")),
    )(*args, *args)


def _rw_readout_body(yf_ref, yb_ref, bonus_ref, g_ref, gn_ref, ones_ref, o_ref):
    y = yf_ref[...] + yb_ref[...]
    ones = ones_ref[...]
    mean = _split_dot(y, ones) * (1.0 / HEAD)
    yc = y - mean
    var = _split_dot(yc * yc, ones) * (1.0 / HEAD)
    yn = yc * lax.rsqrt(var + GN_EPS) * gn_ref[0:1, :] + gn_ref[1:2, :]
    o_ref[...] = ((yn + bonus_ref[...]) * g_ref[...]).astype(o_ref.dtype)


def _rw_readout(y_f, y_b, bonus, g, gn, layer, ones_blk, *, tm):
    m = y_f.shape[0]
    full = pl.BlockSpec((tm, RW), lambda i: (i, 0))
    return pl.pallas_call(
        _rw_readout_body,
        grid=(m // tm,),
        in_specs=[full, full, full, full,
                  pl.BlockSpec((None, 8, RW), lambda i: (layer, 0, 0)),
                  pl.BlockSpec((RW, RW), lambda i: (0, 0))],
        out_specs=full,
        out_shape=jax.ShapeDtypeStruct((m, RW), BF16),
        compiler_params=_cp(("arbitrary",)),
    )(y_f, y_b, bonus, g, gn, ones_blk)


def _pad_rows(x, rows):
    return jnp.concatenate([x, jnp.zeros((rows - x.shape[0],) + x.shape[1:], x.dtype)], axis=0)


def _pick_tile(n, target):
    best = 8
    for t in range(8, target + 1, 8):
        if n % t == 0:
            best = t
    return best


def kernel(x, c, ctx, c_ctx, w_ada, b_ada, norm1, norm2, norm_f, w_in, s5_lambda_re, s5_lambda_im, s5_log_step, s5_b_re, s5_b_im, s5_c_re, s5_c_im, s5_d, s5_w_glu, rw_mu, rw_w0, rw_w_up, rw_a0, rw_a_up, rw_g_up, rw_k_k, rw_k_a, rw_r_k, rw_gn_w, rw_gn_b, w_proj_s5, w_proj_rw, w_out, ffn_w_gate, ffn_w_up, ffn_w_down, moe_router, moe_router_bias, moe_w_gate, moe_w_up, moe_w_down):
    nb, l, d = x.shape
    lc = ctx.shape[1]
    tb = lc + l
    m = nb * tb
    depth = w_in.shape[0]
    tt = lc
    assert lc % GRID_W == 0 and l % tt == 0 and l % GRID_W == 0 and nb * HEADS * 4 == LANES
    assert tb % S5_T == 0 and lc % S5_T == 0 and tt % WKV_CHUNK == 0
    tm = _pick_tile(tb, 1088)
    ncb, ncc = tb // S5_T, lc // S5_T

    xs = jnp.concatenate([ctx, x], axis=1).reshape(m, d)
    silu_rows = _pad_rows(jnp.concatenate([c_ctx[None], c], axis=0), 8)
    silu_rows = (silu_rows * _sigmoid(silu_rows)).astype(BF16)

    ones_blk = (jnp.arange(RW)[:, None] // HEAD == jnp.arange(RW)[None, :] // HEAD).astype(BF16)
    tri = _chunk_tri(tt)
    mu_pad = jnp.pad(rw_mu, ((0, 0), (0, ZR_COLS - rw_mu.shape[1])))[:, None, :]
    zeros64 = jnp.zeros((depth, 1, DECAY_LORA, RW), F32)
    wup = jnp.concatenate([jnp.concatenate([rw_w_up[:, :1], zeros64], axis=2),
                           jnp.concatenate([zeros64, rw_w_up[:, 1:]], axis=2)], axis=1).astype(BF16)
    aup = jnp.concatenate([jnp.concatenate([rw_a_up[:, :1], zeros64], axis=2),
                           jnp.concatenate([zeros64, rw_a_up[:, 1:]], axis=2)], axis=1).astype(BF16)
    gup = jnp.pad(rw_g_up, ((0, 0), (0, 2 * LANES - GATE_LORA), (0, 0))).astype(BF16)
    vecs = jnp.stack([rw_k_k, rw_k_a, rw_r_k.reshape(depth, RW), rw_w0[:, 0], rw_w0[:, 1],
                      rw_a0[:, 0], rw_a0[:, 1], jnp.zeros_like(rw_k_k)], axis=1)
    gn = jnp.pad(jnp.stack([rw_gn_w, rw_gn_b], axis=1), ((0, 0), (0, 6), (0, 0)))
    d_skip = s5_d.reshape(depth, 1, S5_W)
    norm1, norm2 = norm1[:, None, :], norm2[:, None, :]
    w_gates = w_in[:, :, S5_W + 3 * RW + LORA_COLS:]

    for i in range(depth):
        mod = _matmul(silu_rows, w_ada, (i,), 6 * d, tm=8, tn=1024, bias=b_ada[i][None])

        h = _norm_mod(xs, norm1, i, mod, 0, tb=tb, lc=lc, tm=tt)
        u = _matmul(h, w_in, (i,), S5_W, tm=tm, tn=512)
        zr = _matmul(h, w_in, (i,), ZR_COLS, col_off=S5_W, tm=tm, tn=512)
        zg = _matmul(h, w_gates, (i,), 2 * d, tm=tm, tn=512)

        tables = _s5_tables(s5_lambda_re[i], s5_lambda_im[i], s5_log_step[i], s5_b_re[i], s5_b_im[i],
                            s5_c_re[i], s5_c_im[i])
        ug = jnp.transpose(u.reshape(nb * ncb, S5_T, S5_G, S5_H), (2, 0, 1, 3))
        ug = ug.reshape(S5_G, nb * ncb, S5_T * S5_H).astype(BF16)
        yg = _s5_mix(ug, tables, nb=nb, ncb=ncb, ncc=ncc)
        ys = jnp.transpose(yg.reshape(S5_G, nb * ncb, S5_T, S5_H), (1, 2, 0, 3)).reshape(m, S5_W)
        y_s5 = _s5_readout(ys, u, d_skip, s5_w_glu, i, tm=tt)

        v, g, bonus, ah, rh, bh, kh, bt, kt, pt = _rw_features(zr, i, mu_pad, wup, aup, gup, vecs, ones_blk, tri,
                                                               tb=tb, lc=lc, tt=tt)
        y_f, y_b = _wkv_scan(ah, rh, bh, kh, bt, kt, v, pt, nb=nb, tb=tb, lc=lc)
        y_rw = _rw_readout(y_f, y_b, bonus, g, gn, i, ones_blk, tm=tt)

        mrg = _merge(y_s5, y_rw, w_proj_s5, w_proj_rw, i, zg, tm=tm, tn=512)
        xs = _down_resid(mrg, w_out, (i,), xs, mod, 2, tb=tb, lc=lc, tm=tm, tn=512)

        j = i // 2
        h2 = _norm_mod(xs, norm2, i, mod, 3, tb=tb, lc=lc, tm=tt)
        if i % 2 == 0:
            act = _swiglu_up(h2, ffn_w_gate, ffn_w_up, (j,), tm=tm, tn=512)
            xs = _down_resid(act, ffn_w_down, (j,), xs, mod, 5, tb=tb, lc=lc, tm=_pick_tile(tb, 544), tn=512)
        else:
            router_pad = jnp.pad(moe_router[j], ((0, 0), (0, LANES - N_EXPERTS)))
            bias_pad = jnp.pad(moe_router_bias[j], (0, LANES - N_EXPERTS))[None]
            gates = _router_gates(xs, norm2, i, mod, router_pad, bias_pad, tb=tb, lc=lc, tm=tt)
            for e in range(N_EXPERTS):
                act = _swiglu_up(h2, moe_w_gate, moe_w_up, (j, e), tm=tm, tn=256)
                xs = _down_resid(act, moe_w_down, (j, e), xs, mod, 5, tb=tb, lc=lc, tm=tm, tn=512,
                                 rowscale=gates, expert=e)

    return _final_norm(xs.reshape(nb, tb, d), norm_f[None], lc=lc, tm=tt)
```

```python
import functools
import math

import jax
import jax.numpy as jnp
from jax import lax
from jax.experimental import pallas as pl
from jax.experimental.pallas import tpu as pltpu

F32 = jnp.float32
BF16 = jnp.bfloat16

GRID_W = 64
HEADS = 16
HEAD = 64
RW = HEADS * HEAD
S5_G = 32
S5_H = 16
S5_P = 64
S5_W = S5_G * S5_H
S5_T = 32
WKV_CHUNK = 64
DECAY_LORA = 64
ICLR_LORA = 64
GATE_LORA = 160
LORA_COLS = 2 * DECAY_LORA + 2 * ICLR_LORA + GATE_LORA
LORA_PAD = 512
ZR_COLS = 3 * RW + LORA_PAD
NORM_EPS = 1e-6
GN_EPS = 64e-5
N_EXPERTS = 8
LANES = 128
VMEM_LIMIT = 56 * 1024 * 1024


def _cp(sem):
    return pltpu.CompilerParams(dimension_semantics=sem, vmem_limit_bytes=VMEM_LIMIT)


def _sigmoid(x):
    return 1.0 / (1.0 + jnp.exp(-x))


def _gelu_tanh(x):
    return 0.5 * x * (1.0 + jnp.tanh(math.sqrt(2.0 / math.pi) * (x + 0.044715 * (x * x * x))))


def _row_mod(tab_ref, tile, tm, tb, lc):
    start = tile * tm
    b = start // tb
    pos = (start - b * tb) + lax.broadcasted_iota(jnp.int32, (tm, 1), 0)
    ctx_row = tab_ref[0:1, :]
    lat_row = tab_ref[pl.ds(1 + b, 1), :]
    return jnp.where(pos < lc, ctx_row, lat_row)


def _split_dot(x, w_bf16):
    hi = x.astype(BF16)
    lo = (x - hi.astype(F32)).astype(BF16)
    return (jnp.dot(hi, w_bf16, preferred_element_type=F32)
            + jnp.dot(lo, w_bf16, preferred_element_type=F32))


def _lead_spec(lead, shape, fn):
    return pl.BlockSpec((None,) * len(lead) + shape, lambda j, i: tuple(lead) + fn(j, i))


def _mm_body(x_ref, w_ref, *rest, has_bias):
    if has_bias:
        b_ref, o_ref, wb_ref = rest
    else:
        o_ref, wb_ref = rest

    @pl.when(pl.program_id(1) == 0)
    def _():
        wb_ref[...] = w_ref[...].astype(BF16)

    acc = jnp.dot(x_ref[...], wb_ref[...], preferred_element_type=F32)
    if has_bias:
        acc = acc + b_ref[...]
    o_ref[...] = acc.astype(o_ref.dtype)


def _matmul(x, w, lead, n_out, *, col_off=0, tm, tn, out_dtype=F32, bias=None):
    m, k = x.shape
    off = col_off // tn
    in_specs = [pl.BlockSpec((tm, k), lambda j, i: (i, 0)),
                _lead_spec(lead, (k, tn), lambda j, i: (0, j + off))]
    args = [x, w]
    if bias is not None:
        in_specs.append(pl.BlockSpec((1, tn), lambda j, i: (0, j)))
        args.append(bias)
    return pl.pallas_call(
        functools.partial(_mm_body, has_bias=bias is not None),
        grid=(n_out // tn, m // tm),
        in_specs=in_specs,
        out_specs=pl.BlockSpec((tm, tn), lambda j, i: (i, j)),
        out_shape=jax.ShapeDtypeStruct((m, n_out), out_dtype),
        scratch_shapes=[pltpu.VMEM((k, tn), BF16)],
        compiler_params=_cp(("arbitrary", "arbitrary")),
    )(*args)


def _swiglu_up_body(x_ref, wg_ref, wu_ref, o_ref, wgb_ref, wub_ref):
    @pl.when(pl.program_id(1) == 0)
    def _():
        wgb_ref[...] = wg_ref[...].astype(BF16)
        wub_ref[...] = wu_ref[...].astype(BF16)

    x = x_ref[...]
    g = jnp.dot(x, wgb_ref[...], preferred_element_type=F32)
    u = jnp.dot(x, wub_ref[...], preferred_element_type=F32)
    o_ref[...] = (g * _sigmoid(g) * u).astype(o_ref.dtype)


def _swiglu_up(x, wg, wu, lead, *, tm, tn):
    m, k = x.shape
    f = wg.shape[-1]
    wspec = _lead_spec(lead, (k, tn), lambda j, i: (0, j))
    return pl.pallas_call(
        _swiglu_up_body,
        grid=(f // tn, m // tm),
        in_specs=[pl.BlockSpec((tm, k), lambda j, i: (i, 0)), wspec, wspec],
        out_specs=pl.BlockSpec((tm, tn), lambda j, i: (i, j)),
        out_shape=jax.ShapeDtypeStruct((m, f), BF16),
        scratch_shapes=[pltpu.VMEM((k, tn), BF16), pltpu.VMEM((k, tn), BF16)],
        compiler_params=_cp(("arbitrary", "arbitrary")),
    )(x, wg, wu)


def _down_resid_body(a_ref, w_ref, x_ref, tab_ref, *rest, tm, tb, lc, expert):
    if expert is None:
        o_ref, wb_ref = rest
    else:
        rs_ref, o_ref, wb_ref = rest

    @pl.when(pl.program_id(1) == 0)
    def _():
        wb_ref[...] = w_ref[...].astype(BF16)

    y = jnp.dot(a_ref[...], wb_ref[...], preferred_element_type=F32)
    gate = _row_mod(tab_ref, pl.program_id(1), tm, tb, lc)
    if expert is not None:
        gate = gate * rs_ref[:, expert:expert + 1]
    o_ref[...] = x_ref[...] + gate * y


def _down_resid(a, w, lead, x, mod, gate_idx, *, tb, lc, tm, tn, rowscale=None, expert=None):
    m, k = a.shape
    d = x.shape[1]
    goff = gate_idx * (d // tn)
    in_specs = [pl.BlockSpec((tm, k), lambda j, i: (i, 0)),
                _lead_spec(lead, (k, tn), lambda j, i: (0, j)),
                pl.BlockSpec((tm, tn), lambda j, i: (i, j)),
                pl.BlockSpec((8, tn), lambda j, i: (0, goff + j))]
    args = [a, w, x, mod]
    if rowscale is not None:
        in_specs.append(pl.BlockSpec((tm, LANES), lambda j, i: (i, 0)))
        args.append(rowscale)
    return pl.pallas_call(
        functools.partial(_down_resid_body, tm=tm, tb=tb, lc=lc, expert=expert),
        grid=(d // tn, m // tm),
        in_specs=in_specs,
        out_specs=pl.BlockSpec((tm, tn), lambda j, i: (i, j)),
        out_shape=jax.ShapeDtypeStruct((m, d), F32),
        scratch_shapes=[pltpu.VMEM((k, tn), BF16)],
        input_output_aliases={2: 0},
        compiler_params=_cp(("arbitrary", "arbitrary")),
    )(*args)


def _merge_body(ys_ref, yr_ref, ws_ref, wr_ref, gs_ref, gr_ref, o_ref, wsb_ref, wrb_ref):
    @pl.when(pl.program_id(1) == 0)
    def _():
        wsb_ref[...] = ws_ref[...].astype(BF16)
        wrb_ref[...] = wr_ref[...].astype(BF16)

    ps = jnp.dot(ys_ref[...], wsb_ref[...], preferred_element_type=F32)
    pr = jnp.dot(yr_ref[...], wrb_ref[...], preferred_element_type=F32)
    o_ref[...] = (_sigmoid(gs_ref[...]) * ps + _sigmoid(gr_ref[...]) * pr).astype(o_ref.dtype)


def _merge(ys, yr, w_ps, w_pr, layer, zg, *, tm, tn):
    m = ys.shape[0]
    d = w_ps.shape[-1]
    nb = d // tn
    return pl.pallas_call(
        _merge_body,
        grid=(nb, m // tm),
        in_specs=[pl.BlockSpec((tm, S5_W), lambda j, i: (i, 0)),
                  pl.BlockSpec((tm, RW), lambda j, i: (i, 0)),
                  _lead_spec((layer,), (S5_W, tn), lambda j, i: (0, j)),
                  _lead_spec((layer,), (RW, tn), lambda j, i: (0, j)),
                  pl.BlockSpec((tm, tn), lambda j, i: (i, j)),
                  pl.BlockSpec((tm, tn), lambda j, i: (i, nb + j))],
        out_specs=pl.BlockSpec((tm, tn), lambda j, i: (i, j)),
        out_shape=jax.ShapeDtypeStruct((m, d), BF16),
        scratch_shapes=[pltpu.VMEM((S5_W, tn), BF16), pltpu.VMEM((RW, tn), BF16)],
        compiler_params=_cp(("arbitrary", "arbitrary")),
    )(ys, yr, w_ps, w_pr, zg, zg)


def _norm_mod_body(x_ref, gain_ref, sh_ref, sc_ref, o_ref, *, tm, tb, lc):
    x = x_ref[...]
    y = x * lax.rsqrt(jnp.mean(x * x, axis=-1, keepdims=True) + NORM_EPS) * gain_ref[...]
    i = pl.program_id(0)
    shift = _row_mod(sh_ref, i, tm, tb, lc)
    scale = _row_mod(sc_ref, i, tm, tb, lc)
    o_ref[...] = (y * (1.0 + scale) + shift).astype(o_ref.dtype)


def _norm_mod(x, gain, layer, mod, shift_idx, *, tb, lc, tm, out_dtype=BF16):
    m, d = x.shape
    return pl.pallas_call(
        functools.partial(_norm_mod_body, tm=tm, tb=tb, lc=lc),
        grid=(m // tm,),
        in_specs=[pl.BlockSpec((tm, d), lambda i: (i, 0)),
                  pl.BlockSpec((None, 1, d), lambda i: (layer, 0, 0)),
                  pl.BlockSpec((8, d), lambda i: (0, shift_idx)),
                  pl.BlockSpec((8, d), lambda i: (0, shift_idx + 1))],
        out_specs=pl.BlockSpec((tm, d), lambda i: (i, 0)),
        out_shape=jax.ShapeDtypeStruct((m, d), out_dtype),
        compiler_params=_cp(("arbitrary",)),
    )(x, gain, mod, mod)


def _final_norm_body(x_ref, gain_ref, o_ref):
    x = x_ref[...]
    o_ref[...] = x * lax.rsqrt(jnp.mean(x * x, axis=-1, keepdims=True) + NORM_EPS) * gain_ref[...]


def _final_norm(x3, gain, *, lc, tm):
    b, tb, d = x3.shape
    l = tb - lc
    skip = lc // tm
    return pl.pallas_call(
        _final_norm_body,
        grid=(b, l // tm),
        in_specs=[pl.BlockSpec((None, tm, d), lambda bi, t: (bi, skip + t, 0)),
                  pl.BlockSpec((1, d), lambda bi, t: (0, 0))],
        out_specs=pl.BlockSpec((None, tm, d), lambda bi, t: (bi, t, 0)),
        out_shape=jax.ShapeDtypeStruct((b, l, d), F32),
        compiler_params=_cp(("arbitrary", "arbitrary")),
    )(x3, gain)


def _router_body(x_ref, gain_ref, sh_ref, sc_ref, w_ref, b_ref, o_ref, *, tm, tb, lc):
    x = x_ref[...]
    y = x * lax.rsqrt(jnp.mean(x * x, axis=-1, keepdims=True) + NORM_EPS) * gain_ref[...]
    i = pl.program_id(0)
    h = y * (1.0 + _row_mod(sc_ref, i, tm, tb, lc)) + _row_mod(sh_ref, i, tm, tb, lc)
    logits = jnp.dot(h, w_ref[...], preferred_element_type=F32,
                     precision=lax.Precision.HIGHEST) + b_ref[...]
    lane = lax.broadcasted_iota(jnp.int32, logits.shape, 1).astype(F32)
    neg = jnp.float32(-jnp.inf)
    logits = jnp.where(lane < N_EXPERTS, logits, neg)
    v1 = jnp.max(logits, axis=-1, keepdims=True)
    i1 = jnp.min(jnp.where(logits == v1, lane, float(LANES)), axis=-1, keepdims=True)
    rest = jnp.where(lane == i1, neg, logits)
    v2 = jnp.max(rest, axis=-1, keepdims=True)
    i2 = jnp.min(jnp.where(rest == v2, lane, float(LANES)), axis=-1, keepdims=True)
    e2 = jnp.exp(v2 - v1)
    w1 = 1.0 / (1.0 + e2)
    w2 = e2 / (1.0 + e2)
    o_ref[...] = (jnp.where(lane == i1, w1, 0.0) + jnp.where(lane == i2, w2, 0.0)
                  + jnp.where(lane == N_EXPERTS, i1, 0.0) + jnp.where(lane == N_EXPERTS + 1, i2, 0.0)
                  + jnp.where(lane == N_EXPERTS + 2, w1, 0.0) + jnp.where(lane == N_EXPERTS + 3, w2, 0.0))


def _router_gates(x, gain, layer, mod, router_pad, bias_pad, *, tb, lc, tm):
    m, d = x.shape
    return pl.pallas_call(
        functools.partial(_router_body, tm=tm, tb=tb, lc=lc),
        grid=(m // tm,),
        in_specs=[pl.BlockSpec((tm, d), lambda i: (i, 0)),
                  pl.BlockSpec((None, 1, d), lambda i: (layer, 0, 0)),
                  pl.BlockSpec((8, d), lambda i: (0, 3)),
                  pl.BlockSpec((8, d), lambda i: (0, 4)),
                  pl.BlockSpec((d, LANES), lambda i: (0, 0)),
                  pl.BlockSpec((1, LANES), lambda i: (0, 0))],
        out_specs=pl.BlockSpec((tm, LANES), lambda i: (i, 0)),
        out_shape=jax.ShapeDtypeStruct((m, LANES), F32),
        compiler_params=_cp(("arbitrary",)),
    )(x, gain, mod, mod, router_pad, bias_pad)


def _s5_tables(lam_re, lam_im, log_step, b_re, b_im, c_re, c_im):
    hp = lax.Precision.HIGHEST
    t = S5_T
    lam_re, lam_im = lam_re.astype(F32), lam_im.astype(F32)
    b_re, b_im = b_re.astype(F32), b_im.astype(F32)
    c_re, c_im = c_re.astype(F32), c_im.astype(F32)
    step = jnp.exp(log_step.astype(F32))[..., None]
    tau = jnp.arange(t + 1, dtype=F32)[None, :, None, None]
    mag = jnp.exp((lam_re * step)[:, None] * tau)
    ang = (lam_im * step)[:, None] * tau
    pw_re, pw_im = mag * jnp.cos(ang), mag * jnp.sin(ang)
    nr, ni = pw_re[:, 1] - 1.0, pw_im[:, 1]
    den = lam_re * lam_re + lam_im * lam_im
    q_re = (nr * lam_re + ni * lam_im) / den
    q_im = (ni * lam_re - nr * lam_im) / den
    bb_re = q_re[..., None] * b_re - q_im[..., None] * b_im
    bb_im = q_re[..., None] * b_im + q_im[..., None] * b_re

    def cmul(ar, ai, br, bi):
        return ar * br - ai * bi, ar * bi + ai * br

    x_re, x_im = cmul(pw_re[:, :t, :, :, None], pw_im[:, :t, :, :, None], bb_re[:, None], bb_im[:, None])
    kd = (jnp.einsum('ghp,dtgpk->dtghk', c_re, x_re, precision=hp)
          - jnp.einsum('ghp,dtgpk->dtghk', c_im, x_im, precision=hp))
    kf, kb = kd[0], kd[1]
    kall = jnp.concatenate([kb[1:][::-1], (kf[0] + kb[0])[None], kf[1:]], axis=0)
    idx = (jnp.arange(t)[None, :] - jnp.arange(t)[:, None]) + (t - 1)
    kbig = jnp.transpose(kall[idx], (2, 0, 4, 1, 3)).reshape(S5_G, t * S5_H, t * S5_H)

    def chunk_in(d, powers):
        gr, gi = cmul(pw_re[d][powers][..., None], pw_im[d][powers][..., None], bb_re[d][None], bb_im[d][None])
        pack = lambda g: jnp.transpose(g, (1, 0, 3, 2)).reshape(S5_G, t * S5_H, S5_P)
        return pack(gr), pack(gi)

    def chunk_out(d, powers):
        cr, ci = cmul(c_re[None], c_im[None], pw_re[d][powers][:, :, None, :], pw_im[d][powers][:, :, None, :])
        pack = lambda g: jnp.transpose(g, (1, 3, 0, 2)).reshape(S5_G, S5_P, t * S5_H)
        return pack(cr), -pack(ci)

    gin = jnp.stack(chunk_in(0, t - 1 - jnp.arange(t)) + chunk_in(1, jnp.arange(t)), axis=1)
    cout = jnp.stack(chunk_out(0, 1 + jnp.arange(t)) + chunk_out(1, t - jnp.arange(t)), axis=1)
    lam_t = jnp.stack([pw_re[0, t], pw_im[0, t], pw_re[1, t], pw_im[1, t]], axis=1)
    return kbig.astype(BF16), gin.astype(BF16), cout.astype(BF16), lam_t[:, :, None, :]


def _s5_body(u_ref, k_ref, gin_ref, cout_ref, lam_ref, o_ref, g_s, h_s, *, nb, ncb, ncc):
    u = u_ref[...]
    for q in range(4):
        g_s[q] = jnp.dot(u, gin_ref[q], preferred_element_type=F32)
    lfr, lfi, lbr, lbi = lam_ref[0], lam_ref[1], lam_ref[2], lam_ref[3]
    nlc = ncb - ncc

    def step(n, carry):
        out = []
        for b in range(nb):
            fr, fi, br, bi = carry[4 * b:4 * b + 4]
            rf = b * ncb + n
            nb_idx = jnp.where(n < ncc, ncc - 1 - n, ncc + (nlc - 1) - (n - ncc))
            rb = b * ncb + nb_idx
            h_s[0, pl.ds(rf, 1), :] = fr
            h_s[1, pl.ds(rf, 1), :] = fi
            h_s[2, pl.ds(rb, 1), :] = br
            h_s[3, pl.ds(rb, 1), :] = bi
            out += [lfr * fr - lfi * fi + g_s[0, pl.ds(rf, 1), :],
                    lfr * fi + lfi * fr + g_s[1, pl.ds(rf, 1), :],
                    lbr * br - lbi * bi + g_s[2, pl.ds(rb, 1), :],
                    lbr * bi + lbi * br + g_s[3, pl.ds(rb, 1), :]]
        return tuple(out)

    zero = jnp.zeros((1, S5_P), F32)
    lax.fori_loop(0, ncb, step, (zero,) * (4 * nb))
    y = jnp.dot(u, k_ref[...], preferred_element_type=F32)
    for q in range(4):
        y = y + jnp.dot(h_s[q].astype(BF16), cout_ref[q], preferred_element_type=F32)
    o_ref[...] = y


def _s5_mix(ug, tables, *, nb, ncb, ncc):
    kbig, gin, cout, lam_t = tables
    g, nc, tw = ug.shape
    return pl.pallas_call(
        functools.partial(_s5_body, nb=nb, ncb=ncb, ncc=ncc),
        grid=(g,),
        in_specs=[pl.BlockSpec((None, nc, tw), lambda i: (i, 0, 0)),
                  pl.BlockSpec((None, tw, tw), lambda i: (i, 0, 0)),
                  pl.BlockSpec((None, 4, tw, S5_P), lambda i: (i, 0, 0, 0)),
                  pl.BlockSpec((None, 4, S5_P, tw), lambda i: (i, 0, 0, 0)),
                  pl.BlockSpec((None, 4, 1, S5_P), lambda i: (i, 0, 0, 0))],
        out_specs=pl.BlockSpec((None, nc, tw), lambda i: (i, 0, 0)),
        out_shape=jax.ShapeDtypeStruct((g, nc, tw), F32),
        scratch_shapes=[pltpu.VMEM((4, nc, S5_P), F32), pltpu.VMEM((4, nc, S5_P), F32)],
        compiler_params=_cp(("arbitrary",)),
    )(ug, kbig, gin, cout, lam_t)


def _s5_readout_body(y_ref, u_ref, d_ref, w_ref, o_ref, wb_ref):
    @pl.when(pl.program_id(0) == 0)
    def _():
        wb_ref[...] = w_ref[...].astype(BF16)

    y = _gelu_tanh(y_ref[...] + d_ref[...] * u_ref[...])
    z = jnp.dot(y.astype(BF16), wb_ref[...], preferred_element_type=F32)
    o_ref[...] = (y * _sigmoid(z)).astype(o_ref.dtype)


def _s5_readout(y, u, d_skip, w_glu, layer, *, tm):
    m = y.shape[0]
    return pl.pallas_call(
        _s5_readout_body,
        grid=(m // tm,),
        in_specs=[pl.BlockSpec((tm, S5_W), lambda i: (i, 0)),
                  pl.BlockSpec((tm, S5_W), lambda i: (i, 0)),
                  pl.BlockSpec((None, 1, S5_W), lambda i: (layer, 0, 0)),
                  pl.BlockSpec((None, S5_W, S5_W), lambda i: (layer, 0, 0))],
        out_specs=pl.BlockSpec((tm, S5_W), lambda i: (i, 0)),
        out_shape=jax.ShapeDtypeStruct((m, S5_W), BF16),
        scratch_shapes=[pltpu.VMEM((S5_W, S5_W), BF16)],
        compiler_params=_cp(("arbitrary",)),
    )(y, u, d_skip, w_glu)


def _rw_feat_body(z_ref, up_ref, dn_ref, mu_ref, wup_ref, aup_ref, gup_ref, vec_ref, ones_ref, tri_ref,
                  v_ref, g_ref, bonus_ref, ah_ref, rh_ref, bh_ref, kh_ref, bt_ref, kt_ref, pt_ref,
                  *, tt, tb, lc):
    i = pl.program_id(0)
    start = i * tt
    b = start // tb
    pos = (start - b * tb) + lax.broadcasted_iota(jnp.int32, (tt, 1), 0)
    is_ctx = pos < lc
    tl = pos - lc
    col = tl & (GRID_W - 1)
    l_lat = tb - lc

    z = z_ref[...]
    prev = pltpu.roll(z, 1, axis=0)
    nxt = pltpu.roll(z, tt - 1, axis=0)
    if tt > GRID_W:
        up = jnp.concatenate([up_ref[...], z[:tt - GRID_W]], axis=0)
        down = jnp.concatenate([z[GRID_W:], dn_ref[...]], axis=0)
    else:
        up, down = up_ref[...], dn_ref[...]
    m_prev = jnp.logical_or(jnp.logical_and(is_ctx, pos >= 1), jnp.logical_and(tl >= 0, col >= 1))
    m_next = jnp.logical_or(jnp.logical_and(is_ctx, pos <= lc - 2),
                            jnp.logical_and(tl >= 0, col <= GRID_W - 2))
    m_up = tl >= GRID_W
    m_down = jnp.logical_and(tl >= 0, tl < l_lat - GRID_W)
    prev = jnp.where(m_prev, prev, 0.0)
    nxt = jnp.where(m_next, nxt, 0.0)
    up = jnp.where(is_ctx, prev, jnp.where(m_up, up, 0.0))
    down = jnp.where(is_ctx, nxt, jnp.where(m_down, down, 0.0))
    l4 = lax.broadcasted_iota(jnp.int32, (1, ZR_COLS), 1) & 3
    shifted = jnp.where(l4 == 0, prev, jnp.where(l4 == 1, nxt, jnp.where(l4 == 2, up, down)))
    z = z + mu_ref[...] * (shifted - z)

    r = z[:, 0:RW]
    k = z[:, RW:2 * RW]
    v = z[:, 2 * RW:3 * RW]
    lora_w = jnp.tanh(z[:, 3 * RW:3 * RW + LANES]).astype(BF16)
    lora_a = z[:, 3 * RW + LANES:3 * RW + 2 * LANES].astype(BF16)
    lora_g = _sigmoid(z[:, 3 * RW + 2 * LANES:3 * RW + 4 * LANES]).astype(BF16)
    k_k, k_a, r_k = vec_ref[0:1, :], vec_ref[1:2, :], vec_ref[2:3, :]
    ones = ones_ref[...]

    kk = k * k_k
    kk = kk / jnp.maximum(jnp.sqrt(_split_dot(kk * kk, ones)), 1e-12)
    ksum = jnp.zeros_like(k)
    for d in range(2):
        w0 = vec_ref[3 + d:4 + d, :]
        a0 = vec_ref[5 + d:6 + d, :]
        wl = w0 + jnp.dot(lora_w, wup_ref[d], preferred_element_type=F32)
        sp = jnp.maximum(-wl, 0.0) + jnp.log(1.0 + jnp.exp(-jnp.abs(wl)))
        lw = -jnp.exp(-sp - 0.5)
        a = _sigmoid(a0 + jnp.dot(lora_a, aup_ref[d], preferred_element_type=F32))
        k_d = k * (1.0 + (a - 1.0) * k_a)
        bb = kk * a
        ksum = ksum + k_d
        lw_hi = lw.astype(BF16)
        lw_lo = (lw - lw_hi.astype(F32)).astype(BF16)
        cl = (jnp.dot(tri_ref[d, 0], lw_hi, preferred_element_type=F32)
              + jnp.dot(tri_ref[d, 0], lw_lo, preferred_element_type=F32))
        rem = (jnp.dot(tri_ref[d, 1], lw_hi, preferred_element_type=F32)
               + jnp.dot(tri_ref[d, 1], lw_lo, preferred_element_type=F32))
        p_in = jnp.exp(cl)
        p_inv = jnp.exp(-cl)
        p_rem = jnp.exp(rem)
        ah_ref[d] = (-kk * jnp.exp(cl - lw)).astype(BF16)
        rh_ref[d] = (r * p_in).astype(BF16)
        bh_ref[d] = (bb * p_inv).astype(BF16)
        kh_ref[d] = (k_d * p_inv).astype(BF16)
        bt_ref[d] = (bb * p_rem).astype(BF16)
        kt_ref[d] = (k_d * p_rem).astype(BF16)
        last = 0 if d == 1 else WKV_CHUNK - 1
        for cidx in range(tt // WKV_CHUNK):
            row = cidx * WKV_CHUNK + last
            pt_ref[d, cidx] = p_in[row:row + 1, :]
    v_ref[...] = v.astype(BF16)
    g_ref[...] = jnp.dot(lora_g, gup_ref[...], preferred_element_type=F32)
    bonus_ref[...] = _split_dot(r * ksum * r_k, ones) * v


def _rw_features(zr, layer, mu_pad, wup, aup, gup, vecs, ones_blk, tri, *, tb, lc, tt):
    m = zr.shape[0]
    nh = tt // GRID_W
    nblk64 = m // GRID_W
    ncht = tt // WKV_CHUNK
    full = pl.BlockSpec((tt, RW), lambda i: (i, 0))
    per_dir = pl.BlockSpec((2, tt, RW), lambda i: (0, i, 0))
    one = jax.ShapeDtypeStruct((m, RW), F32)
    two = jax.ShapeDtypeStruct((2, m, RW), BF16)
    return pl.pallas_call(
        functools.partial(_rw_feat_body, tt=tt, tb=tb, lc=lc),
        grid=(m // tt,),
        in_specs=[pl.BlockSpec((tt, ZR_COLS), lambda i: (i, 0)),
                  pl.BlockSpec((GRID_W, ZR_COLS), lambda i: (jnp.maximum(i * nh - 1, 0), 0)),
                  pl.BlockSpec((GRID_W, ZR_COLS), lambda i: (jnp.minimum((i + 1) * nh, nblk64 - 1), 0)),
                  pl.BlockSpec((None, 1, ZR_COLS), lambda i: (layer, 0, 0)),
                  pl.BlockSpec((None, 2, LANES, RW), lambda i: (layer, 0, 0, 0)),
                  pl.BlockSpec((None, 2, LANES, RW), lambda i: (layer, 0, 0, 0)),
                  pl.BlockSpec((None, 2 * LANES, RW), lambda i: (layer, 0, 0)),
                  pl.BlockSpec((None, 8, RW), lambda i: (layer, 0, 0)),
                  pl.BlockSpec((RW, RW), lambda i: (0, 0)),
                  pl.BlockSpec((2, 2, tt, tt), lambda i: (0, 0, 0, 0))],
        out_specs=[full, full, full, per_dir, per_dir, per_dir, per_dir, per_dir, per_dir,
                   pl.BlockSpec((2, ncht, 1, RW), lambda i: (0, i, 0, 0))],
        out_shape=[jax.ShapeDtypeStruct((m, RW), BF16), one, one, two, two, two, two, two, two,
                   jax.ShapeDtypeStruct((2, m // WKV_CHUNK, 1, RW), F32)],
        compiler_params=_cp(("arbitrary",)),
    )(zr, zr, zr, mu_pad, wup, aup, gup, vecs, ones_blk, tri)


def _chunk_tri(tt):
    t = jnp.arange(tt)[:, None]
    s = jnp.arange(tt)[None, :]
    same = (t // WKV_CHUNK) == (s // WKV_CHUNK)
    fwd = jnp.stack([same & (s <= t), same & (s > t)])
    bwd = jnp.stack([same & (s >= t), same & (s < t)])
    return jnp.stack([fwd, bwd]).astype(BF16)


def _dot_nt(a, b):
    return lax.dot_general(a, b, (((1,), (1,)), ((), ())), preferred_element_type=F32)


def _dot_tn(a, b):
    return lax.dot_general(a, b, (((0,), (0,)), ((), ())), preferred_element_type=F32)


def _wkv_body(*refs):
    c = WKV_CHUNK
    ins = (refs[0:8], refs[8:16])
    y_refs = refs[16:18]
    s_ref = refs[18]

    @pl.when(pl.program_id(1) == 0)
    def _():
        s_ref[...] = jnp.zeros_like(s_ref)

    t_idx = lax.broadcasted_iota(jnp.int32, (2 * c, LANES), 0)
    lane = lax.broadcasted_iota(jnp.int32, (2 * c, LANES), 1)
    diff = (t_idx & (c - 1)) - (lane & (c - 1))
    tri = (jnp.where(t_idx < c, diff, diff + 1) > 0, jnp.where(t_idx < c, -diff, 1 - diff) > 0)
    same_head = (t_idx // c) == (lane // c)
    lo = lax.broadcasted_iota(jnp.int32, (c, LANES), 1) < c

    def bd(x):
        zero = jnp.zeros_like(x)
        return jnp.concatenate([jnp.where(lo, x, zero), jnp.where(lo, zero, x)], axis=0)

    chains = [(d, p) for d in range(2) for p in range(HEADS // 2)]
    n = range(len(chains))
    sl = [slice(p * LANES, (p + 1) * LANES) for _, p in chains]
    ah, rh, bh, kh, bt, kt, vv, pt = ([ins[d][q] for d, _ in chains] for q in range(8))
    ar = [jnp.concatenate([ah[i][:, sl[i]], rh[i][:, sl[i]]], axis=0) for i in n]
    s_old = [s_ref[d, p] for d, p in chains]
    pb = [jnp.where(tri[chains[i][0]], _dot_nt(ar[i], bd(bh[i][:, sl[i]])), 0.0) for i in n]
    pk = [jnp.where(tri[chains[i][0]], _dot_nt(ar[i], bd(kh[i][:, sl[i]])), 0.0) for i in n]
    ps = [_dot_nt(ar[i], s_old[i].astype(BF16)) for i in n]
    bdv = [bd(vv[i][:, sl[i]]) for i in n]
    x = [ps[i][:c] + jnp.dot(pk[i][:c].astype(BF16), bdv[i], preferred_element_type=F32) for i in n]
    lp = [pb[i][:c].astype(BF16) for i in n]
    for it in range(6):
        x = [x[i] + jnp.dot(lp[i], bd(x[i].astype(BF16)), preferred_element_type=F32) for i in n]
        if it < 5:
            lp = [jnp.dot(lp[i], bd(lp[i]), preferred_element_type=F32).astype(BF16) for i in n]
    u = [x[i].astype(BF16) for i in n]
    for i in n:
        y_refs[chains[i][0]][:, sl[i]] = (
            ps[i][c:] + jnp.dot(pb[i][c:].astype(BF16), bd(u[i]), preferred_element_type=F32)
            + jnp.dot(pk[i][c:].astype(BF16), bdv[i], preferred_element_type=F32))
    upd = [_dot_tn(jnp.concatenate([u[i], vv[i][:, sl[i]]], axis=0),
                   jnp.concatenate([bt[i][:, sl[i]], kt[i][:, sl[i]]], axis=0)) for i in n]
    for i, (d, p) in enumerate(chains):
        s_ref[d, p] = s_old[i] * pt[i][:, sl[i]] + jnp.where(same_head, upd[i], 0.0)


def _wkv_scan(ah, rh, bh, kh, bt, kt, v, pt, *, nb, tb, lc):
    m = v.shape[0]
    c = WKV_CHUNK
    ncb, ncc = tb // c, lc // c
    nlc = ncb - ncc

    def chunk(d, b, n):
        if d == 0:
            return b * ncb + n
        return b * ncb + jnp.where(n < ncc, ncc - 1 - n, ncc + (nlc - 1) - (n - ncc))

    def specs(d):
        per_dir = pl.BlockSpec((None, c, RW), lambda b, n: (d, chunk(d, b, n), 0))
        return [per_dir] * 6 + [pl.BlockSpec((c, RW), lambda b, n: (chunk(d, b, n), 0)),
                                pl.BlockSpec((None, None, 1, RW), lambda b, n: (d, chunk(d, b, n), 0, 0))]

    out = jax.ShapeDtypeStruct((m, RW), F32)
    args = (ah, rh, bh, kh, bt, kt, v, pt)
    return pl.pallas_call(
        _wkv_body,
        grid=(nb, ncb),
        in_specs=specs(0) + specs(1),
        out_specs=[pl.BlockSpec((c, RW), lambda b, n: (chunk(0, b, n), 0)),
                   pl.BlockSpec((c, RW), lambda b, n: (chunk(1, b, n), 0))],
        out_shape=[out, out],
        scratch_shapes=[pltpu.VMEM((2, HEADS // 2, LANES, LANES), F32)],
        compiler_params=_cp(("arbitrary", "arbitrary")),
    )(*args, *args)


def _rw_readout_body(yf_ref, yb_ref, bonus_ref, g_ref, gn_ref, ones_ref, o_ref):
    y = yf_ref[...] + yb_ref[...]
    ones = ones_ref[...]
    mean = _split_dot(y, ones) * (1.0 / HEAD)
    yc = y - mean
    var = _split_dot(yc * yc, ones) * (1.0 / HEAD)
    yn = yc * lax.rsqrt(var + GN_EPS) * gn_ref[0:1, :] + gn_ref[1:2, :]
    o_ref[...] = ((yn + bonus_ref[...]) * g_ref[...]).astype(o_ref.dtype)


def _rw_readout(y_f, y_b, bonus, g, gn, layer, ones_blk, *, tm):
    m = y_f.shape[0]
    full = pl.BlockSpec((tm, RW), lambda i: (i, 0))
    return pl.pallas_call(
        _rw_readout_body,
        grid=(m // tm,),
        in_specs=[full, full, full, full,
                  pl.BlockSpec((None, 8, RW), lambda i: (layer, 0, 0)),
                  pl.BlockSpec((RW, RW), lambda i: (0, 0))],
        out_specs=full,
        out_shape=jax.ShapeDtypeStruct((m, RW), BF16),
        compiler_params=_cp(("arbitrary",)),
    )(y_f, y_b, bonus, g, gn, ones_blk)


MOE_TILE = 512


def _moe_plan(route, tme):
    m = route.shape[0]
    n_asg = 2 * m
    idx = route[:, N_EXPERTS:N_EXPERTS + 2].astype(jnp.int32)
    e_flat = idx.T.reshape(n_asg)
    w_flat = route[:, N_EXPERTS + 2:N_EXPERTS + 4].T.reshape(n_asg)
    tok_flat = jnp.tile(jnp.arange(m, dtype=jnp.int32), 2)
    order = jnp.argsort(e_flat, stable=True).astype(jnp.int32)
    e_sorted, tok_sorted, w_sorted = e_flat[order], tok_flat[order], w_flat[order]
    experts = jnp.arange(N_EXPERTS, dtype=jnp.int32)
    cnt = jnp.sum((e_flat[:, None] == experts[None, :]).astype(jnp.int32), axis=0)
    start = jnp.cumsum(cnt) - cnt
    padded = ((cnt + tme - 1) // tme) * tme
    off_end = jnp.cumsum(padded)
    off = off_end - padded
    n_rows = ((n_asg + tme - 1) // tme) * tme + N_EXPERTS * tme
    n_tiles = n_rows // tme
    tile_e = jnp.sum((jnp.arange(n_tiles, dtype=jnp.int32)[:, None] * tme >= off_end[None, :]).astype(jnp.int32),
                     axis=1)
    tile_e = jnp.minimum(tile_e, N_EXPERTS - 1)
    first = jnp.concatenate([jnp.ones((1,), jnp.int32), (tile_e[1:] != tile_e[:-1]).astype(jnp.int32)])
    n_act = (off_end[-1] // tme).reshape(1)
    e_row = jnp.repeat(tile_e, tme)
    q = jnp.arange(n_rows, dtype=jnp.int32) - off[e_row]
    valid = q < cnt[e_row]
    s_row = jnp.clip(start[e_row] + q, 0, n_asg - 1)
    src_tok = jnp.where(valid, tok_sorted[s_row], 0)
    row_w = jnp.where(valid, w_sorted[s_row], 0.0)[:, None]
    dest_sorted = off[e_sorted] + (jnp.arange(n_asg, dtype=jnp.int32) - start[e_sorted])
    pos = jnp.zeros((n_asg,), jnp.int32).at[order].set(dest_sorted, unique_indices=True)
    return dict(src_tok=src_tok, row_w=row_w, pos=pos, tile_e=tile_e, first=first, n_act=n_act,
                n_rows=n_rows, n_tiles=n_tiles)


def _gather_body(idx_ref, src_ref, dst_ref, sem, *, rows):
    base = pl.program_id(0) * rows

    def issue(r, carry):
        pltpu.make_async_copy(src_ref.at[idx_ref[base + r]], dst_ref.at[base + r], sem).start()
        return carry

    lax.fori_loop(0, rows, issue, 0)
    pltpu.make_async_copy(dst_ref.at[pl.ds(base, rows)], dst_ref.at[pl.ds(base, rows)], sem).wait()


def _gather_rows(src3, idx, *, rows):
    n_rows = idx.shape[0]
    return pl.pallas_call(
        functools.partial(_gather_body, rows=rows),
        grid_spec=pltpu.PrefetchScalarGridSpec(
            num_scalar_prefetch=1, grid=(n_rows // rows,),
            in_specs=[pl.BlockSpec(memory_space=pl.ANY)],
            out_specs=pl.BlockSpec(memory_space=pl.ANY),
            scratch_shapes=[pltpu.SemaphoreType.DMA(())]),
        out_shape=jax.ShapeDtypeStruct((n_rows,) + src3.shape[1:], src3.dtype),
        compiler_params=_cp(("arbitrary",)),
    )(idx, src3)


def _moe_up_body(te_ref, first_ref, nact_ref, x_ref, wg_ref, wu_ref, o_ref, wgb_ref, wub_ref):
    i = pl.program_id(1)

    @pl.when(i < nact_ref[0])
    def _():
        @pl.when(first_ref[i] == 1)
        def _():
            wgb_ref[...] = wg_ref[...].astype(BF16)
            wub_ref[...] = wu_ref[...].astype(BF16)

        x = x_ref[...]
        g = jnp.dot(x, wgb_ref[...], preferred_element_type=F32)
        u = jnp.dot(x, wub_ref[...], preferred_element_type=F32)
        o_ref[...] = (g * _sigmoid(g) * u).astype(o_ref.dtype)

    @pl.when(i >= nact_ref[0])
    def _():
        o_ref[...] = jnp.zeros_like(o_ref)


def _moe_up(xg, wg, wu, layer, plan, *, tme, tn):
    p, k = xg.shape
    f = wg.shape[-1]

    def row(i, nact):
        return jnp.minimum(i, nact[0] - 1)

    wspec = pl.BlockSpec((None, None, k, tn), lambda j, i, te, fi, na: (layer, te[row(i, na)], 0, j))
    return pl.pallas_call(
        _moe_up_body,
        grid_spec=pltpu.PrefetchScalarGridSpec(
            num_scalar_prefetch=3, grid=(f // tn, p // tme),
            in_specs=[pl.BlockSpec((tme, k), lambda j, i, te, fi, na: (row(i, na), 0)), wspec, wspec],
            out_specs=pl.BlockSpec((tme, tn), lambda j, i, te, fi, na: (i, j)),
            scratch_shapes=[pltpu.VMEM((k, tn), BF16), pltpu.VMEM((k, tn), BF16)]),
        out_shape=jax.ShapeDtypeStruct((p, f), BF16),
        compiler_params=_cp(("arbitrary", "arbitrary")),
    )(plan["tile_e"], plan["first"], plan["n_act"], xg, wg, wu)


def _moe_down_body(te_ref, first_ref, nact_ref, a_ref, w_ref, rw_ref, o_ref, wb_ref):
    i = pl.program_id(1)

    @pl.when(i < nact_ref[0])
    def _():
        @pl.when(first_ref[i] == 1)
        def _():
            wb_ref[...] = w_ref[...].astype(BF16)

        o_ref[...] = rw_ref[...] * jnp.dot(a_ref[...], wb_ref[...], preferred_element_type=F32)

    @pl.when(i >= nact_ref[0])
    def _():
        o_ref[...] = jnp.zeros_like(o_ref)


def _moe_down(act, wd, layer, plan, *, tme, tn):
    p, k = act.shape
    d = wd.shape[-1]

    def row(i, nact):
        return jnp.minimum(i, nact[0] - 1)

    return pl.pallas_call(
        _moe_down_body,
        grid_spec=pltpu.PrefetchScalarGridSpec(
            num_scalar_prefetch=3, grid=(d // tn, p // tme),
            in_specs=[pl.BlockSpec((tme, k), lambda j, i, te, fi, na: (row(i, na), 0)),
                      pl.BlockSpec((None, None, k, tn), lambda j, i, te, fi, na: (layer, te[row(i, na)], 0, j)),
                      pl.BlockSpec((tme, 1), lambda j, i, te, fi, na: (row(i, na), 0))],
            out_specs=pl.BlockSpec((tme, tn), lambda j, i, te, fi, na: (i, j)),
            scratch_shapes=[pltpu.VMEM((k, tn), BF16)]),
        out_shape=jax.ShapeDtypeStruct((p, d), F32),
        compiler_params=_cp(("arbitrary", "arbitrary")),
    )(plan["tile_e"], plan["first"], plan["n_act"], act, wd, plan["row_w"])


def _moe_combine_body(pos_ref, x_ref, y_ref, tab_ref, o_ref, buf, sem, *, tc, m, tb, lc):
    i = pl.program_id(0)
    base = i * tc

    def issue(r, carry):
        pltpu.make_async_copy(y_ref.at[pos_ref[base + r]], buf.at[0, r], sem).start()
        pltpu.make_async_copy(y_ref.at[pos_ref[m + base + r]], buf.at[1, r], sem).start()
        return carry

    lax.fori_loop(0, tc, issue, 0)
    pltpu.make_async_copy(buf, buf, sem).wait()
    gate = _row_mod(tab_ref, i, tc, tb, lc)
    o_ref[...] = x_ref[...] + gate * (buf[0] + buf[1])


def _moe_combine(x, y, pos, mod, gate_idx, *, tb, lc, tc):
    m, d = x.shape
    return pl.pallas_call(
        functools.partial(_moe_combine_body, tc=tc, m=m, tb=tb, lc=lc),
        grid_spec=pltpu.PrefetchScalarGridSpec(
            num_scalar_prefetch=1, grid=(m // tc,),
            in_specs=[pl.BlockSpec((tc, d), lambda i, pos: (i, 0)),
                      pl.BlockSpec(memory_space=pl.ANY),
                      pl.BlockSpec((8, d), lambda i, pos: (0, gate_idx))],
            out_specs=pl.BlockSpec((tc, d), lambda i, pos: (i, 0)),
            scratch_shapes=[pltpu.VMEM((2, tc, d), F32), pltpu.SemaphoreType.DMA(())]),
        out_shape=jax.ShapeDtypeStruct((m, d), F32),
        input_output_aliases={1: 0},
        compiler_params=_cp(("arbitrary",)),
    )(pos, x, y, mod)


def _pad_rows(x, rows):
    return jnp.concatenate([x, jnp.zeros((rows - x.shape[0],) + x.shape[1:], x.dtype)], axis=0)


def _pick_tile(n, target):
    best = 8
    for t in range(8, target + 1, 8):
        if n % t == 0:
            best = t
    return best


def kernel(x, c, ctx, c_ctx, w_ada, b_ada, norm1, norm2, norm_f, w_in, s5_lambda_re, s5_lambda_im, s5_log_step, s5_b_re, s5_b_im, s5_c_re, s5_c_im, s5_d, s5_w_glu, rw_mu, rw_w0, rw_w_up, rw_a0, rw_a_up, rw_g_up, rw_k_k, rw_k_a, rw_r_k, rw_gn_w, rw_gn_b, w_proj_s5, w_proj_rw, w_out, ffn_w_gate, ffn_w_up, ffn_w_down, moe_router, moe_router_bias, moe_w_gate, moe_w_up, moe_w_down):
    nb, l, d = x.shape
    lc = ctx.shape[1]
    tb = lc + l
    m = nb * tb
    depth = w_in.shape[0]
    tt = lc
    assert lc % GRID_W == 0 and l % tt == 0 and l % GRID_W == 0 and nb * HEADS * 4 == LANES
    assert tb % S5_T == 0 and lc % S5_T == 0 and tt % WKV_CHUNK == 0
    tm = _pick_tile(tb, 1088)
    ncb, ncc = tb // S5_T, lc // S5_T

    xs = jnp.concatenate([ctx, x], axis=1).reshape(m, d)
    silu_rows = _pad_rows(jnp.concatenate([c_ctx[None], c], axis=0), 8)
    silu_rows = (silu_rows * _sigmoid(silu_rows)).astype(BF16)

    ones_blk = (jnp.arange(RW)[:, None] // HEAD == jnp.arange(RW)[None, :] // HEAD).astype(BF16)
    tri = _chunk_tri(tt)
    mu_pad = jnp.pad(rw_mu, ((0, 0), (0, ZR_COLS - rw_mu.shape[1])))[:, None, :]
    zeros64 = jnp.zeros((depth, 1, DECAY_LORA, RW), F32)
    wup = jnp.concatenate([jnp.concatenate([rw_w_up[:, :1], zeros64], axis=2),
                           jnp.concatenate([zeros64, rw_w_up[:, 1:]], axis=2)], axis=1).astype(BF16)
    aup = jnp.concatenate([jnp.concatenate([rw_a_up[:, :1], zeros64], axis=2),
                           jnp.concatenate([zeros64, rw_a_up[:, 1:]], axis=2)], axis=1).astype(BF16)
    gup = jnp.pad(rw_g_up, ((0, 0), (0, 2 * LANES - GATE_LORA), (0, 0))).astype(BF16)
    vecs = jnp.stack([rw_k_k, rw_k_a, rw_r_k.reshape(depth, RW), rw_w0[:, 0], rw_w0[:, 1],
                      rw_a0[:, 0], rw_a0[:, 1], jnp.zeros_like(rw_k_k)], axis=1)
    gn = jnp.pad(jnp.stack([rw_gn_w, rw_gn_b], axis=1), ((0, 0), (0, 6), (0, 0)))
    d_skip = s5_d.reshape(depth, 1, S5_W)
    norm1, norm2 = norm1[:, None, :], norm2[:, None, :]
    w_gates = w_in[:, :, S5_W + 3 * RW + LORA_COLS:]

    for i in range(depth):
        mod = _matmul(silu_rows, w_ada, (i,), 6 * d, tm=8, tn=1024, bias=b_ada[i][None])

        h = _norm_mod(xs, norm1, i, mod, 0, tb=tb, lc=lc, tm=tt)
        u = _matmul(h, w_in, (i,), S5_W, tm=tm, tn=512)
        zr = _matmul(h, w_in, (i,), ZR_COLS, col_off=S5_W, tm=tm, tn=512)
        zg = _matmul(h, w_gates, (i,), 2 * d, tm=tm, tn=512)

        tables = _s5_tables(s5_lambda_re[i], s5_lambda_im[i], s5_log_step[i], s5_b_re[i], s5_b_im[i],
                            s5_c_re[i], s5_c_im[i])
        ug = jnp.transpose(u.reshape(nb * ncb, S5_T, S5_G, S5_H), (2, 0, 1, 3))
        ug = ug.reshape(S5_G, nb * ncb, S5_T * S5_H).astype(BF16)
        yg = _s5_mix(ug, tables, nb=nb, ncb=ncb, ncc=ncc)
        ys = jnp.transpose(yg.reshape(S5_G, nb * ncb, S5_T, S5_H), (1, 2, 0, 3)).reshape(m, S5_W)
        y_s5 = _s5_readout(ys, u, d_skip, s5_w_glu, i, tm=tt)

        v, g, bonus, ah, rh, bh, kh, bt, kt, pt = _rw_features(zr, i, mu_pad, wup, aup, gup, vecs, ones_blk, tri,
                                                               tb=tb, lc=lc, tt=tt)
        y_f, y_b = _wkv_scan(ah, rh, bh, kh, bt, kt, v, pt, nb=nb, tb=tb, lc=lc)
        y_rw = _rw_readout(y_f, y_b, bonus, g, gn, i, ones_blk, tm=tt)

        mrg = _merge(y_s5, y_rw, w_proj_s5, w_proj_rw, i, zg, tm=tm, tn=512)
        xs = _down_resid(mrg, w_out, (i,), xs, mod, 2, tb=tb, lc=lc, tm=tm, tn=512)

        j = i // 2
        h2 = _norm_mod(xs, norm2, i, mod, 3, tb=tb, lc=lc, tm=tt)
        if i % 2 == 0:
            act = _swiglu_up(h2, ffn_w_gate, ffn_w_up, (j,), tm=tm, tn=512)
            xs = _down_resid(act, ffn_w_down, (j,), xs, mod, 5, tb=tb, lc=lc, tm=_pick_tile(tb, 544), tn=512)
        else:
            router_pad = jnp.pad(moe_router[j], ((0, 0), (0, LANES - N_EXPERTS)))
            bias_pad = jnp.pad(moe_router_bias[j], (0, LANES - N_EXPERTS))[None]
            route = _router_gates(xs, norm2, i, mod, router_pad, bias_pad, tb=tb, lc=lc, tm=tt)
            plan = _moe_plan(route, MOE_TILE)
            xg = _gather_rows(h2.reshape(m, d // LANES, LANES), plan["src_tok"], rows=MOE_TILE)
            act = _moe_up(xg.reshape(plan["n_rows"], d), moe_w_gate, moe_w_up, j, plan, tme=MOE_TILE, tn=256)
            y_exp = _moe_down(act, moe_w_down, j, plan, tme=MOE_TILE, tn=512)
            xs = _moe_combine(xs, y_exp, plan["pos"], mod, 5, tb=tb, lc=lc, tc=tt)

    return _final_norm(xs.reshape(nb, tb, d), norm_f[None], lc=lc, tm=tt)
```

```python
import functools
import math

import jax
import jax.numpy as jnp
from jax import lax
from jax.experimental import pallas as pl
from jax.experimental.pallas import tpu as pltpu

F32 = jnp.float32
BF16 = jnp.bfloat16

GRID_W = 64
HEADS = 16
HEAD = 64
RW = HEADS * HEAD
S5_G = 32
S5_H = 16
S5_P = 64
S5_W = S5_G * S5_H
S5_T = 32
WKV_CHUNK = 64
DECAY_LORA = 64
ICLR_LORA = 64
GATE_LORA = 160
LORA_COLS = 2 * DECAY_LORA + 2 * ICLR_LORA + GATE_LORA
LORA_PAD = 512
ZR_COLS = 3 * RW + LORA_PAD
NORM_EPS = 1e-6
GN_EPS = 64e-5
N_EXPERTS = 8
LANES = 128
LOG2_E = 1.4426950408889634
VMEM_LIMIT = 56 * 1024 * 1024


def _cp(sem):
    return pltpu.CompilerParams(dimension_semantics=sem, vmem_limit_bytes=VMEM_LIMIT)


def _sigmoid(x):
    return 0.5 * jnp.tanh(0.5 * x) + 0.5


def _gelu_tanh(x):
    return 0.5 * x * (1.0 + jnp.tanh(math.sqrt(2.0 / math.pi) * (x + 0.044715 * (x * x * x))))


def _row_mod(tab_ref, tile, tm, tb, lc):
    start = tile * tm
    b = start // tb
    pos = (start - b * tb) + lax.broadcasted_iota(jnp.int32, (tm, 1), 0)
    ctx_row = tab_ref[0:1, :]
    lat_row = tab_ref[pl.ds(1 + b, 1), :]
    return jnp.where(pos < lc, ctx_row, lat_row)


def _split_dot(x, w_bf16):
    hi = x.astype(BF16)
    lo = (x - hi.astype(F32)).astype(BF16)
    return (jnp.dot(hi, w_bf16, preferred_element_type=F32)
            + jnp.dot(lo, w_bf16, preferred_element_type=F32))


def _lead_spec(lead, shape, fn):
    return pl.BlockSpec((None,) * len(lead) + shape, lambda j, i: tuple(lead) + fn(j, i))


def _mm_body(x_ref, w_ref, *rest, has_bias):
    if has_bias:
        b_ref, o_ref, wb_ref = rest
    else:
        o_ref, wb_ref = rest

    @pl.when(pl.program_id(1) == 0)
    def _():
        wb_ref[...] = w_ref[...].astype(BF16)

    acc = jnp.dot(x_ref[...], wb_ref[...], preferred_element_type=F32)
    if has_bias:
        acc = acc + b_ref[...]
    o_ref[...] = acc.astype(o_ref.dtype)


def _matmul(x, w, lead, n_out, *, col_off=0, tm, tn, out_dtype=F32, bias=None):
    m, k = x.shape
    off = col_off // tn
    in_specs = [pl.BlockSpec((tm, k), lambda j, i: (i, 0)),
                _lead_spec(lead, (k, tn), lambda j, i: (0, j + off))]
    args = [x, w]
    if bias is not None:
        in_specs.append(pl.BlockSpec((1, tn), lambda j, i: (0, j)))
        args.append(bias)
    return pl.pallas_call(
        functools.partial(_mm_body, has_bias=bias is not None),
        grid=(n_out // tn, m // tm),
        in_specs=in_specs,
        out_specs=pl.BlockSpec((tm, tn), lambda j, i: (i, j)),
        out_shape=jax.ShapeDtypeStruct((m, n_out), out_dtype),
        scratch_shapes=[pltpu.VMEM((k, tn), BF16)],
        compiler_params=_cp(("arbitrary", "arbitrary")),
    )(*args)


def _swiglu_up_body(x_ref, wg_ref, wu_ref, o_ref, wgb_ref, wub_ref):
    @pl.when(pl.program_id(1) == 0)
    def _():
        wgb_ref[...] = wg_ref[...].astype(BF16)
        wub_ref[...] = wu_ref[...].astype(BF16)

    x = x_ref[...]
    g = jnp.dot(x, wgb_ref[...], preferred_element_type=F32)
    u = jnp.dot(x, wub_ref[...], preferred_element_type=F32)
    o_ref[...] = (g * _sigmoid(g) * u).astype(o_ref.dtype)


def _swiglu_up(x, wg, wu, lead, *, tm, tn):
    m, k = x.shape
    f = wg.shape[-1]
    wspec = _lead_spec(lead, (k, tn), lambda j, i: (0, j))
    return pl.pallas_call(
        _swiglu_up_body,
        grid=(f // tn, m // tm),
        in_specs=[pl.BlockSpec((tm, k), lambda j, i: (i, 0)), wspec, wspec],
        out_specs=pl.BlockSpec((tm, tn), lambda j, i: (i, j)),
        out_shape=jax.ShapeDtypeStruct((m, f), BF16),
        scratch_shapes=[pltpu.VMEM((k, tn), BF16), pltpu.VMEM((k, tn), BF16)],
        compiler_params=_cp(("arbitrary", "arbitrary")),
    )(x, wg, wu)


def _down_resid_body(a_ref, w_ref, x_ref, tab_ref, *rest, tm, tb, lc, expert):
    if expert is None:
        o_ref, wb_ref = rest
    else:
        rs_ref, o_ref, wb_ref = rest

    @pl.when(pl.program_id(1) == 0)
    def _():
        wb_ref[...] = w_ref[...].astype(BF16)

    y = jnp.dot(a_ref[...], wb_ref[...], preferred_element_type=F32)
    gate = _row_mod(tab_ref, pl.program_id(1), tm, tb, lc)
    if expert is not None:
        gate = gate * rs_ref[:, expert:expert + 1]
    o_ref[...] = x_ref[...] + gate * y


def _down_resid(a, w, lead, x, mod, gate_idx, *, tb, lc, tm, tn, rowscale=None, expert=None):
    m, k = a.shape
    d = x.shape[1]
    goff = gate_idx * (d // tn)
    in_specs = [pl.BlockSpec((tm, k), lambda j, i: (i, 0)),
                _lead_spec(lead, (k, tn), lambda j, i: (0, j)),
                pl.BlockSpec((tm, tn), lambda j, i: (i, j)),
                pl.BlockSpec((8, tn), lambda j, i: (0, goff + j))]
    args = [a, w, x, mod]
    if rowscale is not None:
        in_specs.append(pl.BlockSpec((tm, LANES), lambda j, i: (i, 0)))
        args.append(rowscale)
    return pl.pallas_call(
        functools.partial(_down_resid_body, tm=tm, tb=tb, lc=lc, expert=expert),
        grid=(d // tn, m // tm),
        in_specs=in_specs,
        out_specs=pl.BlockSpec((tm, tn), lambda j, i: (i, j)),
        out_shape=jax.ShapeDtypeStruct((m, d), F32),
        scratch_shapes=[pltpu.VMEM((k, tn), BF16)],
        input_output_aliases={2: 0},
        compiler_params=_cp(("arbitrary", "arbitrary")),
    )(*args)


def _merge_body(ys_ref, yr_ref, ws_ref, wr_ref, gs_ref, gr_ref, o_ref, wsb_ref, wrb_ref):
    @pl.when(pl.program_id(1) == 0)
    def _():
        wsb_ref[...] = ws_ref[...].astype(BF16)
        wrb_ref[...] = wr_ref[...].astype(BF16)

    ps = jnp.dot(ys_ref[...], wsb_ref[...], preferred_element_type=F32)
    pr = jnp.dot(yr_ref[...], wrb_ref[...], preferred_element_type=F32)
    o_ref[...] = (_sigmoid(gs_ref[...]) * ps + _sigmoid(gr_ref[...]) * pr).astype(o_ref.dtype)


def _merge(ys, yr, w_ps, w_pr, layer, zg, *, tm, tn):
    m = ys.shape[0]
    d = w_ps.shape[-1]
    nb = d // tn
    return pl.pallas_call(
        _merge_body,
        grid=(nb, m // tm),
        in_specs=[pl.BlockSpec((tm, S5_W), lambda j, i: (i, 0)),
                  pl.BlockSpec((tm, RW), lambda j, i: (i, 0)),
                  _lead_spec((layer,), (S5_W, tn), lambda j, i: (0, j)),
                  _lead_spec((layer,), (RW, tn), lambda j, i: (0, j)),
                  pl.BlockSpec((tm, tn), lambda j, i: (i, j)),
                  pl.BlockSpec((tm, tn), lambda j, i: (i, nb + j))],
        out_specs=pl.BlockSpec((tm, tn), lambda j, i: (i, j)),
        out_shape=jax.ShapeDtypeStruct((m, d), BF16),
        scratch_shapes=[pltpu.VMEM((S5_W, tn), BF16), pltpu.VMEM((RW, tn), BF16)],
        compiler_params=_cp(("arbitrary", "arbitrary")),
    )(ys, yr, w_ps, w_pr, zg, zg)


def _norm_mod_body(x_ref, gain_ref, sh_ref, sc_ref, o_ref, *, tm, tb, lc):
    x = x_ref[...]
    y = x * lax.rsqrt(jnp.mean(x * x, axis=-1, keepdims=True) + NORM_EPS) * gain_ref[...]
    i = pl.program_id(0)
    shift = _row_mod(sh_ref, i, tm, tb, lc)
    scale = _row_mod(sc_ref, i, tm, tb, lc)
    o_ref[...] = (y * (1.0 + scale) + shift).astype(o_ref.dtype)


def _norm_mod(x, gain, layer, mod, shift_idx, *, tb, lc, tm, out_dtype=BF16):
    m, d = x.shape
    return pl.pallas_call(
        functools.partial(_norm_mod_body, tm=tm, tb=tb, lc=lc),
        grid=(m // tm,),
        in_specs=[pl.BlockSpec((tm, d), lambda i: (i, 0)),
                  pl.BlockSpec((None, 1, d), lambda i: (layer, 0, 0)),
                  pl.BlockSpec((8, d), lambda i: (0, shift_idx)),
                  pl.BlockSpec((8, d), lambda i: (0, shift_idx + 1))],
        out_specs=pl.BlockSpec((tm, d), lambda i: (i, 0)),
        out_shape=jax.ShapeDtypeStruct((m, d), out_dtype),
        compiler_params=_cp(("arbitrary",)),
    )(x, gain, mod, mod)


def _final_norm_body(x_ref, gain_ref, o_ref):
    x = x_ref[...]
    o_ref[...] = x * lax.rsqrt(jnp.mean(x * x, axis=-1, keepdims=True) + NORM_EPS) * gain_ref[...]


def _final_norm(x3, gain, *, lc, tm):
    b, tb, d = x3.shape
    l = tb - lc
    skip = lc // tm
    return pl.pallas_call(
        _final_norm_body,
        grid=(b, l // tm),
        in_specs=[pl.BlockSpec((None, tm, d), lambda bi, t: (bi, skip + t, 0)),
                  pl.BlockSpec((1, d), lambda bi, t: (0, 0))],
        out_specs=pl.BlockSpec((None, tm, d), lambda bi, t: (bi, t, 0)),
        out_shape=jax.ShapeDtypeStruct((b, l, d), F32),
        compiler_params=_cp(("arbitrary", "arbitrary")),
    )(x3, gain)


def _router_body(x_ref, gain_ref, sh_ref, sc_ref, w_ref, b_ref, o_ref, *, tm, tb, lc):
    x = x_ref[...]
    y = x * lax.rsqrt(jnp.mean(x * x, axis=-1, keepdims=True) + NORM_EPS) * gain_ref[...]
    i = pl.program_id(0)
    h = y * (1.0 + _row_mod(sc_ref, i, tm, tb, lc)) + _row_mod(sh_ref, i, tm, tb, lc)
    logits = jnp.dot(h, w_ref[...], preferred_element_type=F32,
                     precision=lax.Precision.HIGHEST) + b_ref[...]
    lane = lax.broadcasted_iota(jnp.int32, logits.shape, 1).astype(F32)
    neg = jnp.float32(-jnp.inf)
    logits = jnp.where(lane < N_EXPERTS, logits, neg)
    v1 = jnp.max(logits, axis=-1, keepdims=True)
    i1 = jnp.min(jnp.where(logits == v1, lane, float(LANES)), axis=-1, keepdims=True)
    rest = jnp.where(lane == i1, neg, logits)
    v2 = jnp.max(rest, axis=-1, keepdims=True)
    i2 = jnp.min(jnp.where(rest == v2, lane, float(LANES)), axis=-1, keepdims=True)
    e2 = jnp.exp(v2 - v1)
    w1 = 1.0 / (1.0 + e2)
    w2 = e2 / (1.0 + e2)
    o_ref[...] = (jnp.where(lane == i1, w1, 0.0) + jnp.where(lane == i2, w2, 0.0)
                  + jnp.where(lane == N_EXPERTS, i1, 0.0) + jnp.where(lane == N_EXPERTS + 1, i2, 0.0)
                  + jnp.where(lane == N_EXPERTS + 2, w1, 0.0) + jnp.where(lane == N_EXPERTS + 3, w2, 0.0))


def _router_gates(x, gain, layer, mod, router_pad, bias_pad, *, tb, lc, tm):
    m, d = x.shape
    return pl.pallas_call(
        functools.partial(_router_body, tm=tm, tb=tb, lc=lc),
        grid=(m // tm,),
        in_specs=[pl.BlockSpec((tm, d), lambda i: (i, 0)),
                  pl.BlockSpec((None, 1, d), lambda i: (layer, 0, 0)),
                  pl.BlockSpec((8, d), lambda i: (0, 3)),
                  pl.BlockSpec((8, d), lambda i: (0, 4)),
                  pl.BlockSpec((d, LANES), lambda i: (0, 0)),
                  pl.BlockSpec((1, LANES), lambda i: (0, 0))],
        out_specs=pl.BlockSpec((tm, LANES), lambda i: (i, 0)),
        out_shape=jax.ShapeDtypeStruct((m, LANES), F32),
        compiler_params=_cp(("arbitrary",)),
    )(x, gain, mod, mod, router_pad, bias_pad)


def _s5_tables(lam_re, lam_im, log_step, b_re, b_im, c_re, c_im):
    hp = lax.Precision.HIGHEST
    t = S5_T
    lam_re, lam_im = lam_re.astype(F32), lam_im.astype(F32)
    b_re, b_im = b_re.astype(F32), b_im.astype(F32)
    c_re, c_im = c_re.astype(F32), c_im.astype(F32)
    step = jnp.exp(log_step.astype(F32))[..., None]
    tau = jnp.arange(t + 1, dtype=F32)[None, :, None, None]
    mag = jnp.exp((lam_re * step)[:, None] * tau)
    ang = (lam_im * step)[:, None] * tau
    pw_re, pw_im = mag * jnp.cos(ang), mag * jnp.sin(ang)
    nr, ni = pw_re[:, 1] - 1.0, pw_im[:, 1]
    den = lam_re * lam_re + lam_im * lam_im
    q_re = (nr * lam_re + ni * lam_im) / den
    q_im = (ni * lam_re - nr * lam_im) / den
    bb_re = q_re[..., None] * b_re - q_im[..., None] * b_im
    bb_im = q_re[..., None] * b_im + q_im[..., None] * b_re

    def cmul(ar, ai, br, bi):
        return ar * br - ai * bi, ar * bi + ai * br

    x_re, x_im = cmul(pw_re[:, :t, :, :, None], pw_im[:, :t, :, :, None], bb_re[:, None], bb_im[:, None])
    kd = (jnp.einsum('ghp,dtgpk->dtghk', c_re, x_re, precision=hp)
          - jnp.einsum('ghp,dtgpk->dtghk', c_im, x_im, precision=hp))
    kf, kb = kd[0], kd[1]
    kall = jnp.concatenate([kb[1:][::-1], (kf[0] + kb[0])[None], kf[1:]], axis=0)
    idx = (jnp.arange(t)[None, :] - jnp.arange(t)[:, None]) + (t - 1)
    kbig = jnp.transpose(kall[idx], (2, 0, 4, 1, 3)).reshape(S5_G, t * S5_H, t * S5_H)

    def chunk_in(d, powers):
        gr, gi = cmul(pw_re[d][powers][..., None], pw_im[d][powers][..., None], bb_re[d][None], bb_im[d][None])
        pack = lambda g: jnp.transpose(g, (1, 0, 3, 2)).reshape(S5_G, t * S5_H, S5_P)
        return pack(gr), pack(gi)

    def chunk_out(d, powers):
        cr, ci = cmul(c_re[None], c_im[None], pw_re[d][powers][:, :, None, :], pw_im[d][powers][:, :, None, :])
        pack = lambda g: jnp.transpose(g, (1, 3, 0, 2)).reshape(S5_G, S5_P, t * S5_H)
        return pack(cr), -pack(ci)

    gin = jnp.stack(chunk_in(0, t - 1 - jnp.arange(t)) + chunk_in(1, jnp.arange(t)), axis=1)
    cout = jnp.stack(chunk_out(0, 1 + jnp.arange(t)) + chunk_out(1, t - jnp.arange(t)), axis=1)
    lam_t = jnp.stack([pw_re[0, t], pw_im[0, t], pw_re[1, t], pw_im[1, t]], axis=1)
    return kbig.astype(BF16), gin.astype(BF16), cout.astype(BF16), lam_t[:, :, None, :]


def _s5_body(u_ref, k_ref, gin_ref, cout_ref, lam_ref, o_ref, g_s, h_s, *, nb, ncb, ncc):
    u = u_ref[...]
    for q in range(4):
        g_s[q] = jnp.dot(u, gin_ref[q], preferred_element_type=F32)
    lfr, lfi, lbr, lbi = lam_ref[0], lam_ref[1], lam_ref[2], lam_ref[3]
    nlc = ncb - ncc

    def step(n, carry):
        out = []
        for b in range(nb):
            fr, fi, br, bi = carry[4 * b:4 * b + 4]
            rf = b * ncb + n
            nb_idx = jnp.where(n < ncc, ncc - 1 - n, ncc + (nlc - 1) - (n - ncc))
            rb = b * ncb + nb_idx
            h_s[0, pl.ds(rf, 1), :] = fr
            h_s[1, pl.ds(rf, 1), :] = fi
            h_s[2, pl.ds(rb, 1), :] = br
            h_s[3, pl.ds(rb, 1), :] = bi
            out += [lfr * fr - lfi * fi + g_s[0, pl.ds(rf, 1), :],
                    lfr * fi + lfi * fr + g_s[1, pl.ds(rf, 1), :],
                    lbr * br - lbi * bi + g_s[2, pl.ds(rb, 1), :],
                    lbr * bi + lbi * br + g_s[3, pl.ds(rb, 1), :]]
        return tuple(out)

    zero = jnp.zeros((1, S5_P), F32)
    lax.fori_loop(0, ncb, step, (zero,) * (4 * nb))
    y = jnp.dot(u, k_ref[...], preferred_element_type=F32)
    for q in range(4):
        y = y + jnp.dot(h_s[q].astype(BF16), cout_ref[q], preferred_element_type=F32)
    o_ref[...] = y


def _s5_mix(ug, tables, layer, *, nb, ncb, ncc):
    kbig, gin, cout, lam_t = tables
    g, nc, tw = ug.shape
    return pl.pallas_call(
        functools.partial(_s5_body, nb=nb, ncb=ncb, ncc=ncc),
        grid=(g,),
        in_specs=[pl.BlockSpec((None, nc, tw), lambda i: (i, 0, 0)),
                  pl.BlockSpec((None, None, tw, tw), lambda i: (layer, i, 0, 0)),
                  pl.BlockSpec((None, None, 4, tw, S5_P), lambda i: (layer, i, 0, 0, 0)),
                  pl.BlockSpec((None, None, 4, S5_P, tw), lambda i: (layer, i, 0, 0, 0)),
                  pl.BlockSpec((None, None, 4, 1, S5_P), lambda i: (layer, i, 0, 0, 0))],
        out_specs=pl.BlockSpec((None, nc, tw), lambda i: (i, 0, 0)),
        out_shape=jax.ShapeDtypeStruct((g, nc, tw), F32),
        scratch_shapes=[pltpu.VMEM((4, nc, S5_P), F32), pltpu.VMEM((4, nc, S5_P), F32)],
        compiler_params=_cp(("arbitrary",)),
    )(ug, kbig, gin, cout, lam_t)


def _s5_readout_body(y_ref, u_ref, d_ref, w_ref, o_ref, wb_ref):
    @pl.when(pl.program_id(0) == 0)
    def _():
        wb_ref[...] = w_ref[...].astype(BF16)

    y = _gelu_tanh(y_ref[...] + d_ref[...] * u_ref[...])
    z = jnp.dot(y.astype(BF16), wb_ref[...], preferred_element_type=F32)
    o_ref[...] = (y * _sigmoid(z)).astype(o_ref.dtype)


def _s5_readout(y, u, d_skip, w_glu, layer, *, tm):
    m = y.shape[0]
    return pl.pallas_call(
        _s5_readout_body,
        grid=(m // tm,),
        in_specs=[pl.BlockSpec((tm, S5_W), lambda i: (i, 0)),
                  pl.BlockSpec((tm, S5_W), lambda i: (i, 0)),
                  pl.BlockSpec((None, 1, S5_W), lambda i: (layer, 0, 0)),
                  pl.BlockSpec((None, S5_W, S5_W), lambda i: (layer, 0, 0))],
        out_specs=pl.BlockSpec((tm, S5_W), lambda i: (i, 0)),
        out_shape=jax.ShapeDtypeStruct((m, S5_W), BF16),
        scratch_shapes=[pltpu.VMEM((S5_W, S5_W), BF16)],
        compiler_params=_cp(("arbitrary",)),
    )(y, u, d_skip, w_glu)


def _rw_feat_body(z_ref, up_ref, dn_ref, mu_ref, wup_ref, aup_ref, gup_ref, vec_ref, ones_ref, tri_ref,
                  v_ref, g_ref, bonus_ref, ah_ref, rh_ref, bh_ref, kh_ref, bt_ref, kt_ref, pt_ref,
                  *, tt, tb, lc):
    i = pl.program_id(0)
    start = i * tt
    b = start // tb
    pos = (start - b * tb) + lax.broadcasted_iota(jnp.int32, (tt, 1), 0)
    is_ctx = pos < lc
    tl = pos - lc
    col = tl & (GRID_W - 1)
    l_lat = tb - lc

    z = z_ref[...]
    prev = pltpu.roll(z, 1, axis=0)
    nxt = pltpu.roll(z, tt - 1, axis=0)
    if tt > GRID_W:
        up = jnp.concatenate([up_ref[...], z[:tt - GRID_W]], axis=0)
        down = jnp.concatenate([z[GRID_W:], dn_ref[...]], axis=0)
    else:
        up, down = up_ref[...], dn_ref[...]
    m_prev = jnp.logical_or(jnp.logical_and(is_ctx, pos >= 1), jnp.logical_and(tl >= 0, col >= 1))
    m_next = jnp.logical_or(jnp.logical_and(is_ctx, pos <= lc - 2),
                            jnp.logical_and(tl >= 0, col <= GRID_W - 2))
    m_up = tl >= GRID_W
    m_down = jnp.logical_and(tl >= 0, tl < l_lat - GRID_W)
    prev = jnp.where(m_prev, prev, 0.0)
    nxt = jnp.where(m_next, nxt, 0.0)
    up = jnp.where(is_ctx, prev, jnp.where(m_up, up, 0.0))
    down = jnp.where(is_ctx, nxt, jnp.where(m_down, down, 0.0))
    l4 = lax.broadcasted_iota(jnp.int32, (1, ZR_COLS), 1) & 3
    shifted = jnp.where(l4 == 0, prev, jnp.where(l4 == 1, nxt, jnp.where(l4 == 2, up, down)))
    z = z + mu_ref[...] * (shifted - z)

    r = z[:, 0:RW]
    k = z[:, RW:2 * RW]
    v = z[:, 2 * RW:3 * RW]
    lora_w = jnp.tanh(z[:, 3 * RW:3 * RW + LANES]).astype(BF16)
    lora_a = z[:, 3 * RW + LANES:3 * RW + 2 * LANES].astype(BF16)
    lora_g = _sigmoid(z[:, 3 * RW + 2 * LANES:3 * RW + 4 * LANES]).astype(BF16)
    k_k, k_a, r_k = vec_ref[0:1, :], vec_ref[1:2, :], vec_ref[2:3, :]
    ones = ones_ref[...]

    kk = k * k_k
    kk = kk * lax.rsqrt(jnp.maximum(_split_dot(kk * kk, ones), 1e-24))
    ksum = jnp.zeros_like(k)
    for d in range(2):
        w0 = vec_ref[3 + d:4 + d, :]
        a0 = vec_ref[5 + d:6 + d, :]
        wl = w0 + jnp.dot(lora_w, wup_ref[d], preferred_element_type=F32)
        lw = -math.exp(-0.5) * _sigmoid(wl)
        lw2 = lw * LOG2_E
        a = _sigmoid(a0 + jnp.dot(lora_a, aup_ref[d], preferred_element_type=F32))
        k_d = k * (1.0 + (a - 1.0) * k_a)
        bb = kk * a
        ksum = ksum + k_d
        lw_hi = lw2.astype(BF16)
        lw_lo = (lw2 - lw_hi.astype(F32)).astype(BF16)
        cl = (jnp.dot(tri_ref[d, 0], lw_hi, preferred_element_type=F32)
              + jnp.dot(tri_ref[d, 0], lw_lo, preferred_element_type=F32))
        rem = (jnp.dot(tri_ref[d, 1], lw_hi, preferred_element_type=F32)
               + jnp.dot(tri_ref[d, 1], lw_lo, preferred_element_type=F32))
        p_in = jnp.exp2(cl)
        p_inv = jnp.exp2(-cl)
        p_rem = jnp.exp2(rem)
        ah_ref[d] = (-kk * jnp.exp2(cl - lw2)).astype(BF16)
        rh_ref[d] = (r * p_in).astype(BF16)
        bh_ref[d] = (bb * p_inv).astype(BF16)
        kh_ref[d] = (k_d * p_inv).astype(BF16)
        bt_ref[d] = (bb * p_rem).astype(BF16)
        kt_ref[d] = (k_d * p_rem).astype(BF16)
        last = 0 if d == 1 else WKV_CHUNK - 1
        for cidx in range(tt // WKV_CHUNK):
            row = cidx * WKV_CHUNK + last
            pt_ref[d, cidx] = p_in[row:row + 1, :]
    v_ref[...] = v.astype(BF16)
    g_ref[...] = jnp.dot(lora_g, gup_ref[...], preferred_element_type=F32)
    bonus_ref[...] = _split_dot(r * ksum * r_k, ones) * v


def _rw_features(zr, layer, mu_pad, wup, aup, gup, vecs, ones_blk, tri, *, tb, lc, tt):
    m = zr.shape[0]
    nh = tt // GRID_W
    nblk64 = m // GRID_W
    ncht = tt // WKV_CHUNK
    full = pl.BlockSpec((tt, RW), lambda i: (i, 0))
    per_dir = pl.BlockSpec((2, tt, RW), lambda i: (0, i, 0))
    one = jax.ShapeDtypeStruct((m, RW), F32)
    two = jax.ShapeDtypeStruct((2, m, RW), BF16)
    return pl.pallas_call(
        functools.partial(_rw_feat_body, tt=tt, tb=tb, lc=lc),
        grid=(m // tt,),
        in_specs=[pl.BlockSpec((tt, ZR_COLS), lambda i: (i, 0)),
                  pl.BlockSpec((GRID_W, ZR_COLS), lambda i: (jnp.maximum(i * nh - 1, 0), 0)),
                  pl.BlockSpec((GRID_W, ZR_COLS), lambda i: (jnp.minimum((i + 1) * nh, nblk64 - 1), 0)),
                  pl.BlockSpec((None, 1, ZR_COLS), lambda i: (layer, 0, 0)),
                  pl.BlockSpec((None, 2, LANES, RW), lambda i: (layer, 0, 0, 0)),
                  pl.BlockSpec((None, 2, LANES, RW), lambda i: (layer, 0, 0, 0)),
                  pl.BlockSpec((None, 2 * LANES, RW), lambda i: (layer, 0, 0)),
                  pl.BlockSpec((None, 8, RW), lambda i: (layer, 0, 0)),
                  pl.BlockSpec((RW, RW), lambda i: (0, 0)),
                  pl.BlockSpec((2, 2, tt, tt), lambda i: (0, 0, 0, 0))],
        out_specs=[full, full, full, per_dir, per_dir, per_dir, per_dir, per_dir, per_dir,
                   pl.BlockSpec((2, ncht, 1, RW), lambda i: (0, i, 0, 0))],
        out_shape=[jax.ShapeDtypeStruct((m, RW), BF16), one, one, two, two, two, two, two, two,
                   jax.ShapeDtypeStruct((2, m // WKV_CHUNK, 1, RW), F32)],
        compiler_params=_cp(("arbitrary",)),
    )(zr, zr, zr, mu_pad, wup, aup, gup, vecs, ones_blk, tri)


def _chunk_tri(tt):
    t = jnp.arange(tt)[:, None]
    s = jnp.arange(tt)[None, :]
    same = (t // WKV_CHUNK) == (s // WKV_CHUNK)
    fwd = jnp.stack([same & (s <= t), same & (s > t)])
    bwd = jnp.stack([same & (s >= t), same & (s < t)])
    return jnp.stack([fwd, bwd]).astype(BF16)


def _dot_nt(a, b):
    return lax.dot_general(a, b, (((1,), (1,)), ((), ())), preferred_element_type=F32)


def _dot_tn(a, b):
    return lax.dot_general(a, b, (((0,), (0,)), ((), ())), preferred_element_type=F32)


def _wkv_body(*refs):
    c = WKV_CHUNK
    ins = (refs[0:8], refs[8:16])
    y_refs = refs[16:18]
    s_ref = refs[18]

    @pl.when(pl.program_id(1) == 0)
    def _():
        s_ref[...] = jnp.zeros_like(s_ref)

    t_idx = lax.broadcasted_iota(jnp.int32, (2 * c, LANES), 0)
    lane = lax.broadcasted_iota(jnp.int32, (2 * c, LANES), 1)
    diff = (t_idx & (c - 1)) - (lane & (c - 1))
    tri = (jnp.where(t_idx < c, diff, diff + 1) > 0, jnp.where(t_idx < c, -diff, 1 - diff) > 0)
    same_head = (t_idx // c) == (lane // c)
    lo = lax.broadcasted_iota(jnp.int32, (c, LANES), 1) < c

    def bd(x):
        zero = jnp.zeros_like(x)
        return jnp.concatenate([jnp.where(lo, x, zero), jnp.where(lo, zero, x)], axis=0)

    chains = [(d, p) for d in range(2) for p in range(HEADS // 2)]
    n = range(len(chains))
    sl = [slice(p * LANES, (p + 1) * LANES) for _, p in chains]
    ah, rh, bh, kh, bt, kt, vv, pt = ([ins[d][q] for d, _ in chains] for q in range(8))
    ar = [jnp.concatenate([ah[i][:, sl[i]], rh[i][:, sl[i]]], axis=0) for i in n]
    s_old = [s_ref[d, p] for d, p in chains]
    pbk = [_dot_nt(ar[i], jnp.concatenate([bd(bh[i][:, sl[i]]), bd(kh[i][:, sl[i]])], axis=0)) for i in n]
    ps = [_dot_nt(ar[i], s_old[i].astype(BF16)) for i in n]
    pb = [jnp.where(tri[chains[i][0]], pbk[i][:, :LANES], 0.0) for i in n]
    pk = [jnp.where(tri[chains[i][0]], pbk[i][:, LANES:], 0.0) for i in n]
    bdv = [bd(vv[i][:, sl[i]]) for i in n]
    x = [ps[i][:c] + jnp.dot(pk[i][:c].astype(BF16), bdv[i], preferred_element_type=F32) for i in n]
    lp = [pb[i][:c].astype(BF16) for i in n]
    for it in range(6):
        if it < 5:
            prod = [jnp.dot(lp[i], jnp.concatenate([bd(x[i].astype(BF16)), bd(lp[i])], axis=1),
                            preferred_element_type=F32) for i in n]
            x = [x[i] + prod[i][:, :LANES] for i in n]
            lp = [prod[i][:, LANES:].astype(BF16) for i in n]
        else:
            x = [x[i] + jnp.dot(lp[i], bd(x[i].astype(BF16)), preferred_element_type=F32) for i in n]
    u = [x[i].astype(BF16) for i in n]
    for i in n:
        y_refs[chains[i][0]][:, sl[i]] = ps[i][c:] + jnp.dot(
            jnp.concatenate([pb[i][c:], pk[i][c:]], axis=1).astype(BF16),
            jnp.concatenate([bd(u[i]), bdv[i]], axis=0), preferred_element_type=F32)
    upd = [_dot_tn(jnp.concatenate([u[i], vv[i][:, sl[i]]], axis=0),
                   jnp.concatenate([bt[i][:, sl[i]], kt[i][:, sl[i]]], axis=0)) for i in n]
    for i, (d, p) in enumerate(chains):
        s_ref[d, p] = s_old[i] * pt[i][:, sl[i]] + jnp.where(same_head, upd[i], 0.0)


def _wkv_scan(ah, rh, bh, kh, bt, kt, v, pt, *, nb, tb, lc):
    m = v.shape[0]
    c = WKV_CHUNK
    ncb, ncc = tb // c, lc // c
    nlc = ncb - ncc

    def chunk(d, b, n):
        if d == 0:
            return b * ncb + n
        return b * ncb + jnp.where(n < ncc, ncc - 1 - n, ncc + (nlc - 1) - (n - ncc))

    def specs(d):
        per_dir = pl.BlockSpec((None, c, RW), lambda b, n: (d, chunk(d, b, n), 0))
        return [per_dir] * 6 + [pl.BlockSpec((c, RW), lambda b, n: (chunk(d, b, n), 0)),
                                pl.BlockSpec((None, None, 1, RW), lambda b, n: (d, chunk(d, b, n), 0, 0))]

    out = jax.ShapeDtypeStruct((m, RW), F32)
    args = (ah, rh, bh, kh, bt, kt, v, pt)
    return pl.pallas_call(
        _wkv_body,
        grid=(nb, ncb),
        in_specs=specs(0) + specs(1),
        out_specs=[pl.BlockSpec((c, RW), lambda b, n: (chunk(0, b, n), 0)),
                   pl.BlockSpec((c, RW), lambda b, n: (chunk(1, b, n), 0))],
        out_shape=[out, out],
        scratch_shapes=[pltpu.VMEM((2, HEADS // 2, LANES, LANES), F32)],
        compiler_params=_cp(("arbitrary", "arbitrary")),
    )(*args, *args)


def _rw_readout_body(yf_ref, yb_ref, bonus_ref, g_ref, gn_ref, ones_ref, o_ref):
    y = yf_ref[...] + yb_ref[...]
    ones = ones_ref[...]
    mean = _split_dot(y, ones) * (1.0 / HEAD)
    yc = y - mean
    var = _split_dot(yc * yc, ones) * (1.0 / HEAD)
    yn = yc * lax.rsqrt(var + GN_EPS) * gn_ref[0:1, :] + gn_ref[1:2, :]
    o_ref[...] = ((yn + bonus_ref[...]) * g_ref[...]).astype(o_ref.dtype)


def _rw_readout(y_f, y_b, bonus, g, gn, layer, ones_blk, *, tm):
    m = y_f.shape[0]
    full = pl.BlockSpec((tm, RW), lambda i: (i, 0))
    return pl.pallas_call(
        _rw_readout_body,
        grid=(m // tm,),
        in_specs=[full, full, full, full,
                  pl.BlockSpec((None, 8, RW), lambda i: (layer, 0, 0)),
                  pl.BlockSpec((RW, RW), lambda i: (0, 0))],
        out_specs=full,
        out_shape=jax.ShapeDtypeStruct((m, RW), BF16),
        compiler_params=_cp(("arbitrary",)),
    )(y_f, y_b, bonus, g, gn, ones_blk)


MOE_TILE = 512


def _moe_plan(route, tme):
    m = route.shape[0]
    n_asg = 2 * m
    idx = route[:, N_EXPERTS:N_EXPERTS + 2].astype(jnp.int32)
    e_flat = idx.T.reshape(n_asg)
    w_flat = route[:, N_EXPERTS + 2:N_EXPERTS + 4].T.reshape(n_asg)
    tok_flat = jnp.tile(jnp.arange(m, dtype=jnp.int32), 2)
    order = jnp.argsort(e_flat, stable=True).astype(jnp.int32)
    e_sorted, tok_sorted, w_sorted = e_flat[order], tok_flat[order], w_flat[order]
    experts = jnp.arange(N_EXPERTS, dtype=jnp.int32)
    cnt = jnp.sum((e_flat[:, None] == experts[None, :]).astype(jnp.int32), axis=0)
    start = jnp.cumsum(cnt) - cnt
    padded = ((cnt + tme - 1) // tme) * tme
    off_end = jnp.cumsum(padded)
    off = off_end - padded
    n_rows = ((n_asg + tme - 1) // tme) * tme + N_EXPERTS * tme
    n_tiles = n_rows // tme
    tile_e = jnp.sum((jnp.arange(n_tiles, dtype=jnp.int32)[:, None] * tme >= off_end[None, :]).astype(jnp.int32),
                     axis=1)
    tile_e = jnp.minimum(tile_e, N_EXPERTS - 1)
    first = jnp.concatenate([jnp.ones((1,), jnp.int32), (tile_e[1:] != tile_e[:-1]).astype(jnp.int32)])
    n_act = (off_end[-1] // tme).reshape(1)
    e_row = jnp.repeat(tile_e, tme)
    q = jnp.arange(n_rows, dtype=jnp.int32) - off[e_row]
    valid = q < cnt[e_row]
    s_row = jnp.clip(start[e_row] + q, 0, n_asg - 1)
    src_tok = jnp.where(valid, tok_sorted[s_row], 0)
    row_w = jnp.where(valid, w_sorted[s_row], 0.0)[:, None]
    dest_sorted = off[e_sorted] + (jnp.arange(n_asg, dtype=jnp.int32) - start[e_sorted])
    pos = jnp.zeros((n_asg,), jnp.int32).at[order].set(dest_sorted, unique_indices=True)
    return dict(src_tok=src_tok, row_w=row_w, pos=pos, tile_e=tile_e, first=first, n_act=n_act,
                n_rows=n_rows, n_tiles=n_tiles)


def _gather_body(idx_ref, src_ref, o_ref, sem, *, rows):
    base = pl.program_id(0) * rows

    def issue(r, carry):
        pltpu.make_async_copy(src_ref.at[idx_ref[base + r]], o_ref.at[r], sem).start()
        return carry

    lax.fori_loop(0, rows, issue, 0)
    pltpu.make_async_copy(o_ref, o_ref, sem).wait()


def _gather_rows(src3, idx, *, rows):
    n_rows = idx.shape[0]
    blk = (rows,) + src3.shape[1:]
    return pl.pallas_call(
        functools.partial(_gather_body, rows=rows),
        grid_spec=pltpu.PrefetchScalarGridSpec(
            num_scalar_prefetch=1, grid=(n_rows // rows,),
            in_specs=[pl.BlockSpec(memory_space=pl.ANY)],
            out_specs=pl.BlockSpec(blk, lambda i, idx: (i, 0, 0)),
            scratch_shapes=[pltpu.SemaphoreType.DMA(())]),
        out_shape=jax.ShapeDtypeStruct((n_rows,) + src3.shape[1:], src3.dtype),
        compiler_params=_cp(("arbitrary",)),
    )(idx, src3)


def _moe_up_body(te_ref, first_ref, nact_ref, x_ref, wg_ref, wu_ref, o_ref, wgb_ref, wub_ref):
    i = pl.program_id(1)

    @pl.when(i < nact_ref[0])
    def _():
        @pl.when(first_ref[i] == 1)
        def _():
            wgb_ref[...] = wg_ref[...].astype(BF16)
            wub_ref[...] = wu_ref[...].astype(BF16)

        x = x_ref[...]
        g = jnp.dot(x, wgb_ref[...], preferred_element_type=F32)
        u = jnp.dot(x, wub_ref[...], preferred_element_type=F32)
        o_ref[...] = (g * _sigmoid(g) * u).astype(o_ref.dtype)

    @pl.when(i >= nact_ref[0])
    def _():
        o_ref[...] = jnp.zeros_like(o_ref)


def _moe_up(xg, wg, wu, layer, plan, *, tme, tn):
    p, k = xg.shape
    f = wg.shape[-1]

    def row(i, nact):
        return jnp.minimum(i, nact[0] - 1)

    wspec = pl.BlockSpec((None, None, k, tn), lambda j, i, te, fi, na: (layer, te[row(i, na)], 0, j))
    return pl.pallas_call(
        _moe_up_body,
        grid_spec=pltpu.PrefetchScalarGridSpec(
            num_scalar_prefetch=3, grid=(f // tn, p // tme),
            in_specs=[pl.BlockSpec((tme, k), lambda j, i, te, fi, na: (row(i, na), 0)), wspec, wspec],
            out_specs=pl.BlockSpec((tme, tn), lambda j, i, te, fi, na: (i, j)),
            scratch_shapes=[pltpu.VMEM((k, tn), BF16), pltpu.VMEM((k, tn), BF16)]),
        out_shape=jax.ShapeDtypeStruct((p, f), BF16),
        compiler_params=_cp(("arbitrary", "arbitrary")),
    )(plan["tile_e"], plan["first"], plan["n_act"], xg, wg, wu)


def _moe_down_body(te_ref, first_ref, nact_ref, a_ref, w_ref, rw_ref, o_ref, wb_ref):
    i = pl.program_id(1)

    @pl.when(i < nact_ref[0])
    def _():
        @pl.when(first_ref[i] == 1)
        def _():
            wb_ref[...] = w_ref[...].astype(BF16)

        o_ref[...] = rw_ref[...] * jnp.dot(a_ref[...], wb_ref[...], preferred_element_type=F32)

    @pl.when(i >= nact_ref[0])
    def _():
        o_ref[...] = jnp.zeros_like(o_ref)


def _moe_down(act, wd, layer, plan, *, tme, tn):
    p, k = act.shape
    d = wd.shape[-1]

    def row(i, nact):
        return jnp.minimum(i, nact[0] - 1)

    return pl.pallas_call(
        _moe_down_body,
        grid_spec=pltpu.PrefetchScalarGridSpec(
            num_scalar_prefetch=3, grid=(d // tn, p // tme),
            in_specs=[pl.BlockSpec((tme, k), lambda j, i, te, fi, na: (row(i, na), 0)),
                      pl.BlockSpec((None, None, k, tn), lambda j, i, te, fi, na: (layer, te[row(i, na)], 0, j)),
                      pl.BlockSpec((tme, 1), lambda j, i, te, fi, na: (row(i, na), 0))],
            out_specs=pl.BlockSpec((tme, tn), lambda j, i, te, fi, na: (i, j)),
            scratch_shapes=[pltpu.VMEM((k, tn), BF16)]),
        out_shape=jax.ShapeDtypeStruct((p, d), F32),
        compiler_params=_cp(("arbitrary", "arbitrary")),
    )(plan["tile_e"], plan["first"], plan["n_act"], act, wd, plan["row_w"])


def _moe_combine_body(pos_ref, x_ref, y_ref, tab_ref, o_ref, buf, sem, *, tc, m, tb, lc):
    i = pl.program_id(0)
    base = i * tc

    def issue(r, carry):
        pltpu.make_async_copy(y_ref.at[pos_ref[base + r]], buf.at[0, r], sem).start()
        pltpu.make_async_copy(y_ref.at[pos_ref[m + base + r]], buf.at[1, r], sem).start()
        return carry

    lax.fori_loop(0, tc, issue, 0)
    pltpu.make_async_copy(buf, buf, sem).wait()
    gate = _row_mod(tab_ref, i, tc, tb, lc)
    o_ref[...] = x_ref[...] + gate * (buf[0] + buf[1])


def _moe_combine(x, y, pos, mod, gate_idx, *, tb, lc, tc):
    m, d = x.shape
    return pl.pallas_call(
        functools.partial(_moe_combine_body, tc=tc, m=m, tb=tb, lc=lc),
        grid_spec=pltpu.PrefetchScalarGridSpec(
            num_scalar_prefetch=1, grid=(m // tc,),
            in_specs=[pl.BlockSpec((tc, d), lambda i, pos: (i, 0)),
                      pl.BlockSpec(memory_space=pl.ANY),
                      pl.BlockSpec((8, d), lambda i, pos: (0, gate_idx))],
            out_specs=pl.BlockSpec((tc, d), lambda i, pos: (i, 0)),
            scratch_shapes=[pltpu.VMEM((2, tc, d), F32), pltpu.SemaphoreType.DMA(())]),
        out_shape=jax.ShapeDtypeStruct((m, d), F32),
        input_output_aliases={1: 0},
        compiler_params=_cp(("arbitrary",)),
    )(pos, x, y, mod)


def _pad_rows(x, rows):
    return jnp.concatenate([x, jnp.zeros((rows - x.shape[0],) + x.shape[1:], x.dtype)], axis=0)


def _pick_tile(n, target):
    best = 8
    for t in range(8, target + 1, 8):
        if n % t == 0:
            best = t
    return best


def kernel(x, c, ctx, c_ctx, w_ada, b_ada, norm1, norm2, norm_f, w_in, s5_lambda_re, s5_lambda_im, s5_log_step, s5_b_re, s5_b_im, s5_c_re, s5_c_im, s5_d, s5_w_glu, rw_mu, rw_w0, rw_w_up, rw_a0, rw_a_up, rw_g_up, rw_k_k, rw_k_a, rw_r_k, rw_gn_w, rw_gn_b, w_proj_s5, w_proj_rw, w_out, ffn_w_gate, ffn_w_up, ffn_w_down, moe_router, moe_router_bias, moe_w_gate, moe_w_up, moe_w_down):
    nb, l, d = x.shape
    lc = ctx.shape[1]
    tb = lc + l
    m = nb * tb
    depth = w_in.shape[0]
    tt = lc
    assert lc % GRID_W == 0 and l % tt == 0 and l % GRID_W == 0 and nb * HEADS * 4 == LANES
    assert tb % S5_T == 0 and lc % S5_T == 0 and tt % WKV_CHUNK == 0
    tm = _pick_tile(tb, 1088)
    ncb, ncc = tb // S5_T, lc // S5_T

    xs = jnp.concatenate([ctx, x], axis=1).reshape(m, d)
    silu_rows = _pad_rows(jnp.concatenate([c_ctx[None], c], axis=0), 8)
    silu_rows = (silu_rows * _sigmoid(silu_rows)).astype(BF16)

    ones_blk = (jnp.arange(RW)[:, None] // HEAD == jnp.arange(RW)[None, :] // HEAD).astype(BF16)
    tri = _chunk_tri(tt)
    mu_pad = jnp.pad(rw_mu, ((0, 0), (0, ZR_COLS - rw_mu.shape[1])))[:, None, :]
    zeros64 = jnp.zeros((depth, 1, DECAY_LORA, RW), F32)
    wup = jnp.concatenate([jnp.concatenate([rw_w_up[:, :1], zeros64], axis=2),
                           jnp.concatenate([zeros64, rw_w_up[:, 1:]], axis=2)], axis=1).astype(BF16)
    aup = jnp.concatenate([jnp.concatenate([rw_a_up[:, :1], zeros64], axis=2),
                           jnp.concatenate([zeros64, rw_a_up[:, 1:]], axis=2)], axis=1).astype(BF16)
    gup = jnp.pad(rw_g_up, ((0, 0), (0, 2 * LANES - GATE_LORA), (0, 0))).astype(BF16)
    vecs = jnp.stack([rw_k_k, rw_k_a, rw_r_k.reshape(depth, RW), rw_w0[:, 0], rw_w0[:, 1],
                      rw_a0[:, 0], rw_a0[:, 1], jnp.zeros_like(rw_k_k)], axis=1)
    gn = jnp.pad(jnp.stack([rw_gn_w, rw_gn_b], axis=1), ((0, 0), (0, 6), (0, 0)))
    d_skip = s5_d.reshape(depth, 1, S5_W)
    norm1, norm2 = norm1[:, None, :], norm2[:, None, :]
    w_gates = w_in[:, :, S5_W + 3 * RW + LORA_COLS:]
    tables = jax.vmap(_s5_tables)(s5_lambda_re, s5_lambda_im, s5_log_step, s5_b_re, s5_b_im, s5_c_re, s5_c_im)

    for i in range(depth):
        mod = _matmul(silu_rows, w_ada, (i,), 6 * d, tm=8, tn=1024, bias=b_ada[i][None])

        h = _norm_mod(xs, norm1, i, mod, 0, tb=tb, lc=lc, tm=tt)
        u = _matmul(h, w_in, (i,), S5_W, tm=tm, tn=512)
        zr = _matmul(h, w_in, (i,), ZR_COLS, col_off=S5_W, tm=tm, tn=512)
        zg = _matmul(h, w_gates, (i,), 2 * d, tm=tm, tn=512)

        ug = jnp.transpose(u.reshape(nb * ncb, S5_T, S5_G, S5_H), (2, 0, 1, 3))
        ug = ug.reshape(S5_G, nb * ncb, S5_T * S5_H).astype(BF16)
        yg = _s5_mix(ug, tables, i, nb=nb, ncb=ncb, ncc=ncc)
        ys = jnp.transpose(yg.reshape(S5_G, nb * ncb, S5_T, S5_H), (1, 2, 0, 3)).reshape(m, S5_W)
        y_s5 = _s5_readout(ys, u, d_skip, s5_w_glu, i, tm=tt)

        v, g, bonus, ah, rh, bh, kh, bt, kt, pt = _rw_features(zr, i, mu_pad, wup, aup, gup, vecs, ones_blk, tri,
                                                               tb=tb, lc=lc, tt=tt)
        y_f, y_b = _wkv_scan(ah, rh, bh, kh, bt, kt, v, pt, nb=nb, tb=tb, lc=lc)
        y_rw = _rw_readout(y_f, y_b, bonus, g, gn, i, ones_blk, tm=tt)

        mrg = _merge(y_s5, y_rw, w_proj_s5, w_proj_rw, i, zg, tm=tm, tn=512)
        xs = _down_resid(mrg, w_out, (i,), xs, mod, 2, tb=tb, lc=lc, tm=tm, tn=512)

        j = i // 2
        h2 = _norm_mod(xs, norm2, i, mod, 3, tb=tb, lc=lc, tm=tt)
        if i % 2 == 0:
            act = _swiglu_up(h2, ffn_w_gate, ffn_w_up, (j,), tm=tm, tn=512)
            xs = _down_resid(act, ffn_w_down, (j,), xs, mod, 5, tb=tb, lc=lc, tm=_pick_tile(tb, 544), tn=512)
        else:
            router_pad = jnp.pad(moe_router[j], ((0, 0), (0, LANES - N_EXPERTS)))
            bias_pad = jnp.pad(moe_router_bias[j], (0, LANES - N_EXPERTS))[None]
            route = _router_gates(xs, norm2, i, mod, router_pad, bias_pad, tb=tb, lc=lc, tm=tt)
            plan = _moe_plan(route, MOE_TILE)
            xg = _gather_rows(h2.reshape(m, d // LANES, LANES), plan["src_tok"], rows=MOE_TILE)
            act = _moe_up(xg.reshape(plan["n_rows"], d), moe_w_gate, moe_w_up, j, plan, tme=MOE_TILE, tn=256)
            y_exp = _moe_down(act, moe_w_down, j, plan, tme=MOE_TILE, tn=512)
            xs = _moe_combine(xs, y_exp, plan["pos"], mod, 5, tb=tb, lc=lc, tc=tt)

    return _final_norm(xs.reshape(nb, tb, d), norm_f[None], lc=lc, tm=tt)
```

```python
import functools
import math

import jax
import jax.numpy as jnp
from jax import lax
from jax.experimental import pallas as pl
from jax.experimental.pallas import tpu as pltpu

F32 = jnp.float32
BF16 = jnp.bfloat16

GRID_W = 64
HEADS = 16
HEAD = 64
RW = HEADS * HEAD
S5_G = 32
S5_H = 16
S5_P = 64
S5_W = S5_G * S5_H
S5_T = 16
WKV_CHUNK = 64
DECAY_LORA = 64
ICLR_LORA = 64
GATE_LORA = 160
LORA_COLS = 2 * DECAY_LORA + 2 * ICLR_LORA + GATE_LORA
LORA_PAD = 512
ZR_COLS = 3 * RW + LORA_PAD
NORM_EPS = 1e-6
GN_EPS = 64e-5
N_EXPERTS = 8
LANES = 128
LOG2_E = 1.4426950408889634
VMEM_LIMIT = 56 * 1024 * 1024


def _cp(sem):
    return pltpu.CompilerParams(dimension_semantics=sem, vmem_limit_bytes=VMEM_LIMIT)


def _sigmoid(x):
    return 0.5 * jnp.tanh(0.5 * x) + 0.5


def _gelu_tanh(x):
    return 0.5 * x * (1.0 + jnp.tanh(math.sqrt(2.0 / math.pi) * (x + 0.044715 * (x * x * x))))


def _row_mod(tab_ref, tile, tm, tb, lc):
    start = tile * tm
    b = start // tb
    pos = (start - b * tb) + lax.broadcasted_iota(jnp.int32, (tm, 1), 0)
    ctx_row = tab_ref[0:1, :]
    lat_row = tab_ref[pl.ds(1 + b, 1), :]
    return jnp.where(pos < lc, ctx_row, lat_row)


def _split_dot(x, w_bf16):
    hi = x.astype(BF16)
    lo = (x - hi.astype(F32)).astype(BF16)
    return (jnp.dot(hi, w_bf16, preferred_element_type=F32)
            + jnp.dot(lo, w_bf16, preferred_element_type=F32))


def _lead_spec(lead, shape, fn):
    return pl.BlockSpec((None,) * len(lead) + shape, lambda j, i: tuple(lead) + fn(j, i))


def _mm_body(x_ref, w_ref, *rest, has_bias):
    if has_bias:
        b_ref, o_ref, wb_ref = rest
    else:
        o_ref, wb_ref = rest

    @pl.when(pl.program_id(1) == 0)
    def _():
        wb_ref[...] = w_ref[...].astype(BF16)

    acc = jnp.dot(x_ref[...], wb_ref[...], preferred_element_type=F32)
    if has_bias:
        acc = acc + b_ref[...]
    o_ref[...] = acc.astype(o_ref.dtype)


def _matmul(x, w, lead, n_out, *, col_off=0, tm, tn, out_dtype=F32, bias=None):
    m, k = x.shape
    off = col_off // tn
    in_specs = [pl.BlockSpec((tm, k), lambda j, i: (i, 0)),
                _lead_spec(lead, (k, tn), lambda j, i: (0, j + off))]
    args = [x, w]
    if bias is not None:
        in_specs.append(pl.BlockSpec((1, tn), lambda j, i: (0, j)))
        args.append(bias)
    return pl.pallas_call(
        functools.partial(_mm_body, has_bias=bias is not None),
        grid=(n_out // tn, m // tm),
        in_specs=in_specs,
        out_specs=pl.BlockSpec((tm, tn), lambda j, i: (i, j)),
        out_shape=jax.ShapeDtypeStruct((m, n_out), out_dtype),
        scratch_shapes=[pltpu.VMEM((k, tn), BF16)],
        compiler_params=_cp(("arbitrary", "arbitrary")),
    )(*args)


def _swiglu_up_body(x_ref, wg_ref, wu_ref, o_ref, wgb_ref, wub_ref):
    @pl.when(pl.program_id(1) == 0)
    def _():
        wgb_ref[...] = wg_ref[...].astype(BF16)
        wub_ref[...] = wu_ref[...].astype(BF16)

    x = x_ref[...]
    g = jnp.dot(x, wgb_ref[...], preferred_element_type=F32)
    u = jnp.dot(x, wub_ref[...], preferred_element_type=F32)
    o_ref[...] = (g * _sigmoid(g) * u).astype(o_ref.dtype)


def _swiglu_up(x, wg, wu, lead, *, tm, tn):
    m, k = x.shape
    f = wg.shape[-1]
    wspec = _lead_spec(lead, (k, tn), lambda j, i: (0, j))
    return pl.pallas_call(
        _swiglu_up_body,
        grid=(f // tn, m // tm),
        in_specs=[pl.BlockSpec((tm, k), lambda j, i: (i, 0)), wspec, wspec],
        out_specs=pl.BlockSpec((tm, tn), lambda j, i: (i, j)),
        out_shape=jax.ShapeDtypeStruct((m, f), BF16),
        scratch_shapes=[pltpu.VMEM((k, tn), BF16), pltpu.VMEM((k, tn), BF16)],
        compiler_params=_cp(("arbitrary", "arbitrary")),
    )(x, wg, wu)


def _down_resid_body(a_ref, w_ref, x_ref, tab_ref, *rest, tm, tb, lc, expert):
    if expert is None:
        o_ref, wb_ref = rest
    else:
        rs_ref, o_ref, wb_ref = rest

    @pl.when(pl.program_id(1) == 0)
    def _():
        wb_ref[...] = w_ref[...].astype(BF16)

    y = jnp.dot(a_ref[...], wb_ref[...], preferred_element_type=F32)
    gate = _row_mod(tab_ref, pl.program_id(1), tm, tb, lc)
    if expert is not None:
        gate = gate * rs_ref[:, expert:expert + 1]
    o_ref[...] = x_ref[...] + gate * y


def _down_resid(a, w, lead, x, mod, gate_idx, *, tb, lc, tm, tn, rowscale=None, expert=None):
    m, k = a.shape
    d = x.shape[1]
    goff = gate_idx * (d // tn)
    in_specs = [pl.BlockSpec((tm, k), lambda j, i: (i, 0)),
                _lead_spec(lead, (k, tn), lambda j, i: (0, j)),
                pl.BlockSpec((tm, tn), lambda j, i: (i, j)),
                pl.BlockSpec((8, tn), lambda j, i: (0, goff + j))]
    args = [a, w, x, mod]
    if rowscale is not None:
        in_specs.append(pl.BlockSpec((tm, LANES), lambda j, i: (i, 0)))
        args.append(rowscale)
    return pl.pallas_call(
        functools.partial(_down_resid_body, tm=tm, tb=tb, lc=lc, expert=expert),
        grid=(d // tn, m // tm),
        in_specs=in_specs,
        out_specs=pl.BlockSpec((tm, tn), lambda j, i: (i, j)),
        out_shape=jax.ShapeDtypeStruct((m, d), F32),
        scratch_shapes=[pltpu.VMEM((k, tn), BF16)],
        input_output_aliases={2: 0},
        compiler_params=_cp(("arbitrary", "arbitrary")),
    )(*args)


def _merge_body(ys_ref, yr_ref, ws_ref, wr_ref, gs_ref, gr_ref, o_ref, wsb_ref, wrb_ref):
    @pl.when(pl.program_id(1) == 0)
    def _():
        wsb_ref[...] = ws_ref[...].astype(BF16)
        wrb_ref[...] = wr_ref[...].astype(BF16)

    ps = jnp.dot(ys_ref[...], wsb_ref[...], preferred_element_type=F32)
    pr = jnp.dot(yr_ref[...], wrb_ref[...], preferred_element_type=F32)
    o_ref[...] = (_sigmoid(gs_ref[...]) * ps + _sigmoid(gr_ref[...]) * pr).astype(o_ref.dtype)


def _merge(ys, yr, w_ps, w_pr, layer, zg, *, tm, tn):
    m = ys.shape[0]
    d = w_ps.shape[-1]
    nb = d // tn
    return pl.pallas_call(
        _merge_body,
        grid=(nb, m // tm),
        in_specs=[pl.BlockSpec((tm, S5_W), lambda j, i: (i, 0)),
                  pl.BlockSpec((tm, RW), lambda j, i: (i, 0)),
                  _lead_spec((layer,), (S5_W, tn), lambda j, i: (0, j)),
                  _lead_spec((layer,), (RW, tn), lambda j, i: (0, j)),
                  pl.BlockSpec((tm, tn), lambda j, i: (i, j)),
                  pl.BlockSpec((tm, tn), lambda j, i: (i, nb + j))],
        out_specs=pl.BlockSpec((tm, tn), lambda j, i: (i, j)),
        out_shape=jax.ShapeDtypeStruct((m, d), BF16),
        scratch_shapes=[pltpu.VMEM((S5_W, tn), BF16), pltpu.VMEM((RW, tn), BF16)],
        compiler_params=_cp(("arbitrary", "arbitrary")),
    )(ys, yr, w_ps, w_pr, zg, zg)


def _norm_mod_body(x_ref, gain_ref, sh_ref, sc_ref, o_ref, *, tm, tb, lc):
    x = x_ref[...]
    y = x * lax.rsqrt(jnp.mean(x * x, axis=-1, keepdims=True) + NORM_EPS) * gain_ref[...]
    i = pl.program_id(0)
    shift = _row_mod(sh_ref, i, tm, tb, lc)
    scale = _row_mod(sc_ref, i, tm, tb, lc)
    o_ref[...] = (y * (1.0 + scale) + shift).astype(o_ref.dtype)


def _norm_mod(x, gain, layer, mod, shift_idx, *, tb, lc, tm, out_dtype=BF16):
    m, d = x.shape
    return pl.pallas_call(
        functools.partial(_norm_mod_body, tm=tm, tb=tb, lc=lc),
        grid=(m // tm,),
        in_specs=[pl.BlockSpec((tm, d), lambda i: (i, 0)),
                  pl.BlockSpec((None, 1, d), lambda i: (layer, 0, 0)),
                  pl.BlockSpec((8, d), lambda i: (0, shift_idx)),
                  pl.BlockSpec((8, d), lambda i: (0, shift_idx + 1))],
        out_specs=pl.BlockSpec((tm, d), lambda i: (i, 0)),
        out_shape=jax.ShapeDtypeStruct((m, d), out_dtype),
        compiler_params=_cp(("arbitrary",)),
    )(x, gain, mod, mod)


def _final_norm_body(x_ref, gain_ref, o_ref):
    x = x_ref[...]
    o_ref[...] = x * lax.rsqrt(jnp.mean(x * x, axis=-1, keepdims=True) + NORM_EPS) * gain_ref[...]


def _final_norm(x3, gain, *, lc, tm):
    b, tb, d = x3.shape
    l = tb - lc
    skip = lc // tm
    return pl.pallas_call(
        _final_norm_body,
        grid=(b, l // tm),
        in_specs=[pl.BlockSpec((None, tm, d), lambda bi, t: (bi, skip + t, 0)),
                  pl.BlockSpec((1, d), lambda bi, t: (0, 0))],
        out_specs=pl.BlockSpec((None, tm, d), lambda bi, t: (bi, t, 0)),
        out_shape=jax.ShapeDtypeStruct((b, l, d), F32),
        compiler_params=_cp(("arbitrary", "arbitrary")),
    )(x3, gain)


def _router_body(x_ref, gain_ref, sh_ref, sc_ref, w_ref, b_ref, o_ref, *, tm, tb, lc):
    x = x_ref[...]
    y = x * lax.rsqrt(jnp.mean(x * x, axis=-1, keepdims=True) + NORM_EPS) * gain_ref[...]
    i = pl.program_id(0)
    h = y * (1.0 + _row_mod(sc_ref, i, tm, tb, lc)) + _row_mod(sh_ref, i, tm, tb, lc)
    logits = jnp.dot(h, w_ref[...], preferred_element_type=F32,
                     precision=lax.Precision.HIGHEST) + b_ref[...]
    lane = lax.broadcasted_iota(jnp.int32, logits.shape, 1).astype(F32)
    neg = jnp.float32(-jnp.inf)
    logits = jnp.where(lane < N_EXPERTS, logits, neg)
    v1 = jnp.max(logits, axis=-1, keepdims=True)
    i1 = jnp.min(jnp.where(logits == v1, lane, float(LANES)), axis=-1, keepdims=True)
    rest = jnp.where(lane == i1, neg, logits)
    v2 = jnp.max(rest, axis=-1, keepdims=True)
    i2 = jnp.min(jnp.where(rest == v2, lane, float(LANES)), axis=-1, keepdims=True)
    e2 = jnp.exp(v2 - v1)
    w1 = 1.0 / (1.0 + e2)
    w2 = e2 / (1.0 + e2)
    o_ref[...] = (jnp.where(lane == i1, w1, 0.0) + jnp.where(lane == i2, w2, 0.0)
                  + jnp.where(lane == N_EXPERTS, i1, 0.0) + jnp.where(lane == N_EXPERTS + 1, i2, 0.0)
                  + jnp.where(lane == N_EXPERTS + 2, w1, 0.0) + jnp.where(lane == N_EXPERTS + 3, w2, 0.0))


def _router_gates(x, gain, layer, mod, router_pad, bias_pad, *, tb, lc, tm):
    m, d = x.shape
    return pl.pallas_call(
        functools.partial(_router_body, tm=tm, tb=tb, lc=lc),
        grid=(m // tm,),
        in_specs=[pl.BlockSpec((tm, d), lambda i: (i, 0)),
                  pl.BlockSpec((None, 1, d), lambda i: (layer, 0, 0)),
                  pl.BlockSpec((8, d), lambda i: (0, 3)),
                  pl.BlockSpec((8, d), lambda i: (0, 4)),
                  pl.BlockSpec((d, LANES), lambda i: (0, 0)),
                  pl.BlockSpec((1, LANES), lambda i: (0, 0))],
        out_specs=pl.BlockSpec((tm, LANES), lambda i: (i, 0)),
        out_shape=jax.ShapeDtypeStruct((m, LANES), F32),
        compiler_params=_cp(("arbitrary",)),
    )(x, gain, mod, mod, router_pad, bias_pad)


def _s5_tables(lam_re, lam_im, log_step, b_re, b_im, c_re, c_im):
    hp = lax.Precision.HIGHEST
    t = S5_T
    lam_re, lam_im = lam_re.astype(F32), lam_im.astype(F32)
    b_re, b_im = b_re.astype(F32), b_im.astype(F32)
    c_re, c_im = c_re.astype(F32), c_im.astype(F32)
    step = jnp.exp(log_step.astype(F32))[..., None]
    tau = jnp.arange(t + 1, dtype=F32)[None, :, None, None]
    mag = jnp.exp((lam_re * step)[:, None] * tau)
    ang = (lam_im * step)[:, None] * tau
    pw_re, pw_im = mag * jnp.cos(ang), mag * jnp.sin(ang)
    nr, ni = pw_re[:, 1] - 1.0, pw_im[:, 1]
    den = lam_re * lam_re + lam_im * lam_im
    q_re = (nr * lam_re + ni * lam_im) / den
    q_im = (ni * lam_re - nr * lam_im) / den
    bb_re = q_re[..., None] * b_re - q_im[..., None] * b_im
    bb_im = q_re[..., None] * b_im + q_im[..., None] * b_re

    def cmul(ar, ai, br, bi):
        return ar * br - ai * bi, ar * bi + ai * br

    x_re, x_im = cmul(pw_re[:, :t, :, :, None], pw_im[:, :t, :, :, None], bb_re[:, None], bb_im[:, None])
    kd = (jnp.einsum('ghp,dtgpk->dtghk', c_re, x_re, precision=hp)
          - jnp.einsum('ghp,dtgpk->dtghk', c_im, x_im, precision=hp))
    kf, kb = kd[0], kd[1]
    kall = jnp.concatenate([kb[1:][::-1], (kf[0] + kb[0])[None], kf[1:]], axis=0).astype(BF16)
    idx = (jnp.arange(t)[None, :] - jnp.arange(t)[:, None]) + (t - 1)
    kbig = jnp.transpose(kall[idx], (2, 0, 4, 1, 3)).reshape(S5_G, t * S5_H, t * S5_H)

    def chunk_in(d, powers):
        gr, gi = cmul(pw_re[d][powers][..., None], pw_im[d][powers][..., None], bb_re[d][None], bb_im[d][None])
        pack = lambda g: jnp.transpose(g, (1, 0, 3, 2)).reshape(S5_G, t * S5_H, S5_P)
        return pack(gr), pack(gi)

    def chunk_out(d, powers):
        cr, ci = cmul(c_re[None], c_im[None], pw_re[d][powers][:, :, None, :], pw_im[d][powers][:, :, None, :])
        pack = lambda g: jnp.transpose(g, (1, 3, 0, 2)).reshape(S5_G, S5_P, t * S5_H)
        return pack(cr), -pack(ci)

    gin = jnp.stack(chunk_in(0, t - 1 - jnp.arange(t)) + chunk_in(1, jnp.arange(t)), axis=1)
    cout = jnp.stack(chunk_out(0, 1 + jnp.arange(t)) + chunk_out(1, t - jnp.arange(t)), axis=1)
    lam_t = jnp.stack([pw_re[0, t], pw_im[0, t], pw_re[1, t], pw_im[1, t]], axis=1)
    return kbig, gin.astype(BF16), cout.astype(BF16), lam_t[:, :, None, :]


def _s5_body(u_ref, k_ref, gin_ref, cout_ref, lam_ref, o_ref, g_s, h_s, *, nb, ncb, ncc):
    u = u_ref[...]
    for q in range(4):
        g_s[q] = jnp.dot(u, gin_ref[q], preferred_element_type=F32)
    lfr, lfi, lbr, lbi = lam_ref[0], lam_ref[1], lam_ref[2], lam_ref[3]
    nlc = ncb - ncc

    def step(n, carry):
        out = []
        for b in range(nb):
            fr, fi, br, bi = carry[4 * b:4 * b + 4]
            rf = b * ncb + n
            nb_idx = jnp.where(n < ncc, ncc - 1 - n, ncc + (nlc - 1) - (n - ncc))
            rb = b * ncb + nb_idx
            h_s[0, pl.ds(rf, 1), :] = fr
            h_s[1, pl.ds(rf, 1), :] = fi
            h_s[2, pl.ds(rb, 1), :] = br
            h_s[3, pl.ds(rb, 1), :] = bi
            out += [lfr * fr - lfi * fi + g_s[0, pl.ds(rf, 1), :],
                    lfr * fi + lfi * fr + g_s[1, pl.ds(rf, 1), :],
                    lbr * br - lbi * bi + g_s[2, pl.ds(rb, 1), :],
                    lbr * bi + lbi * br + g_s[3, pl.ds(rb, 1), :]]
        return tuple(out)

    zero = jnp.zeros((1, S5_P), F32)
    lax.fori_loop(0, ncb, step, (zero,) * (4 * nb))
    y = jnp.dot(u, k_ref[...], preferred_element_type=F32)
    for q in range(4):
        y = y + jnp.dot(h_s[q].astype(BF16), cout_ref[q], preferred_element_type=F32)
    o_ref[...] = y


def _s5_mix(ug, tables, layer, *, nb, ncb, ncc):
    kbig, gin, cout, lam_t = tables
    g, nc, tw = ug.shape
    return pl.pallas_call(
        functools.partial(_s5_body, nb=nb, ncb=ncb, ncc=ncc),
        grid=(g,),
        in_specs=[pl.BlockSpec((None, nc, tw), lambda i: (i, 0, 0)),
                  pl.BlockSpec((None, None, tw, tw), lambda i: (layer, i, 0, 0)),
                  pl.BlockSpec((None, None, 4, tw, S5_P), lambda i: (layer, i, 0, 0, 0)),
                  pl.BlockSpec((None, None, 4, S5_P, tw), lambda i: (layer, i, 0, 0, 0)),
                  pl.BlockSpec((None, None, 4, 1, S5_P), lambda i: (layer, i, 0, 0, 0))],
        out_specs=pl.BlockSpec((None, nc, tw), lambda i: (i, 0, 0)),
        out_shape=jax.ShapeDtypeStruct((g, nc, tw), F32),
        scratch_shapes=[pltpu.VMEM((4, nc, S5_P), F32), pltpu.VMEM((4, nc, S5_P), F32)],
        compiler_params=_cp(("arbitrary",)),
    )(ug, kbig, gin, cout, lam_t)


def _s5_readout_body(y_ref, u_ref, d_ref, w_ref, o_ref, wb_ref):
    @pl.when(pl.program_id(0) == 0)
    def _():
        wb_ref[...] = w_ref[...].astype(BF16)

    y = _gelu_tanh(y_ref[...] + d_ref[...] * u_ref[...])
    z = jnp.dot(y.astype(BF16), wb_ref[...], preferred_element_type=F32)
    o_ref[...] = (y * _sigmoid(z)).astype(o_ref.dtype)


def _s5_readout(y, u, d_skip, w_glu, layer, *, tm):
    m = y.shape[0]
    return pl.pallas_call(
        _s5_readout_body,
        grid=(m // tm,),
        in_specs=[pl.BlockSpec((tm, S5_W), lambda i: (i, 0)),
                  pl.BlockSpec((tm, S5_W), lambda i: (i, 0)),
                  pl.BlockSpec((None, 1, S5_W), lambda i: (layer, 0, 0)),
                  pl.BlockSpec((None, S5_W, S5_W), lambda i: (layer, 0, 0))],
        out_specs=pl.BlockSpec((tm, S5_W), lambda i: (i, 0)),
        out_shape=jax.ShapeDtypeStruct((m, S5_W), BF16),
        scratch_shapes=[pltpu.VMEM((S5_W, S5_W), BF16)],
        compiler_params=_cp(("arbitrary",)),
    )(y, u, d_skip, w_glu)


def _rw_feat_body(z_ref, up_ref, dn_ref, mu_ref, wup_ref, aup_ref, gup_ref, vec_ref, ones_ref, tri_ref,
                  v_ref, g_ref, bonus_ref, ah_ref, rh_ref, bh_ref, kh_ref, bt_ref, kt_ref, pt_ref,
                  *, tt, tb, lc):
    i = pl.program_id(0)
    start = i * tt
    b = start // tb
    pos = (start - b * tb) + lax.broadcasted_iota(jnp.int32, (tt, 1), 0)
    is_ctx = pos < lc
    tl = pos - lc
    col = tl & (GRID_W - 1)
    l_lat = tb - lc

    z = z_ref[...]
    prev = pltpu.roll(z, 1, axis=0)
    nxt = pltpu.roll(z, tt - 1, axis=0)
    if tt > GRID_W:
        up = jnp.concatenate([up_ref[...], z[:tt - GRID_W]], axis=0)
        down = jnp.concatenate([z[GRID_W:], dn_ref[...]], axis=0)
    else:
        up, down = up_ref[...], dn_ref[...]
    m_prev = jnp.logical_or(jnp.logical_and(is_ctx, pos >= 1), jnp.logical_and(tl >= 0, col >= 1))
    m_next = jnp.logical_or(jnp.logical_and(is_ctx, pos <= lc - 2),
                            jnp.logical_and(tl >= 0, col <= GRID_W - 2))
    m_up = tl >= GRID_W
    m_down = jnp.logical_and(tl >= 0, tl < l_lat - GRID_W)
    prev = jnp.where(m_prev, prev, 0.0)
    nxt = jnp.where(m_next, nxt, 0.0)
    up = jnp.where(is_ctx, prev, jnp.where(m_up, up, 0.0))
    down = jnp.where(is_ctx, nxt, jnp.where(m_down, down, 0.0))
    l4 = lax.broadcasted_iota(jnp.int32, (1, ZR_COLS), 1) & 3
    shifted = jnp.where(l4 == 0, prev, jnp.where(l4 == 1, nxt, jnp.where(l4 == 2, up, down)))
    z = z + mu_ref[...] * (shifted - z)

    r = z[:, 0:RW]
    k = z[:, RW:2 * RW]
    v = z[:, 2 * RW:3 * RW]
    lora_w = jnp.tanh(z[:, 3 * RW:3 * RW + LANES]).astype(BF16)
    lora_a = z[:, 3 * RW + LANES:3 * RW + 2 * LANES].astype(BF16)
    lora_g = _sigmoid(z[:, 3 * RW + 2 * LANES:3 * RW + 4 * LANES]).astype(BF16)
    k_k, k_a, r_k = vec_ref[0:1, :], vec_ref[1:2, :], vec_ref[2:3, :]
    ones = ones_ref[...]

    kk = k * k_k
    kk = kk * lax.rsqrt(jnp.maximum(_split_dot(kk * kk, ones), 1e-24))
    ksum = jnp.zeros_like(k)
    for d in range(2):
        w0 = vec_ref[3 + d:4 + d, :]
        a0 = vec_ref[5 + d:6 + d, :]
        wl = w0 + jnp.dot(lora_w, wup_ref[d], preferred_element_type=F32)
        lw = -math.exp(-0.5) * _sigmoid(wl)
        lw2 = lw * LOG2_E
        a = _sigmoid(a0 + jnp.dot(lora_a, aup_ref[d], preferred_element_type=F32))
        k_d = k * (1.0 + (a - 1.0) * k_a)
        bb = kk * a
        ksum = ksum + k_d
        lw_hi = lw2.astype(BF16)
        lw_lo = (lw2 - lw_hi.astype(F32)).astype(BF16)
        cl = (jnp.dot(tri_ref[d, 0], lw_hi, preferred_element_type=F32)
              + jnp.dot(tri_ref[d, 0], lw_lo, preferred_element_type=F32))
        rem = (jnp.dot(tri_ref[d, 1], lw_hi, preferred_element_type=F32)
               + jnp.dot(tri_ref[d, 1], lw_lo, preferred_element_type=F32))
        p_in = jnp.exp2(cl)
        p_inv = jnp.exp2(-cl)
        p_rem = jnp.exp2(rem)
        ah_ref[d] = (-kk * jnp.exp2(cl - lw2)).astype(BF16)
        rh_ref[d] = (r * p_in).astype(BF16)
        bh_ref[d] = (bb * p_inv).astype(BF16)
        kh_ref[d] = (k_d * p_inv).astype(BF16)
        bt_ref[d] = (bb * p_rem).astype(BF16)
        kt_ref[d] = (k_d * p_rem).astype(BF16)
        last = 0 if d == 1 else WKV_CHUNK - 1
        for cidx in range(tt // WKV_CHUNK):
            row = cidx * WKV_CHUNK + last
            pt_ref[d, cidx] = p_in[row:row + 1, :]
    v_ref[...] = v.astype(BF16)
    g_ref[...] = jnp.dot(lora_g, gup_ref[...], preferred_element_type=F32)
    bonus_ref[...] = _split_dot(r * ksum * r_k, ones) * v


def _rw_features(zr, layer, mu_pad, wup, aup, gup, vecs, ones_blk, tri, *, tb, lc, tt):
    m = zr.shape[0]
    nh = tt // GRID_W
    nblk64 = m // GRID_W
    ncht = tt // WKV_CHUNK
    full = pl.BlockSpec((tt, RW), lambda i: (i, 0))
    per_dir = pl.BlockSpec((2, tt, RW), lambda i: (0, i, 0))
    one = jax.ShapeDtypeStruct((m, RW), F32)
    two = jax.ShapeDtypeStruct((2, m, RW), BF16)
    return pl.pallas_call(
        functools.partial(_rw_feat_body, tt=tt, tb=tb, lc=lc),
        grid=(m // tt,),
        in_specs=[pl.BlockSpec((tt, ZR_COLS), lambda i: (i, 0)),
                  pl.BlockSpec((GRID_W, ZR_COLS), lambda i: (jnp.maximum(i * nh - 1, 0), 0)),
                  pl.BlockSpec((GRID_W, ZR_COLS), lambda i: (jnp.minimum((i + 1) * nh, nblk64 - 1), 0)),
                  pl.BlockSpec((None, 1, ZR_COLS), lambda i: (layer, 0, 0)),
                  pl.BlockSpec((None, 2, LANES, RW), lambda i: (layer, 0, 0, 0)),
                  pl.BlockSpec((None, 2, LANES, RW), lambda i: (layer, 0, 0, 0)),
                  pl.BlockSpec((None, 2 * LANES, RW), lambda i: (layer, 0, 0)),
                  pl.BlockSpec((None, 8, RW), lambda i: (layer, 0, 0)),
                  pl.BlockSpec((RW, RW), lambda i: (0, 0)),
                  pl.BlockSpec((2, 2, tt, tt), lambda i: (0, 0, 0, 0))],
        out_specs=[full, full, full, per_dir, per_dir, per_dir, per_dir, per_dir, per_dir,
                   pl.BlockSpec((2, ncht, 1, RW), lambda i: (0, i, 0, 0))],
        out_shape=[jax.ShapeDtypeStruct((m, RW), BF16), one, one, two, two, two, two, two, two,
                   jax.ShapeDtypeStruct((2, m // WKV_CHUNK, 1, RW), F32)],
        compiler_params=_cp(("arbitrary",)),
    )(zr, zr, zr, mu_pad, wup, aup, gup, vecs, ones_blk, tri)


def _chunk_tri(tt):
    t = jnp.arange(tt)[:, None]
    s = jnp.arange(tt)[None, :]
    same = (t // WKV_CHUNK) == (s // WKV_CHUNK)
    fwd = jnp.stack([same & (s <= t), same & (s > t)])
    bwd = jnp.stack([same & (s >= t), same & (s < t)])
    return jnp.stack([fwd, bwd]).astype(BF16)


def _dot_nt(a, b):
    return lax.dot_general(a, b, (((1,), (1,)), ((), ())), preferred_element_type=F32)


def _dot_tn(a, b):
    return lax.dot_general(a, b, (((0,), (0,)), ((), ())), preferred_element_type=F32)


def _wkv_body(*refs):
    c = WKV_CHUNK
    ins = (refs[0:8], refs[8:16])
    y_refs = refs[16:18]
    s_ref = refs[18]

    @pl.when(pl.program_id(1) == 0)
    def _():
        s_ref[...] = jnp.zeros_like(s_ref)

    t_idx = lax.broadcasted_iota(jnp.int32, (2 * c, LANES), 0)
    lane = lax.broadcasted_iota(jnp.int32, (2 * c, LANES), 1)
    diff = (t_idx & (c - 1)) - (lane & (c - 1))
    tri = (jnp.where(t_idx < c, diff, diff + 1) > 0, jnp.where(t_idx < c, -diff, 1 - diff) > 0)
    same_head = (t_idx // c) == (lane // c)
    lo = lax.broadcasted_iota(jnp.int32, (c, LANES), 1) < c

    def bd(x):
        zero = jnp.zeros_like(x)
        return jnp.concatenate([jnp.where(lo, x, zero), jnp.where(lo, zero, x)], axis=0)

    chains = [(d, p) for d in range(2) for p in range(HEADS // 2)]
    n = range(len(chains))
    sl = [slice(p * LANES, (p + 1) * LANES) for _, p in chains]
    ah, rh, bh, kh, bt, kt, vv, pt = ([ins[d][q] for d, _ in chains] for q in range(8))
    ar = [jnp.concatenate([ah[i][:, sl[i]], rh[i][:, sl[i]]], axis=0) for i in n]
    s_old = [s_ref[d, p] for d, p in chains]
    pbk = [_dot_nt(ar[i], jnp.concatenate([bd(bh[i][:, sl[i]]), bd(kh[i][:, sl[i]])], axis=0)) for i in n]
    ps = [_dot_nt(ar[i], s_old[i].astype(BF16)) for i in n]
    pb = [jnp.where(tri[chains[i][0]], pbk[i][:, :LANES], 0.0) for i in n]
    pk = [jnp.where(tri[chains[i][0]], pbk[i][:, LANES:], 0.0) for i in n]
    bdv = [bd(vv[i][:, sl[i]]) for i in n]
    x = [ps[i][:c] + jnp.dot(pk[i][:c].astype(BF16), bdv[i], preferred_element_type=F32) for i in n]
    lp = [pb[i][:c].astype(BF16) for i in n]
    for it in range(6):
        if it < 5:
            prod = [jnp.dot(lp[i], jnp.concatenate([bd(x[i].astype(BF16)), bd(lp[i])], axis=1),
                            preferred_element_type=F32) for i in n]
            x = [x[i] + prod[i][:, :LANES] for i in n]
            lp = [prod[i][:, LANES:].astype(BF16) for i in n]
        else:
            x = [x[i] + jnp.dot(lp[i], bd(x[i].astype(BF16)), preferred_element_type=F32) for i in n]
    u = [x[i].astype(BF16) for i in n]
    for i in n:
        y_refs[chains[i][0]][:, sl[i]] = ps[i][c:] + jnp.dot(
            jnp.concatenate([pb[i][c:], pk[i][c:]], axis=1).astype(BF16),
            jnp.concatenate([bd(u[i]), bdv[i]], axis=0), preferred_element_type=F32)
    upd = [_dot_tn(jnp.concatenate([u[i], vv[i][:, sl[i]]], axis=0),
                   jnp.concatenate([bt[i][:, sl[i]], kt[i][:, sl[i]]], axis=0)) for i in n]
    for i, (d, p) in enumerate(chains):
        s_ref[d, p] = s_old[i] * pt[i][:, sl[i]] + jnp.where(same_head, upd[i], 0.0)


def _wkv_scan(ah, rh, bh, kh, bt, kt, v, pt, *, nb, tb, lc):
    m = v.shape[0]
    c = WKV_CHUNK
    ncb, ncc = tb // c, lc // c
    nlc = ncb - ncc

    def chunk(d, b, n):
        if d == 0:
            return b * ncb + n
        return b * ncb + jnp.where(n < ncc, ncc - 1 - n, ncc + (nlc - 1) - (n - ncc))

    def specs(d):
        per_dir = pl.BlockSpec((None, c, RW), lambda b, n: (d, chunk(d, b, n), 0))
        return [per_dir] * 6 + [pl.BlockSpec((c, RW), lambda b, n: (chunk(d, b, n), 0)),
                                pl.BlockSpec((None, None, 1, RW), lambda b, n: (d, chunk(d, b, n), 0, 0))]

    out = jax.ShapeDtypeStruct((m, RW), F32)
    args = (ah, rh, bh, kh, bt, kt, v, pt)
    return pl.pallas_call(
        _wkv_body,
        grid=(nb, ncb),
        in_specs=specs(0) + specs(1),
        out_specs=[pl.BlockSpec((c, RW), lambda b, n: (chunk(0, b, n), 0)),
                   pl.BlockSpec((c, RW), lambda b, n: (chunk(1, b, n), 0))],
        out_shape=[out, out],
        scratch_shapes=[pltpu.VMEM((2, HEADS // 2, LANES, LANES), F32)],
        compiler_params=_cp(("arbitrary", "arbitrary")),
    )(*args, *args)


def _rw_readout_body(yf_ref, yb_ref, bonus_ref, g_ref, gn_ref, ones_ref, o_ref):
    y = yf_ref[...] + yb_ref[...]
    ones = ones_ref[...]
    mean = _split_dot(y, ones) * (1.0 / HEAD)
    yc = y - mean
    var = _split_dot(yc * yc, ones) * (1.0 / HEAD)
    yn = yc * lax.rsqrt(var + GN_EPS) * gn_ref[0:1, :] + gn_ref[1:2, :]
    o_ref[...] = ((yn + bonus_ref[...]) * g_ref[...]).astype(o_ref.dtype)


def _rw_readout(y_f, y_b, bonus, g, gn, layer, ones_blk, *, tm):
    m = y_f.shape[0]
    full = pl.BlockSpec((tm, RW), lambda i: (i, 0))
    return pl.pallas_call(
        _rw_readout_body,
        grid=(m // tm,),
        in_specs=[full, full, full, full,
                  pl.BlockSpec((None, 8, RW), lambda i: (layer, 0, 0)),
                  pl.BlockSpec((RW, RW), lambda i: (0, 0))],
        out_specs=full,
        out_shape=jax.ShapeDtypeStruct((m, RW), BF16),
        compiler_params=_cp(("arbitrary",)),
    )(y_f, y_b, bonus, g, gn, ones_blk)


MOE_TILE = 1024


def _moe_plan(route, tme):
    m = route.shape[0]
    n_asg = 2 * m
    idx = route[:, N_EXPERTS:N_EXPERTS + 2].astype(jnp.int32)
    e_flat = idx.T.reshape(n_asg)
    w_flat = route[:, N_EXPERTS + 2:N_EXPERTS + 4].T.reshape(n_asg)
    tok_flat = jnp.tile(jnp.arange(m, dtype=jnp.int32), 2)
    order = jnp.argsort(e_flat, stable=True).astype(jnp.int32)
    e_sorted, tok_sorted, w_sorted = e_flat[order], tok_flat[order], w_flat[order]
    experts = jnp.arange(N_EXPERTS, dtype=jnp.int32)
    cnt = jnp.sum((e_flat[:, None] == experts[None, :]).astype(jnp.int32), axis=0)
    start = jnp.cumsum(cnt) - cnt
    padded = ((cnt + tme - 1) // tme) * tme
    off_end = jnp.cumsum(padded)
    off = off_end - padded
    n_rows = ((n_asg + tme - 1) // tme) * tme + N_EXPERTS * tme
    n_tiles = n_rows // tme
    tile_e = jnp.sum((jnp.arange(n_tiles, dtype=jnp.int32)[:, None] * tme >= off_end[None, :]).astype(jnp.int32),
                     axis=1)
    tile_e = jnp.minimum(tile_e, N_EXPERTS - 1)
    first = jnp.concatenate([jnp.ones((1,), jnp.int32), (tile_e[1:] != tile_e[:-1]).astype(jnp.int32)])
    n_act = (off_end[-1] // tme).reshape(1)
    e_row = jnp.repeat(tile_e, tme)
    q = jnp.arange(n_rows, dtype=jnp.int32) - off[e_row]
    valid = q < cnt[e_row]
    s_row = jnp.clip(start[e_row] + q, 0, n_asg - 1)
    src_tok = jnp.where(valid, tok_sorted[s_row], 0)
    row_w = jnp.where(valid, w_sorted[s_row], 0.0)[:, None]
    dest_sorted = off[e_sorted] + (jnp.arange(n_asg, dtype=jnp.int32) - start[e_sorted])
    pos = jnp.zeros((n_asg,), jnp.int32).at[order].set(dest_sorted, unique_indices=True)
    return dict(src_tok=src_tok, row_w=row_w, pos=pos, tile_e=tile_e, first=first, n_act=n_act,
                n_rows=n_rows, n_tiles=n_tiles)


def _gather_body(idx_ref, nact_ref, src_ref, o_ref, sem, *, rows):
    i = pl.program_id(0)
    base = i * rows

    @pl.when(i < nact_ref[0])
    def _():
        def issue(r, carry):
            pltpu.make_async_copy(src_ref.at[idx_ref[base + r]], o_ref.at[r], sem).start()
            return carry

        lax.fori_loop(0, rows, issue, 0, unroll=8)
        pltpu.make_async_copy(o_ref, o_ref, sem).wait()

    @pl.when(i >= nact_ref[0])
    def _():
        o_ref[...] = jnp.zeros_like(o_ref)


def _gather_rows(src3, idx, n_act, *, rows):
    n_rows = idx.shape[0]
    blk = (rows,) + src3.shape[1:]
    return pl.pallas_call(
        functools.partial(_gather_body, rows=rows),
        grid_spec=pltpu.PrefetchScalarGridSpec(
            num_scalar_prefetch=2, grid=(n_rows // rows,),
            in_specs=[pl.BlockSpec(memory_space=pl.ANY)],
            out_specs=pl.BlockSpec(blk, lambda i, idx, na: (i, 0, 0)),
            scratch_shapes=[pltpu.SemaphoreType.DMA(())]),
        out_shape=jax.ShapeDtypeStruct((n_rows,) + src3.shape[1:], src3.dtype),
        compiler_params=_cp(("arbitrary",)),
    )(idx, n_act, src3)


def _moe_up_body(te_ref, first_ref, nact_ref, x_ref, wg_ref, wu_ref, o_ref, wgb_ref, wub_ref):
    i = pl.program_id(1)

    @pl.when(i < nact_ref[0])
    def _():
        @pl.when(first_ref[i] == 1)
        def _():
            wgb_ref[...] = wg_ref[...].astype(BF16)
            wub_ref[...] = wu_ref[...].astype(BF16)

        x = x_ref[...]
        g = jnp.dot(x, wgb_ref[...], preferred_element_type=F32)
        u = jnp.dot(x, wub_ref[...], preferred_element_type=F32)
        o_ref[...] = (g * _sigmoid(g) * u).astype(o_ref.dtype)

    @pl.when(i >= nact_ref[0])
    def _():
        o_ref[...] = jnp.zeros_like(o_ref)


def _moe_up(xg, wg, wu, layer, plan, *, tme, tn):
    p, k = xg.shape
    f = wg.shape[-1]

    def row(i, nact):
        return jnp.minimum(i, nact[0] - 1)

    wspec = pl.BlockSpec((None, None, k, tn), lambda j, i, te, fi, na: (layer, te[row(i, na)], 0, j))
    return pl.pallas_call(
        _moe_up_body,
        grid_spec=pltpu.PrefetchScalarGridSpec(
            num_scalar_prefetch=3, grid=(f // tn, p // tme),
            in_specs=[pl.BlockSpec((tme, k), lambda j, i, te, fi, na: (row(i, na), 0)), wspec, wspec],
            out_specs=pl.BlockSpec((tme, tn), lambda j, i, te, fi, na: (i, j)),
            scratch_shapes=[pltpu.VMEM((k, tn), BF16), pltpu.VMEM((k, tn), BF16)]),
        out_shape=jax.ShapeDtypeStruct((p, f), BF16),
        compiler_params=_cp(("arbitrary", "arbitrary")),
    )(plan["tile_e"], plan["first"], plan["n_act"], xg, wg, wu)


def _moe_down_body(te_ref, first_ref, nact_ref, a_ref, w_ref, rw_ref, o_ref, wb_ref):
    i = pl.program_id(1)

    @pl.when(i < nact_ref[0])
    def _():
        @pl.when(first_ref[i] == 1)
        def _():
            wb_ref[...] = w_ref[...].astype(BF16)

        o_ref[...] = rw_ref[...] * jnp.dot(a_ref[...], wb_ref[...], preferred_element_type=F32)

    @pl.when(i >= nact_ref[0])
    def _():
        o_ref[...] = jnp.zeros_like(o_ref)


def _moe_down(act, wd, layer, plan, *, tme, tn):
    p, k = act.shape
    d = wd.shape[-1]

    def row(i, nact):
        return jnp.minimum(i, nact[0] - 1)

    return pl.pallas_call(
        _moe_down_body,
        grid_spec=pltpu.PrefetchScalarGridSpec(
            num_scalar_prefetch=3, grid=(d // tn, p // tme),
            in_specs=[pl.BlockSpec((tme, k), lambda j, i, te, fi, na: (row(i, na), 0)),
                      pl.BlockSpec((None, None, k, tn), lambda j, i, te, fi, na: (layer, te[row(i, na)], 0, j)),
                      pl.BlockSpec((tme, 1), lambda j, i, te, fi, na: (row(i, na), 0))],
            out_specs=pl.BlockSpec((tme, tn), lambda j, i, te, fi, na: (i, j)),
            scratch_shapes=[pltpu.VMEM((k, tn), BF16)]),
        out_shape=jax.ShapeDtypeStruct((p, d), F32),
        compiler_params=_cp(("arbitrary", "arbitrary")),
    )(plan["tile_e"], plan["first"], plan["n_act"], act, wd, plan["row_w"])


def _moe_combine_body(pos_ref, x_ref, y_ref, tab_ref, o_ref, buf, sem, *, tc, m, tb, lc):
    i = pl.program_id(0)
    base = i * tc

    def issue(r, carry):
        pltpu.make_async_copy(y_ref.at[pos_ref[base + r]], buf.at[0, r], sem).start()
        pltpu.make_async_copy(y_ref.at[pos_ref[m + base + r]], buf.at[1, r], sem).start()
        return carry

    lax.fori_loop(0, tc, issue, 0)
    pltpu.make_async_copy(buf, buf, sem).wait()
    gate = _row_mod(tab_ref, i, tc, tb, lc)
    o_ref[...] = x_ref[...] + gate * (buf[0] + buf[1])


def _moe_combine(x, y, pos, mod, gate_idx, *, tb, lc, tc):
    m, d = x.shape
    return pl.pallas_call(
        functools.partial(_moe_combine_body, tc=tc, m=m, tb=tb, lc=lc),
        grid_spec=pltpu.PrefetchScalarGridSpec(
            num_scalar_prefetch=1, grid=(m // tc,),
            in_specs=[pl.BlockSpec((tc, d), lambda i, pos: (i, 0)),
                      pl.BlockSpec(memory_space=pl.ANY),
                      pl.BlockSpec((8, d), lambda i, pos: (0, gate_idx))],
            out_specs=pl.BlockSpec((tc, d), lambda i, pos: (i, 0)),
            scratch_shapes=[pltpu.VMEM((2, tc, d), F32), pltpu.SemaphoreType.DMA(())]),
        out_shape=jax.ShapeDtypeStruct((m, d), F32),
        input_output_aliases={1: 0},
        compiler_params=_cp(("arbitrary",)),
    )(pos, x, y, mod)


def _pad_rows(x, rows):
    return jnp.concatenate([x, jnp.zeros((rows - x.shape[0],) + x.shape[1:], x.dtype)], axis=0)


def _pick_tile(n, target):
    best = 8
    for t in range(8, target + 1, 8):
        if n % t == 0:
            best = t
    return best


def kernel(x, c, ctx, c_ctx, w_ada, b_ada, norm1, norm2, norm_f, w_in, s5_lambda_re, s5_lambda_im, s5_log_step, s5_b_re, s5_b_im, s5_c_re, s5_c_im, s5_d, s5_w_glu, rw_mu, rw_w0, rw_w_up, rw_a0, rw_a_up, rw_g_up, rw_k_k, rw_k_a, rw_r_k, rw_gn_w, rw_gn_b, w_proj_s5, w_proj_rw, w_out, ffn_w_gate, ffn_w_up, ffn_w_down, moe_router, moe_router_bias, moe_w_gate, moe_w_up, moe_w_down):
    nb, l, d = x.shape
    lc = ctx.shape[1]
    tb = lc + l
    m = nb * tb
    depth = w_in.shape[0]
    tt = lc
    assert lc % GRID_W == 0 and l % tt == 0 and l % GRID_W == 0 and nb * HEADS * 4 == LANES
    assert tb % S5_T == 0 and lc % S5_T == 0 and tt % WKV_CHUNK == 0
    tm = _pick_tile(tb, 1088)
    tr = _pick_tile(tb, 544)
    ncb, ncc = tb // S5_T, lc // S5_T

    xs = jnp.concatenate([ctx, x], axis=1).reshape(m, d)
    silu_rows = _pad_rows(jnp.concatenate([c_ctx[None], c], axis=0), 8)
    silu_rows = (silu_rows * _sigmoid(silu_rows)).astype(BF16)

    ones_blk = (jnp.arange(RW)[:, None] // HEAD == jnp.arange(RW)[None, :] // HEAD).astype(BF16)
    tri = _chunk_tri(tt)
    mu_pad = jnp.pad(rw_mu, ((0, 0), (0, ZR_COLS - rw_mu.shape[1])))[:, None, :]
    zeros64 = jnp.zeros((depth, 1, DECAY_LORA, RW), F32)
    wup = jnp.concatenate([jnp.concatenate([rw_w_up[:, :1], zeros64], axis=2),
                           jnp.concatenate([zeros64, rw_w_up[:, 1:]], axis=2)], axis=1).astype(BF16)
    aup = jnp.concatenate([jnp.concatenate([rw_a_up[:, :1], zeros64], axis=2),
                           jnp.concatenate([zeros64, rw_a_up[:, 1:]], axis=2)], axis=1).astype(BF16)
    gup = jnp.pad(rw_g_up, ((0, 0), (0, 2 * LANES - GATE_LORA), (0, 0))).astype(BF16)
    vecs = jnp.stack([rw_k_k, rw_k_a, rw_r_k.reshape(depth, RW), rw_w0[:, 0], rw_w0[:, 1],
                      rw_a0[:, 0], rw_a0[:, 1], jnp.zeros_like(rw_k_k)], axis=1)
    gn = jnp.pad(jnp.stack([rw_gn_w, rw_gn_b], axis=1), ((0, 0), (0, 6), (0, 0)))
    d_skip = s5_d.reshape(depth, 1, S5_W)
    norm1, norm2 = norm1[:, None, :], norm2[:, None, :]
    w_gates = w_in[:, :, S5_W + 3 * RW + LORA_COLS:]
    tables = jax.vmap(_s5_tables)(s5_lambda_re, s5_lambda_im, s5_log_step, s5_b_re, s5_b_im, s5_c_re, s5_c_im)

    for i in range(depth):
        mod = _matmul(silu_rows, w_ada, (i,), 6 * d, tm=8, tn=1024, bias=b_ada[i][None])

        h = _norm_mod(xs, norm1, i, mod, 0, tb=tb, lc=lc, tm=tr)
        u = _matmul(h, w_in, (i,), S5_W, tm=tm, tn=512)
        zr = _matmul(h, w_in, (i,), ZR_COLS, col_off=S5_W, tm=tm, tn=512)
        zg = _matmul(h, w_gates, (i,), 2 * d, tm=tm, tn=1024)

        ug = jnp.transpose(u.reshape(nb * ncb, S5_T, S5_G, S5_H), (2, 0, 1, 3))
        ug = ug.reshape(S5_G, nb * ncb, S5_T * S5_H).astype(BF16)
        yg = _s5_mix(ug, tables, i, nb=nb, ncb=ncb, ncc=ncc)
        ys = jnp.transpose(yg.reshape(S5_G, nb * ncb, S5_T, S5_H), (1, 2, 0, 3)).reshape(m, S5_W)
        y_s5 = _s5_readout(ys, u, d_skip, s5_w_glu, i, tm=tt)

        v, g, bonus, ah, rh, bh, kh, bt, kt, pt = _rw_features(zr, i, mu_pad, wup, aup, gup, vecs, ones_blk, tri,
                                                               tb=tb, lc=lc, tt=tt)
        y_f, y_b = _wkv_scan(ah, rh, bh, kh, bt, kt, v, pt, nb=nb, tb=tb, lc=lc)
        y_rw = _rw_readout(y_f, y_b, bonus, g, gn, i, ones_blk, tm=tr)

        mrg = _merge(y_s5, y_rw, w_proj_s5, w_proj_rw, i, zg, tm=tm, tn=512)
        xs = _down_resid(mrg, w_out, (i,), xs, mod, 2, tb=tb, lc=lc, tm=tm, tn=512)

        j = i // 2
        h2 = _norm_mod(xs, norm2, i, mod, 3, tb=tb, lc=lc, tm=tr)
        if i % 2 == 0:
            act = _swiglu_up(h2, ffn_w_gate, ffn_w_up, (j,), tm=tm, tn=512)
            xs = _down_resid(act, ffn_w_down, (j,), xs, mod, 5, tb=tb, lc=lc, tm=tr, tn=512)
        else:
            router_pad = jnp.pad(moe_router[j], ((0, 0), (0, LANES - N_EXPERTS)))
            bias_pad = jnp.pad(moe_router_bias[j], (0, LANES - N_EXPERTS))[None]
            route = _router_gates(xs, norm2, i, mod, router_pad, bias_pad, tb=tb, lc=lc, tm=tt)
            plan = _moe_plan(route, MOE_TILE)
            xg = _gather_rows(h2.reshape(m, d // LANES, LANES), plan["src_tok"], plan["n_act"], rows=MOE_TILE)
            act = _moe_up(xg.reshape(plan["n_rows"], d), moe_w_gate, moe_w_up, j, plan, tme=MOE_TILE, tn=256)
            y_exp = _moe_down(act, moe_w_down, j, plan, tme=MOE_TILE, tn=512)
            xs = _moe_combine(xs, y_exp, plan["pos"], mod, 5, tb=tb, lc=lc, tc=tt)

    return _final_norm(xs.reshape(nb, tb, d), norm_f[None], lc=lc, tm=tt)
```

```python
import functools
import math

import jax
import jax.numpy as jnp
from jax import lax
from jax.experimental import pallas as pl
from jax.experimental.pallas import tpu as pltpu

F32 = jnp.float32
BF16 = jnp.bfloat16

GRID_W = 64
HEADS = 16
HEAD = 64
RW = HEADS * HEAD
S5_G = 32
S5_H = 16
S5_P = 64
S5_W = S5_G * S5_H
S5_T = 16
WKV_CHUNK = 64
DECAY_LORA = 64
ICLR_LORA = 64
GATE_LORA = 160
LORA_COLS = 2 * DECAY_LORA + 2 * ICLR_LORA + GATE_LORA
LORA_PAD = 512
ZR_COLS = 3 * RW + LORA_PAD
NORM_EPS = 1e-6
GN_EPS = 64e-5
N_EXPERTS = 8
LANES = 128
LOG2_E = 1.4426950408889634
VMEM_LIMIT = 56 * 1024 * 1024


def _cp(sem):
    return pltpu.CompilerParams(dimension_semantics=sem, vmem_limit_bytes=VMEM_LIMIT)


def _sigmoid(x):
    return 0.5 * jnp.tanh(0.5 * x) + 0.5


def _gelu_tanh(x):
    return 0.5 * x * (1.0 + jnp.tanh(math.sqrt(2.0 / math.pi) * (x + 0.044715 * (x * x * x))))


def _row_mod(tab_ref, tile, tm, tb, lc):
    start = tile * tm
    b = start // tb
    pos = (start - b * tb) + lax.broadcasted_iota(jnp.int32, (tm, 1), 0)
    ctx_row = tab_ref[0:1, :]
    lat_row = tab_ref[pl.ds(1 + b, 1), :]
    return jnp.where(pos < lc, ctx_row, lat_row)


def _split_dot(x, w_bf16):
    hi = x.astype(BF16)
    lo = (x - hi.astype(F32)).astype(BF16)
    return (jnp.dot(hi, w_bf16, preferred_element_type=F32)
            + jnp.dot(lo, w_bf16, preferred_element_type=F32))


def _lead_spec(lead, shape, fn):
    return pl.BlockSpec((None,) * len(lead) + shape, lambda j, i: tuple(lead) + fn(j, i))


def _mm_body(x_ref, w_ref, *rest, has_bias):
    if has_bias:
        b_ref, o_ref, wb_ref = rest
    else:
        o_ref, wb_ref = rest

    @pl.when(pl.program_id(1) == 0)
    def _():
        wb_ref[...] = w_ref[...].astype(BF16)

    acc = jnp.dot(x_ref[...], wb_ref[...], preferred_element_type=F32)
    if has_bias:
        acc = acc + b_ref[...]
    o_ref[...] = acc.astype(o_ref.dtype)


def _matmul(x, w, lead, n_out, *, col_off=0, tm, tn, out_dtype=F32, bias=None):
    m, k = x.shape
    off = col_off // tn
    in_specs = [pl.BlockSpec((tm, k), lambda j, i: (i, 0)),
                _lead_spec(lead, (k, tn), lambda j, i: (0, j + off))]
    args = [x, w]
    if bias is not None:
        in_specs.append(pl.BlockSpec((1, tn), lambda j, i: (0, j)))
        args.append(bias)
    return pl.pallas_call(
        functools.partial(_mm_body, has_bias=bias is not None),
        grid=(n_out // tn, m // tm),
        in_specs=in_specs,
        out_specs=pl.BlockSpec((tm, tn), lambda j, i: (i, j)),
        out_shape=jax.ShapeDtypeStruct((m, n_out), out_dtype),
        scratch_shapes=[pltpu.VMEM((k, tn), BF16)],
        compiler_params=_cp(("arbitrary", "arbitrary")),
    )(*args)


def _swiglu_up_body(x_ref, wg_ref, wu_ref, o_ref, wgb_ref, wub_ref):
    @pl.when(pl.program_id(1) == 0)
    def _():
        wgb_ref[...] = wg_ref[...].astype(BF16)
        wub_ref[...] = wu_ref[...].astype(BF16)

    x = x_ref[...]
    g = jnp.dot(x, wgb_ref[...], preferred_element_type=F32)
    u = jnp.dot(x, wub_ref[...], preferred_element_type=F32)
    o_ref[...] = (g * _sigmoid(g) * u).astype(o_ref.dtype)


def _swiglu_up(x, wg, wu, lead, *, tm, tn):
    m, k = x.shape
    f = wg.shape[-1]
    wspec = _lead_spec(lead, (k, tn), lambda j, i: (0, j))
    return pl.pallas_call(
        _swiglu_up_body,
        grid=(f // tn, m // tm),
        in_specs=[pl.BlockSpec((tm, k), lambda j, i: (i, 0)), wspec, wspec],
        out_specs=pl.BlockSpec((tm, tn), lambda j, i: (i, j)),
        out_shape=jax.ShapeDtypeStruct((m, f), BF16),
        scratch_shapes=[pltpu.VMEM((k, tn), BF16), pltpu.VMEM((k, tn), BF16)],
        compiler_params=_cp(("arbitrary", "arbitrary")),
    )(x, wg, wu)


def _down_resid_body(a_ref, w_ref, x_ref, tab_ref, *rest, tm, tb, lc, expert):
    if expert is None:
        o_ref, wb_ref = rest
    else:
        rs_ref, o_ref, wb_ref = rest

    @pl.when(pl.program_id(1) == 0)
    def _():
        wb_ref[...] = w_ref[...].astype(BF16)

    y = jnp.dot(a_ref[...], wb_ref[...], preferred_element_type=F32)
    gate = _row_mod(tab_ref, pl.program_id(1), tm, tb, lc)
    if expert is not None:
        gate = gate * rs_ref[:, expert:expert + 1]
    o_ref[...] = x_ref[...] + gate * y


def _down_resid(a, w, lead, x, mod, gate_idx, *, tb, lc, tm, tn, rowscale=None, expert=None):
    m, k = a.shape
    d = x.shape[1]
    goff = gate_idx * (d // tn)
    in_specs = [pl.BlockSpec((tm, k), lambda j, i: (i, 0)),
                _lead_spec(lead, (k, tn), lambda j, i: (0, j)),
                pl.BlockSpec((tm, tn), lambda j, i: (i, j)),
                pl.BlockSpec((8, tn), lambda j, i: (0, goff + j))]
    args = [a, w, x, mod]
    if rowscale is not None:
        in_specs.append(pl.BlockSpec((tm, LANES), lambda j, i: (i, 0)))
        args.append(rowscale)
    return pl.pallas_call(
        functools.partial(_down_resid_body, tm=tm, tb=tb, lc=lc, expert=expert),
        grid=(d // tn, m // tm),
        in_specs=in_specs,
        out_specs=pl.BlockSpec((tm, tn), lambda j, i: (i, j)),
        out_shape=jax.ShapeDtypeStruct((m, d), F32),
        scratch_shapes=[pltpu.VMEM((k, tn), BF16)],
        input_output_aliases={2: 0},
        compiler_params=_cp(("arbitrary", "arbitrary")),
    )(*args)


def _merge_body(ys_ref, yr_ref, ws_ref, wr_ref, gs_ref, gr_ref, o_ref, wsb_ref, wrb_ref):
    @pl.when(pl.program_id(1) == 0)
    def _():
        wsb_ref[...] = ws_ref[...].astype(BF16)
        wrb_ref[...] = wr_ref[...].astype(BF16)

    ps = jnp.dot(ys_ref[...], wsb_ref[...], preferred_element_type=F32)
    pr = jnp.dot(yr_ref[...], wrb_ref[...], preferred_element_type=F32)
    o_ref[...] = (_sigmoid(gs_ref[...]) * ps + _sigmoid(gr_ref[...]) * pr).astype(o_ref.dtype)


def _merge(ys, yr, w_ps, w_pr, layer, zg, *, tm, tn):
    m = ys.shape[0]
    d = w_ps.shape[-1]
    nb = d // tn
    return pl.pallas_call(
        _merge_body,
        grid=(nb, m // tm),
        in_specs=[pl.BlockSpec((tm, S5_W), lambda j, i: (i, 0)),
                  pl.BlockSpec((tm, RW), lambda j, i: (i, 0)),
                  _lead_spec((layer,), (S5_W, tn), lambda j, i: (0, j)),
                  _lead_spec((layer,), (RW, tn), lambda j, i: (0, j)),
                  pl.BlockSpec((tm, tn), lambda j, i: (i, j)),
                  pl.BlockSpec((tm, tn), lambda j, i: (i, nb + j))],
        out_specs=pl.BlockSpec((tm, tn), lambda j, i: (i, j)),
        out_shape=jax.ShapeDtypeStruct((m, d), BF16),
        scratch_shapes=[pltpu.VMEM((S5_W, tn), BF16), pltpu.VMEM((RW, tn), BF16)],
        compiler_params=_cp(("arbitrary", "arbitrary")),
    )(ys, yr, w_ps, w_pr, zg, zg)


def _norm_mod_body(x_ref, gain_ref, sh_ref, sc_ref, o_ref, *, tm, tb, lc):
    x = x_ref[...]
    y = x * lax.rsqrt(jnp.mean(x * x, axis=-1, keepdims=True) + NORM_EPS) * gain_ref[...]
    i = pl.program_id(0)
    shift = _row_mod(sh_ref, i, tm, tb, lc)
    scale = _row_mod(sc_ref, i, tm, tb, lc)
    o_ref[...] = (y * (1.0 + scale) + shift).astype(o_ref.dtype)


def _norm_mod(x, gain, layer, mod, shift_idx, *, tb, lc, tm, out_dtype=BF16):
    m, d = x.shape
    return pl.pallas_call(
        functools.partial(_norm_mod_body, tm=tm, tb=tb, lc=lc),
        grid=(m // tm,),
        in_specs=[pl.BlockSpec((tm, d), lambda i: (i, 0)),
                  pl.BlockSpec((None, 1, d), lambda i: (layer, 0, 0)),
                  pl.BlockSpec((8, d), lambda i: (0, shift_idx)),
                  pl.BlockSpec((8, d), lambda i: (0, shift_idx + 1))],
        out_specs=pl.BlockSpec((tm, d), lambda i: (i, 0)),
        out_shape=jax.ShapeDtypeStruct((m, d), out_dtype),
        compiler_params=_cp(("arbitrary",)),
    )(x, gain, mod, mod)


def _final_norm_body(x_ref, gain_ref, o_ref):
    x = x_ref[...]
    o_ref[...] = x * lax.rsqrt(jnp.mean(x * x, axis=-1, keepdims=True) + NORM_EPS) * gain_ref[...]


def _final_norm(x3, gain, *, lc, tm):
    b, tb, d = x3.shape
    l = tb - lc
    skip = lc // tm
    return pl.pallas_call(
        _final_norm_body,
        grid=(b, l // tm),
        in_specs=[pl.BlockSpec((None, tm, d), lambda bi, t: (bi, skip + t, 0)),
                  pl.BlockSpec((1, d), lambda bi, t: (0, 0))],
        out_specs=pl.BlockSpec((None, tm, d), lambda bi, t: (bi, t, 0)),
        out_shape=jax.ShapeDtypeStruct((b, l, d), F32),
        compiler_params=_cp(("arbitrary", "arbitrary")),
    )(x3, gain)


def _router_body(x_ref, gain_ref, sh_ref, sc_ref, w_ref, b_ref, o_ref, *, tm, tb, lc):
    x = x_ref[...]
    y = x * lax.rsqrt(jnp.mean(x * x, axis=-1, keepdims=True) + NORM_EPS) * gain_ref[...]
    i = pl.program_id(0)
    h = y * (1.0 + _row_mod(sc_ref, i, tm, tb, lc)) + _row_mod(sh_ref, i, tm, tb, lc)
    logits = jnp.dot(h, w_ref[...], preferred_element_type=F32,
                     precision=lax.Precision.HIGHEST) + b_ref[...]
    lane = lax.broadcasted_iota(jnp.int32, logits.shape, 1).astype(F32)
    neg = jnp.float32(-jnp.inf)
    logits = jnp.where(lane < N_EXPERTS, logits, neg)
    v1 = jnp.max(logits, axis=-1, keepdims=True)
    i1 = jnp.min(jnp.where(logits == v1, lane, float(LANES)), axis=-1, keepdims=True)
    rest = jnp.where(lane == i1, neg, logits)
    v2 = jnp.max(rest, axis=-1, keepdims=True)
    i2 = jnp.min(jnp.where(rest == v2, lane, float(LANES)), axis=-1, keepdims=True)
    e2 = jnp.exp(v2 - v1)
    w1 = 1.0 / (1.0 + e2)
    w2 = e2 / (1.0 + e2)
    o_ref[...] = (jnp.where(lane == i1, w1, 0.0) + jnp.where(lane == i2, w2, 0.0)
                  + jnp.where(lane == N_EXPERTS, i1, 0.0) + jnp.where(lane == N_EXPERTS + 1, i2, 0.0)
                  + jnp.where(lane == N_EXPERTS + 2, w1, 0.0) + jnp.where(lane == N_EXPERTS + 3, w2, 0.0))


def _router_gates(x, gain, layer, mod, router_pad, bias_pad, *, tb, lc, tm):
    m, d = x.shape
    return pl.pallas_call(
        functools.partial(_router_body, tm=tm, tb=tb, lc=lc),
        grid=(m // tm,),
        in_specs=[pl.BlockSpec((tm, d), lambda i: (i, 0)),
                  pl.BlockSpec((None, 1, d), lambda i: (layer, 0, 0)),
                  pl.BlockSpec((8, d), lambda i: (0, 3)),
                  pl.BlockSpec((8, d), lambda i: (0, 4)),
                  pl.BlockSpec((d, LANES), lambda i: (0, 0)),
                  pl.BlockSpec((1, LANES), lambda i: (0, 0))],
        out_specs=pl.BlockSpec((tm, LANES), lambda i: (i, 0)),
        out_shape=jax.ShapeDtypeStruct((m, LANES), F32),
        compiler_params=_cp(("arbitrary",)),
    )(x, gain, mod, mod, router_pad, bias_pad)


def _s5_tables(lam_re, lam_im, log_step, b_re, b_im, c_re, c_im):
    hp = lax.Precision.HIGHEST
    t = S5_T
    lam_re, lam_im = lam_re.astype(F32), lam_im.astype(F32)
    b_re, b_im = b_re.astype(F32), b_im.astype(F32)
    c_re, c_im = c_re.astype(F32), c_im.astype(F32)
    step = jnp.exp(log_step.astype(F32))[..., None]
    tau = jnp.arange(t + 1, dtype=F32)[None, :, None, None]
    mag = jnp.exp((lam_re * step)[:, None] * tau)
    ang = (lam_im * step)[:, None] * tau
    pw_re, pw_im = mag * jnp.cos(ang), mag * jnp.sin(ang)
    nr, ni = pw_re[:, 1] - 1.0, pw_im[:, 1]
    den = lam_re * lam_re + lam_im * lam_im
    q_re = (nr * lam_re + ni * lam_im) / den
    q_im = (ni * lam_re - nr * lam_im) / den
    bb_re = q_re[..., None] * b_re - q_im[..., None] * b_im
    bb_im = q_re[..., None] * b_im + q_im[..., None] * b_re

    def cmul(ar, ai, br, bi):
        return ar * br - ai * bi, ar * bi + ai * br

    x_re, x_im = cmul(pw_re[:, :t, :, :, None], pw_im[:, :t, :, :, None], bb_re[:, None], bb_im[:, None])
    kd = (jnp.einsum('ghp,dtgpk->dtghk', c_re, x_re, precision=hp)
          - jnp.einsum('ghp,dtgpk->dtghk', c_im, x_im, precision=hp))
    kf, kb = kd[0], kd[1]
    kall = jnp.concatenate([kb[1:][::-1], (kf[0] + kb[0])[None], kf[1:]], axis=0).astype(BF16)
    idx = (jnp.arange(t)[None, :] - jnp.arange(t)[:, None]) + (t - 1)
    kbig = jnp.transpose(kall[idx], (2, 0, 4, 1, 3)).reshape(S5_G, t * S5_H, t * S5_H)

    def chunk_in(d, powers):
        gr, gi = cmul(pw_re[d][powers][..., None], pw_im[d][powers][..., None], bb_re[d][None], bb_im[d][None])
        pack = lambda g: jnp.transpose(g, (1, 0, 3, 2)).reshape(S5_G, t * S5_H, S5_P)
        return pack(gr), pack(gi)

    def chunk_out(d, powers):
        cr, ci = cmul(c_re[None], c_im[None], pw_re[d][powers][:, :, None, :], pw_im[d][powers][:, :, None, :])
        pack = lambda g: jnp.transpose(g, (1, 3, 0, 2)).reshape(S5_G, S5_P, t * S5_H)
        return pack(cr), -pack(ci)

    gin = jnp.stack(chunk_in(0, t - 1 - jnp.arange(t)) + chunk_in(1, jnp.arange(t)), axis=1)
    cout = jnp.stack(chunk_out(0, 1 + jnp.arange(t)) + chunk_out(1, t - jnp.arange(t)), axis=1)
    lam_t = jnp.stack([pw_re[0, t], pw_im[0, t], pw_re[1, t], pw_im[1, t]], axis=1)
    return kbig, gin.astype(BF16), cout.astype(BF16), lam_t[:, :, None, :]


def _s5_body(u_ref, k_ref, gin_ref, cout_ref, lam_ref, o_ref, g_s, h_s, *, nb, ncb, ncc):
    u = u_ref[...]
    for q in range(4):
        g_s[q] = jnp.dot(u, gin_ref[q], preferred_element_type=F32)
    lfr, lfi, lbr, lbi = lam_ref[0], lam_ref[1], lam_ref[2], lam_ref[3]
    nlc = ncb - ncc

    def step(n, carry):
        out = []
        for b in range(nb):
            fr, fi, br, bi = carry[4 * b:4 * b + 4]
            rf = b * ncb + n
            nb_idx = jnp.where(n < ncc, ncc - 1 - n, ncc + (nlc - 1) - (n - ncc))
            rb = b * ncb + nb_idx
            h_s[0, pl.ds(rf, 1), :] = fr
            h_s[1, pl.ds(rf, 1), :] = fi
            h_s[2, pl.ds(rb, 1), :] = br
            h_s[3, pl.ds(rb, 1), :] = bi
            out += [lfr * fr - lfi * fi + g_s[0, pl.ds(rf, 1), :],
                    lfr * fi + lfi * fr + g_s[1, pl.ds(rf, 1), :],
                    lbr * br - lbi * bi + g_s[2, pl.ds(rb, 1), :],
                    lbr * bi + lbi * br + g_s[3, pl.ds(rb, 1), :]]
        return tuple(out)

    zero = jnp.zeros((1, S5_P), F32)
    lax.fori_loop(0, ncb, step, (zero,) * (4 * nb))
    y = jnp.dot(u, k_ref[...], preferred_element_type=F32)
    for q in range(4):
        y = y + jnp.dot(h_s[q].astype(BF16), cout_ref[q], preferred_element_type=F32)
    o_ref[...] = y


def _s5_mix(ug, tables, layer, *, nb, ncb, ncc):
    kbig, gin, cout, lam_t = tables
    g, nc, tw = ug.shape
    return pl.pallas_call(
        functools.partial(_s5_body, nb=nb, ncb=ncb, ncc=ncc),
        grid=(g,),
        in_specs=[pl.BlockSpec((None, nc, tw), lambda i: (i, 0, 0)),
                  pl.BlockSpec((None, None, tw, tw), lambda i: (layer, i, 0, 0)),
                  pl.BlockSpec((None, None, 4, tw, S5_P), lambda i: (layer, i, 0, 0, 0)),
                  pl.BlockSpec((None, None, 4, S5_P, tw), lambda i: (layer, i, 0, 0, 0)),
                  pl.BlockSpec((None, None, 4, 1, S5_P), lambda i: (layer, i, 0, 0, 0))],
        out_specs=pl.BlockSpec((None, nc, tw), lambda i: (i, 0, 0)),
        out_shape=jax.ShapeDtypeStruct((g, nc, tw), F32),
        scratch_shapes=[pltpu.VMEM((4, nc, S5_P), F32), pltpu.VMEM((4, nc, S5_P), F32)],
        compiler_params=_cp(("arbitrary",)),
    )(ug, kbig, gin, cout, lam_t)


def _s5_readout_body(y_ref, u_ref, d_ref, w_ref, o_ref, wb_ref):
    @pl.when(pl.program_id(0) == 0)
    def _():
        wb_ref[...] = w_ref[...].astype(BF16)

    y = _gelu_tanh(y_ref[...] + d_ref[...] * u_ref[...])
    z = jnp.dot(y.astype(BF16), wb_ref[...], preferred_element_type=F32)
    o_ref[...] = (y * _sigmoid(z)).astype(o_ref.dtype)


def _s5_readout(y, u, d_skip, w_glu, layer, *, tm):
    m = y.shape[0]
    return pl.pallas_call(
        _s5_readout_body,
        grid=(m // tm,),
        in_specs=[pl.BlockSpec((tm, S5_W), lambda i: (i, 0)),
                  pl.BlockSpec((tm, S5_W), lambda i: (i, 0)),
                  pl.BlockSpec((None, 1, S5_W), lambda i: (layer, 0, 0)),
                  pl.BlockSpec((None, S5_W, S5_W), lambda i: (layer, 0, 0))],
        out_specs=pl.BlockSpec((tm, S5_W), lambda i: (i, 0)),
        out_shape=jax.ShapeDtypeStruct((m, S5_W), BF16),
        scratch_shapes=[pltpu.VMEM((S5_W, S5_W), BF16)],
        compiler_params=_cp(("arbitrary",)),
    )(y, u, d_skip, w_glu)


def _rw_feat_body(z_ref, up_ref, dn_ref, mu_ref, wup_ref, aup_ref, gup_ref, vec_ref, ones_ref, tri_ref,
                  v_ref, g_ref, bonus_ref, ah_ref, rh_ref, bh_ref, kh_ref, bt_ref, kt_ref, pt_ref,
                  *, tt, tb, lc):
    i = pl.program_id(0)
    start = i * tt
    b = start // tb
    pos = (start - b * tb) + lax.broadcasted_iota(jnp.int32, (tt, 1), 0)
    is_ctx = pos < lc
    tl = pos - lc
    col = tl & (GRID_W - 1)
    l_lat = tb - lc

    z = z_ref[...]
    prev = pltpu.roll(z, 1, axis=0)
    nxt = pltpu.roll(z, tt - 1, axis=0)
    if tt > GRID_W:
        up = jnp.concatenate([up_ref[...], z[:tt - GRID_W]], axis=0)
        down = jnp.concatenate([z[GRID_W:], dn_ref[...]], axis=0)
    else:
        up, down = up_ref[...], dn_ref[...]
    m_prev = jnp.logical_or(jnp.logical_and(is_ctx, pos >= 1), jnp.logical_and(tl >= 0, col >= 1))
    m_next = jnp.logical_or(jnp.logical_and(is_ctx, pos <= lc - 2),
                            jnp.logical_and(tl >= 0, col <= GRID_W - 2))
    m_up = tl >= GRID_W
    m_down = jnp.logical_and(tl >= 0, tl < l_lat - GRID_W)
    prev = jnp.where(m_prev, prev, 0.0)
    nxt = jnp.where(m_next, nxt, 0.0)
    up = jnp.where(is_ctx, prev, jnp.where(m_up, up, 0.0))
    down = jnp.where(is_ctx, nxt, jnp.where(m_down, down, 0.0))
    l4 = lax.broadcasted_iota(jnp.int32, (1, ZR_COLS), 1) & 3
    shifted = jnp.where(l4 == 0, prev, jnp.where(l4 == 1, nxt, jnp.where(l4 == 2, up, down)))
    z = z + mu_ref[...] * (shifted - z)

    r = z[:, 0:RW]
    k = z[:, RW:2 * RW]
    v = z[:, 2 * RW:3 * RW]
    lora_w = jnp.tanh(z[:, 3 * RW:3 * RW + LANES]).astype(BF16)
    lora_a = z[:, 3 * RW + LANES:3 * RW + 2 * LANES].astype(BF16)
    lora_g = _sigmoid(z[:, 3 * RW + 2 * LANES:3 * RW + 4 * LANES]).astype(BF16)
    k_k, k_a, r_k = vec_ref[0:1, :], vec_ref[1:2, :], vec_ref[2:3, :]
    ones = ones_ref[...]

    kk = k * k_k
    kk = kk * lax.rsqrt(jnp.maximum(_split_dot(kk * kk, ones), 1e-24))
    ksum = jnp.zeros_like(k)
    for d in range(2):
        w0 = vec_ref[3 + d:4 + d, :]
        a0 = vec_ref[5 + d:6 + d, :]
        wl = w0 + jnp.dot(lora_w, wup_ref[d], preferred_element_type=F32)
        lw = -math.exp(-0.5) * _sigmoid(wl)
        lw2 = lw * LOG2_E
        a = _sigmoid(a0 + jnp.dot(lora_a, aup_ref[d], preferred_element_type=F32))
        k_d = k * (1.0 + (a - 1.0) * k_a)
        bb = kk * a
        ksum = ksum + k_d
        lw_hi = lw2.astype(BF16)
        lw_lo = (lw2 - lw_hi.astype(F32)).astype(BF16)
        cl = (jnp.dot(tri_ref[d, 0], lw_hi, preferred_element_type=F32)
              + jnp.dot(tri_ref[d, 0], lw_lo, preferred_element_type=F32))
        rem = (jnp.dot(tri_ref[d, 1], lw_hi, preferred_element_type=F32)
               + jnp.dot(tri_ref[d, 1], lw_lo, preferred_element_type=F32))
        p_in = jnp.exp2(cl)
        p_inv = jnp.exp2(-cl)
        p_rem = jnp.exp2(rem)
        ah_ref[d] = (-kk * jnp.exp2(cl - lw2)).astype(BF16)
        rh_ref[d] = (r * p_in).astype(BF16)
        bh_ref[d] = (bb * p_inv).astype(BF16)
        kh_ref[d] = (k_d * p_inv).astype(BF16)
        bt_ref[d] = (bb * p_rem).astype(BF16)
        kt_ref[d] = (k_d * p_rem).astype(BF16)
        last = 0 if d == 1 else WKV_CHUNK - 1
        for cidx in range(tt // WKV_CHUNK):
            row = cidx * WKV_CHUNK + last
            pt_ref[d, cidx] = p_in[row:row + 1, :]
    v_ref[...] = v.astype(BF16)
    g_ref[...] = jnp.dot(lora_g, gup_ref[...], preferred_element_type=F32)
    bonus_ref[...] = _split_dot(r * ksum * r_k, ones) * v


def _rw_features(zr, layer, mu_pad, wup, aup, gup, vecs, ones_blk, tri, *, tb, lc, tt):
    m = zr.shape[0]
    nh = tt // GRID_W
    nblk64 = m // GRID_W
    ncht = tt // WKV_CHUNK
    full = pl.BlockSpec((tt, RW), lambda i: (i, 0))
    per_dir = pl.BlockSpec((2, tt, RW), lambda i: (0, i, 0))
    one = jax.ShapeDtypeStruct((m, RW), F32)
    two = jax.ShapeDtypeStruct((2, m, RW), BF16)
    return pl.pallas_call(
        functools.partial(_rw_feat_body, tt=tt, tb=tb, lc=lc),
        grid=(m // tt,),
        in_specs=[pl.BlockSpec((tt, ZR_COLS), lambda i: (i, 0)),
                  pl.BlockSpec((GRID_W, ZR_COLS), lambda i: (jnp.maximum(i * nh - 1, 0), 0)),
                  pl.BlockSpec((GRID_W, ZR_COLS), lambda i: (jnp.minimum((i + 1) * nh, nblk64 - 1), 0)),
                  pl.BlockSpec((None, 1, ZR_COLS), lambda i: (layer, 0, 0)),
                  pl.BlockSpec((None, 2, LANES, RW), lambda i: (layer, 0, 0, 0)),
                  pl.BlockSpec((None, 2, LANES, RW), lambda i: (layer, 0, 0, 0)),
                  pl.BlockSpec((None, 2 * LANES, RW), lambda i: (layer, 0, 0)),
                  pl.BlockSpec((None, 8, RW), lambda i: (layer, 0, 0)),
                  pl.BlockSpec((RW, RW), lambda i: (0, 0)),
                  pl.BlockSpec((2, 2, tt, tt), lambda i: (0, 0, 0, 0))],
        out_specs=[full, full, full, per_dir, per_dir, per_dir, per_dir, per_dir, per_dir,
                   pl.BlockSpec((2, ncht, 1, RW), lambda i: (0, i, 0, 0))],
        out_shape=[jax.ShapeDtypeStruct((m, RW), BF16), one, one, two, two, two, two, two, two,
                   jax.ShapeDtypeStruct((2, m // WKV_CHUNK, 1, RW), F32)],
        compiler_params=_cp(("arbitrary",)),
    )(zr, zr, zr, mu_pad, wup, aup, gup, vecs, ones_blk, tri)


def _chunk_tri(tt):
    t = jnp.arange(tt)[:, None]
    s = jnp.arange(tt)[None, :]
    same = (t // WKV_CHUNK) == (s // WKV_CHUNK)
    fwd = jnp.stack([same & (s <= t), same & (s > t)])
    bwd = jnp.stack([same & (s >= t), same & (s < t)])
    return jnp.stack([fwd, bwd]).astype(BF16)


def _dot_nt(a, b):
    return lax.dot_general(a, b, (((1,), (1,)), ((), ())), preferred_element_type=F32)


def _dot_tn(a, b):
    return lax.dot_general(a, b, (((0,), (0,)), ((), ())), preferred_element_type=F32)


def _wkv_body(*refs):
    c = WKV_CHUNK
    ins = (refs[0:8], refs[8:16])
    y_refs = refs[16:18]
    s_ref = refs[18]

    @pl.when(pl.program_id(1) == 0)
    def _():
        s_ref[...] = jnp.zeros_like(s_ref)

    t_idx = lax.broadcasted_iota(jnp.int32, (2 * c, LANES), 0)
    lane = lax.broadcasted_iota(jnp.int32, (2 * c, LANES), 1)
    diff = (t_idx & (c - 1)) - (lane & (c - 1))
    tri = (jnp.where(t_idx < c, diff, diff + 1) > 0, jnp.where(t_idx < c, -diff, 1 - diff) > 0)
    same_head = (t_idx // c) == (lane // c)
    lo = lax.broadcasted_iota(jnp.int32, (c, LANES), 1) < c

    def bd(x):
        zero = jnp.zeros_like(x)
        return jnp.concatenate([jnp.where(lo, x, zero), jnp.where(lo, zero, x)], axis=0)

    chains = [(d, p) for d in range(2) for p in range(HEADS // 2)]
    n = range(len(chains))
    sl = [slice(p * LANES, (p + 1) * LANES) for _, p in chains]
    ah, rh, bh, kh, bt, kt, vv, pt = ([ins[d][q] for d, _ in chains] for q in range(8))
    ar = [jnp.concatenate([ah[i][:, sl[i]], rh[i][:, sl[i]]], axis=0) for i in n]
    s_old = [s_ref[d, p] for d, p in chains]
    pbk = [_dot_nt(ar[i], jnp.concatenate([bd(bh[i][:, sl[i]]), bd(kh[i][:, sl[i]])], axis=0)) for i in n]
    ps = [_dot_nt(ar[i], s_old[i].astype(BF16)) for i in n]
    pb = [jnp.where(tri[chains[i][0]], pbk[i][:, :LANES], 0.0) for i in n]
    pk = [jnp.where(tri[chains[i][0]], pbk[i][:, LANES:], 0.0) for i in n]
    bdv = [bd(vv[i][:, sl[i]]) for i in n]
    x = [ps[i][:c] + jnp.dot(pk[i][:c].astype(BF16), bdv[i], preferred_element_type=F32) for i in n]
    lp = [pb[i][:c].astype(BF16) for i in n]
    for it in range(6):
        if it < 5:
            prod = [jnp.dot(lp[i], jnp.concatenate([bd(x[i].astype(BF16)), bd(lp[i])], axis=1),
                            preferred_element_type=F32) for i in n]
            x = [x[i] + prod[i][:, :LANES] for i in n]
            lp = [prod[i][:, LANES:].astype(BF16) for i in n]
        else:
            x = [x[i] + jnp.dot(lp[i], bd(x[i].astype(BF16)), preferred_element_type=F32) for i in n]
    u = [x[i].astype(BF16) for i in n]
    for i in n:
        y_refs[chains[i][0]][:, sl[i]] = ps[i][c:] + jnp.dot(
            jnp.concatenate([pb[i][c:], pk[i][c:]], axis=1).astype(BF16),
            jnp.concatenate([bd(u[i]), bdv[i]], axis=0), preferred_element_type=F32)
    upd = [_dot_tn(jnp.concatenate([u[i], vv[i][:, sl[i]]], axis=0),
                   jnp.concatenate([bt[i][:, sl[i]], kt[i][:, sl[i]]], axis=0)) for i in n]
    for i, (d, p) in enumerate(chains):
        s_ref[d, p] = s_old[i] * pt[i][:, sl[i]] + jnp.where(same_head, upd[i], 0.0)


def _wkv_scan(ah, rh, bh, kh, bt, kt, v, pt, *, nb, tb, lc):
    m = v.shape[0]
    c = WKV_CHUNK
    ncb, ncc = tb // c, lc // c
    nlc = ncb - ncc

    def chunk(d, b, n):
        if d == 0:
            return b * ncb + n
        return b * ncb + jnp.where(n < ncc, ncc - 1 - n, ncc + (nlc - 1) - (n - ncc))

    def specs(d):
        per_dir = pl.BlockSpec((None, c, RW), lambda b, n: (d, chunk(d, b, n), 0))
        return [per_dir] * 6 + [pl.BlockSpec((c, RW), lambda b, n: (chunk(d, b, n), 0)),
                                pl.BlockSpec((None, None, 1, RW), lambda b, n: (d, chunk(d, b, n), 0, 0))]

    out = jax.ShapeDtypeStruct((m, RW), F32)
    args = (ah, rh, bh, kh, bt, kt, v, pt)
    return pl.pallas_call(
        _wkv_body,
        grid=(nb, ncb),
        in_specs=specs(0) + specs(1),
        out_specs=[pl.BlockSpec((c, RW), lambda b, n: (chunk(0, b, n), 0)),
                   pl.BlockSpec((c, RW), lambda b, n: (chunk(1, b, n), 0))],
        out_shape=[out, out],
        scratch_shapes=[pltpu.VMEM((2, HEADS // 2, LANES, LANES), F32)],
        compiler_params=_cp(("arbitrary", "arbitrary")),
    )(*args, *args)


def _rw_readout_body(yf_ref, yb_ref, bonus_ref, g_ref, gn_ref, ones_ref, o_ref):
    y = yf_ref[...] + yb_ref[...]
    ones = ones_ref[...]
    mean = _split_dot(y, ones) * (1.0 / HEAD)
    yc = y - mean
    var = _split_dot(yc * yc, ones) * (1.0 / HEAD)
    yn = yc * lax.rsqrt(var + GN_EPS) * gn_ref[0:1, :] + gn_ref[1:2, :]
    o_ref[...] = ((yn + bonus_ref[...]) * g_ref[...]).astype(o_ref.dtype)


def _rw_readout(y_f, y_b, bonus, g, gn, layer, ones_blk, *, tm):
    m = y_f.shape[0]
    full = pl.BlockSpec((tm, RW), lambda i: (i, 0))
    return pl.pallas_call(
        _rw_readout_body,
        grid=(m // tm,),
        in_specs=[full, full, full, full,
                  pl.BlockSpec((None, 8, RW), lambda i: (layer, 0, 0)),
                  pl.BlockSpec((RW, RW), lambda i: (0, 0))],
        out_specs=full,
        out_shape=jax.ShapeDtypeStruct((m, RW), BF16),
        compiler_params=_cp(("arbitrary",)),
    )(y_f, y_b, bonus, g, gn, ones_blk)


MOE_TILE = 1024


def _moe_plan(route, tme):
    m = route.shape[0]
    n_asg = 2 * m
    idx = route[:, N_EXPERTS:N_EXPERTS + 2].astype(jnp.int32)
    e_flat = idx.T.reshape(n_asg)
    w_flat = route[:, N_EXPERTS + 2:N_EXPERTS + 4].T.reshape(n_asg)
    tok_flat = jnp.tile(jnp.arange(m, dtype=jnp.int32), 2)
    order = jnp.argsort(e_flat, stable=True).astype(jnp.int32)
    e_sorted, tok_sorted, w_sorted = e_flat[order], tok_flat[order], w_flat[order]
    experts = jnp.arange(N_EXPERTS, dtype=jnp.int32)
    cnt = jnp.sum((e_flat[:, None] == experts[None, :]).astype(jnp.int32), axis=0)
    start = jnp.cumsum(cnt) - cnt
    padded = ((cnt + tme - 1) // tme) * tme
    off_end = jnp.cumsum(padded)
    off = off_end - padded
    n_rows = ((n_asg + tme - 1) // tme) * tme + N_EXPERTS * tme
    n_tiles = n_rows // tme
    tile_e = jnp.sum((jnp.arange(n_tiles, dtype=jnp.int32)[:, None] * tme >= off_end[None, :]).astype(jnp.int32),
                     axis=1)
    tile_e = jnp.minimum(tile_e, N_EXPERTS - 1)
    first = jnp.concatenate([jnp.ones((1,), jnp.int32), (tile_e[1:] != tile_e[:-1]).astype(jnp.int32)])
    n_act = (off_end[-1] // tme).reshape(1)
    e_row = jnp.repeat(tile_e, tme)
    q = jnp.arange(n_rows, dtype=jnp.int32) - off[e_row]
    valid = q < cnt[e_row]
    s_row = jnp.clip(start[e_row] + q, 0, n_asg - 1)
    src_tok = jnp.where(valid, tok_sorted[s_row], 0)
    row_w = jnp.where(valid, w_sorted[s_row], 0.0)[:, None]
    dest_sorted = off[e_sorted] + (jnp.arange(n_asg, dtype=jnp.int32) - start[e_sorted])
    pos = jnp.zeros((n_asg,), jnp.int32).at[order].set(dest_sorted, unique_indices=True)
    return dict(src_tok=src_tok, row_w=row_w, pos=pos, tile_e=tile_e, first=first, n_act=n_act,
                n_rows=n_rows, n_tiles=n_tiles)


def _gather_body(idx_ref, nact_ref, src_ref, o_ref, sem, *, rows):
    i = pl.program_id(0)
    base = i * rows

    @pl.when(i < nact_ref[0])
    def _():
        def issue(r, carry):
            pltpu.make_async_copy(src_ref.at[idx_ref[base + r]], o_ref.at[r], sem).start()
            return carry

        lax.fori_loop(0, rows, issue, 0, unroll=8)
        pltpu.make_async_copy(o_ref, o_ref, sem).wait()

    @pl.when(i >= nact_ref[0])
    def _():
        o_ref[...] = jnp.zeros_like(o_ref)


def _gather_rows(src3, idx, n_act, *, rows):
    n_rows = idx.shape[0]
    blk = (rows,) + src3.shape[1:]
    return pl.pallas_call(
        functools.partial(_gather_body, rows=rows),
        grid_spec=pltpu.PrefetchScalarGridSpec(
            num_scalar_prefetch=2, grid=(n_rows // rows,),
            in_specs=[pl.BlockSpec(memory_space=pl.ANY)],
            out_specs=pl.BlockSpec(blk, lambda i, idx, na: (i, 0, 0)),
            scratch_shapes=[pltpu.SemaphoreType.DMA(())]),
        out_shape=jax.ShapeDtypeStruct((n_rows,) + src3.shape[1:], src3.dtype),
        compiler_params=_cp(("arbitrary",)),
    )(idx, n_act, src3)


def _moe_up_body(te_ref, first_ref, nact_ref, x_ref, wg_ref, wu_ref, o_ref, wgb_ref, wub_ref):
    i = pl.program_id(1)

    @pl.when(i < nact_ref[0])
    def _():
        @pl.when(first_ref[i] == 1)
        def _():
            wgb_ref[...] = wg_ref[...].astype(BF16)
            wub_ref[...] = wu_ref[...].astype(BF16)

        x = x_ref[...]
        g = jnp.dot(x, wgb_ref[...], preferred_element_type=F32)
        u = jnp.dot(x, wub_ref[...], preferred_element_type=F32)
        o_ref[...] = (g * _sigmoid(g) * u).astype(o_ref.dtype)

    @pl.when(i >= nact_ref[0])
    def _():
        o_ref[...] = jnp.zeros_like(o_ref)


def _moe_up(xg, wg, wu, layer, plan, *, tme, tn):
    p, k = xg.shape
    f = wg.shape[-1]

    def row(i, nact):
        return jnp.minimum(i, nact[0] - 1)

    wspec = pl.BlockSpec((None, None, k, tn), lambda j, i, te, fi, na: (layer, te[row(i, na)], 0, j))
    return pl.pallas_call(
        _moe_up_body,
        grid_spec=pltpu.PrefetchScalarGridSpec(
            num_scalar_prefetch=3, grid=(f // tn, p // tme),
            in_specs=[pl.BlockSpec((tme, k), lambda j, i, te, fi, na: (row(i, na), 0)), wspec, wspec],
            out_specs=pl.BlockSpec((tme, tn), lambda j, i, te, fi, na: (i, j)),
            scratch_shapes=[pltpu.VMEM((k, tn), BF16), pltpu.VMEM((k, tn), BF16)]),
        out_shape=jax.ShapeDtypeStruct((p, f), BF16),
        compiler_params=_cp(("arbitrary", "arbitrary")),
    )(plan["tile_e"], plan["first"], plan["n_act"], xg, wg, wu)


def _moe_down_body(te_ref, first_ref, nact_ref, a_ref, w_ref, rw_ref, o_ref, wb_ref):
    i = pl.program_id(1)

    @pl.when(i < nact_ref[0])
    def _():
        @pl.when(first_ref[i] == 1)
        def _():
            wb_ref[...] = w_ref[...].astype(BF16)

        o_ref[...] = rw_ref[...] * jnp.dot(a_ref[...], wb_ref[...], preferred_element_type=F32)

    @pl.when(i >= nact_ref[0])
    def _():
        o_ref[...] = jnp.zeros_like(o_ref)


def _moe_down(act, wd, layer, plan, *, tme, tn):
    p, k = act.shape
    d = wd.shape[-1]

    def row(i, nact):
        return jnp.minimum(i, nact[0] - 1)

    return pl.pallas_call(
        _moe_down_body,
        grid_spec=pltpu.PrefetchScalarGridSpec(
            num_scalar_prefetch=3, grid=(d // tn, p // tme),
            in_specs=[pl.BlockSpec((tme, k), lambda j, i, te, fi, na: (row(i, na), 0)),
                      pl.BlockSpec((None, None, k, tn), lambda j, i, te, fi, na: (layer, te[row(i, na)], 0, j)),
                      pl.BlockSpec((tme, 1), lambda j, i, te, fi, na: (row(i, na), 0))],
            out_specs=pl.BlockSpec((tme, tn), lambda j, i, te, fi, na: (i, j)),
            scratch_shapes=[pltpu.VMEM((k, tn), BF16)]),
        out_shape=jax.ShapeDtypeStruct((p, d), F32),
        compiler_params=_cp(("arbitrary", "arbitrary")),
    )(plan["tile_e"], plan["first"], plan["n_act"], act, wd, plan["row_w"])


def _moe_combine_body(pos_ref, x_ref, y_ref, tab_ref, o_ref, buf, sem, *, tc, m, tb, lc):
    i = pl.program_id(0)
    base = i * tc

    def issue(r, carry):
        pltpu.make_async_copy(y_ref.at[pos_ref[base + r]], buf.at[0, r], sem).start()
        pltpu.make_async_copy(y_ref.at[pos_ref[m + base + r]], buf.at[1, r], sem).start()
        return carry

    lax.fori_loop(0, tc, issue, 0)
    pltpu.make_async_copy(buf, buf, sem).wait()
    gate = _row_mod(tab_ref, i, tc, tb, lc)
    o_ref[...] = x_ref[...] + gate * (buf[0] + buf[1])


def _moe_combine(x, y, pos, mod, gate_idx, *, tb, lc, tc):
    m, d = x.shape
    return pl.pallas_call(
        functools.partial(_moe_combine_body, tc=tc, m=m, tb=tb, lc=lc),
        grid_spec=pltpu.PrefetchScalarGridSpec(
            num_scalar_prefetch=1, grid=(m // tc,),
            in_specs=[pl.BlockSpec((tc, d), lambda i, pos: (i, 0)),
                      pl.BlockSpec(memory_space=pl.ANY),
                      pl.BlockSpec((8, d), lambda i, pos: (0, gate_idx))],
            out_specs=pl.BlockSpec((tc, d), lambda i, pos: (i, 0)),
            scratch_shapes=[pltpu.VMEM((2, tc, d), F32), pltpu.SemaphoreType.DMA(())]),
        out_shape=jax.ShapeDtypeStruct((m, d), F32),
        input_output_aliases={1: 0},
        compiler_params=_cp(("arbitrary",)),
    )(pos, x, y, mod)


def _pad_rows(x, rows):
    return jnp.concatenate([x, jnp.zeros((rows - x.shape[0],) + x.shape[1:], x.dtype)], axis=0)


def _pick_tile(n, target):
    best = 8
    for t in range(8, target + 1, 8):
        if n % t == 0:
            best = t
    return best


def kernel(x, c, ctx, c_ctx, w_ada, b_ada, norm1, norm2, norm_f, w_in, s5_lambda_re, s5_lambda_im, s5_log_step, s5_b_re, s5_b_im, s5_c_re, s5_c_im, s5_d, s5_w_glu, rw_mu, rw_w0, rw_w_up, rw_a0, rw_a_up, rw_g_up, rw_k_k, rw_k_a, rw_r_k, rw_gn_w, rw_gn_b, w_proj_s5, w_proj_rw, w_out, ffn_w_gate, ffn_w_up, ffn_w_down, moe_router, moe_router_bias, moe_w_gate, moe_w_up, moe_w_down):
    nb, l, d = x.shape
    lc = ctx.shape[1]
    tb = lc + l
    m = nb * tb
    depth = w_in.shape[0]
    tt = lc
    assert lc % GRID_W == 0 and l % tt == 0 and l % GRID_W == 0 and nb * HEADS * 4 == LANES
    assert tb % S5_T == 0 and lc % S5_T == 0 and tt % WKV_CHUNK == 0
    tm = _pick_tile(tb, 1088)
    tr = _pick_tile(tb, 544)
    ncb, ncc = tb // S5_T, lc // S5_T

    xs = jnp.concatenate([ctx, x], axis=1).reshape(m, d)
    silu_rows = _pad_rows(jnp.concatenate([c_ctx[None], c], axis=0), 8)
    silu_rows = (silu_rows * _sigmoid(silu_rows)).astype(BF16)

    ones_blk = (jnp.arange(RW)[:, None] // HEAD == jnp.arange(RW)[None, :] // HEAD).astype(BF16)
    tri = _chunk_tri(tt)
    mu_pad = jnp.pad(rw_mu, ((0, 0), (0, ZR_COLS - rw_mu.shape[1])))[:, None, :]
    zeros64 = jnp.zeros((depth, 1, DECAY_LORA, RW), F32)
    wup = jnp.concatenate([jnp.concatenate([rw_w_up[:, :1], zeros64], axis=2),
                           jnp.concatenate([zeros64, rw_w_up[:, 1:]], axis=2)], axis=1).astype(BF16)
    aup = jnp.concatenate([jnp.concatenate([rw_a_up[:, :1], zeros64], axis=2),
                           jnp.concatenate([zeros64, rw_a_up[:, 1:]], axis=2)], axis=1).astype(BF16)
    gup = jnp.pad(rw_g_up, ((0, 0), (0, 2 * LANES - GATE_LORA), (0, 0))).astype(BF16)
    vecs = jnp.stack([rw_k_k, rw_k_a, rw_r_k.reshape(depth, RW), rw_w0[:, 0], rw_w0[:, 1],
                      rw_a0[:, 0], rw_a0[:, 1], jnp.zeros_like(rw_k_k)], axis=1)
    gn = jnp.pad(jnp.stack([rw_gn_w, rw_gn_b], axis=1), ((0, 0), (0, 6), (0, 0)))
    d_skip = s5_d.reshape(depth, 1, S5_W)
    norm1, norm2 = norm1[:, None, :], norm2[:, None, :]
    w_gates = w_in[:, :, S5_W + 3 * RW + LORA_COLS:]
    tables = jax.vmap(_s5_tables)(s5_lambda_re, s5_lambda_im, s5_log_step, s5_b_re, s5_b_im, s5_c_re, s5_c_im)

    for i in range(depth):
        mod = _matmul(silu_rows, w_ada, (i,), 6 * d, tm=8, tn=1024, bias=b_ada[i][None])

        h = _norm_mod(xs, norm1, i, mod, 0, tb=tb, lc=lc, tm=tr)
        u = _matmul(h, w_in, (i,), S5_W, tm=tm, tn=512)
        zr = _matmul(h, w_in, (i,), ZR_COLS, col_off=S5_W, tm=tm, tn=512)
        zg = _matmul(h, w_gates, (i,), 2 * d, tm=tm, tn=1024)

        ug = jnp.transpose(u.astype(BF16).reshape(nb * ncb, S5_T, S5_G, S5_H), (2, 0, 1, 3))
        ug = ug.reshape(S5_G, nb * ncb, S5_T * S5_H)
        yg = _s5_mix(ug, tables, i, nb=nb, ncb=ncb, ncc=ncc)
        ys = jnp.transpose(yg.reshape(S5_G, nb * ncb, S5_T, S5_H), (1, 2, 0, 3)).reshape(m, S5_W)
        y_s5 = _s5_readout(ys, u, d_skip, s5_w_glu, i, tm=tt)

        v, g, bonus, ah, rh, bh, kh, bt, kt, pt = _rw_features(zr, i, mu_pad, wup, aup, gup, vecs, ones_blk, tri,
                                                               tb=tb, lc=lc, tt=tt)
        y_f, y_b = _wkv_scan(ah, rh, bh, kh, bt, kt, v, pt, nb=nb, tb=tb, lc=lc)
        y_rw = _rw_readout(y_f, y_b, bonus, g, gn, i, ones_blk, tm=tr)

        mrg = _merge(y_s5, y_rw, w_proj_s5, w_proj_rw, i, zg, tm=tm, tn=1024)
        xs = _down_resid(mrg, w_out, (i,), xs, mod, 2, tb=tb, lc=lc, tm=tm, tn=1024)

        j = i // 2
        h2 = _norm_mod(xs, norm2, i, mod, 3, tb=tb, lc=lc, tm=tr)
        if i % 2 == 0:
            act = _swiglu_up(h2, ffn_w_gate, ffn_w_up, (j,), tm=tm, tn=512)
            xs = _down_resid(act, ffn_w_down, (j,), xs, mod, 5, tb=tb, lc=lc, tm=tr, tn=512)
        else:
            router_pad = jnp.pad(moe_router[j], ((0, 0), (0, LANES - N_EXPERTS)))
            bias_pad = jnp.pad(moe_router_bias[j], (0, LANES - N_EXPERTS))[None]
            route = _router_gates(xs, norm2, i, mod, router_pad, bias_pad, tb=tb, lc=lc, tm=tt)
            plan = _moe_plan(route, MOE_TILE)
            xg = _gather_rows(h2.reshape(m, d // LANES, LANES), plan["src_tok"], plan["n_act"], rows=MOE_TILE)
            act = _moe_up(xg.reshape(plan["n_rows"], d), moe_w_gate, moe_w_up, j, plan, tme=MOE_TILE, tn=256)
            y_exp = _moe_down(act, moe_w_down, j, plan, tme=MOE_TILE, tn=512)
            xs = _moe_combine(xs, y_exp, plan["pos"], mod, 5, tb=tb, lc=lc, tc=tt)

    return _final_norm(xs.reshape(nb, tb, d), norm_f[None], lc=lc, tm=tt)
```

```python
import functools
import math

import jax
import jax.numpy as jnp
from jax import lax
from jax.experimental import pallas as pl
from jax.experimental.pallas import tpu as pltpu

F32 = jnp.float32
BF16 = jnp.bfloat16

GRID_W = 64
HEADS = 16
HEAD = 64
RW = HEADS * HEAD
S5_G = 32
S5_H = 16
S5_P = 64
S5_W = S5_G * S5_H
S5_T = 16
WKV_CHUNK = 64
DECAY_LORA = 64
ICLR_LORA = 64
GATE_LORA = 160
LORA_COLS = 2 * DECAY_LORA + 2 * ICLR_LORA + GATE_LORA
LORA_PAD = 512
ZR_COLS = 3 * RW + LORA_PAD
NORM_EPS = 1e-6
GN_EPS = 64e-5
N_EXPERTS = 8
LANES = 128
LOG2_E = 1.4426950408889634
VMEM_LIMIT = 56 * 1024 * 1024


def _cp(sem):
    return pltpu.CompilerParams(dimension_semantics=sem, vmem_limit_bytes=VMEM_LIMIT)


def _sigmoid(x):
    return 0.5 * jnp.tanh(0.5 * x) + 0.5


def _gelu_tanh(x):
    return 0.5 * x * (1.0 + jnp.tanh(math.sqrt(2.0 / math.pi) * (x + 0.044715 * (x * x * x))))


def _row_mod(tab_ref, tile, tm, tb, lc):
    start = tile * tm
    b = start // tb
    pos = (start - b * tb) + lax.broadcasted_iota(jnp.int32, (tm, 1), 0)
    ctx_row = tab_ref[0:1, :]
    lat_row = tab_ref[pl.ds(1 + b, 1), :]
    return jnp.where(pos < lc, ctx_row, lat_row)


def _split_dot(x, w_bf16):
    hi = x.astype(BF16)
    lo = (x - hi.astype(F32)).astype(BF16)
    return (jnp.dot(hi, w_bf16, preferred_element_type=F32)
            + jnp.dot(lo, w_bf16, preferred_element_type=F32))


def _lead_spec(lead, shape, fn):
    return pl.BlockSpec((None,) * len(lead) + shape, lambda j, i: tuple(lead) + fn(j, i))


def _mm_body(x_ref, w_ref, *rest, has_bias):
    if has_bias:
        b_ref, o_ref, wb_ref = rest
    else:
        o_ref, wb_ref = rest

    @pl.when(pl.program_id(1) == 0)
    def _():
        wb_ref[...] = w_ref[...].astype(BF16)

    acc = jnp.dot(x_ref[...], wb_ref[...], preferred_element_type=F32)
    if has_bias:
        acc = acc + b_ref[...]
    o_ref[...] = acc.astype(o_ref.dtype)


def _matmul(x, w, lead, n_out, *, col_off=0, tm, tn, out_dtype=F32, bias=None):
    m, k = x.shape
    off = col_off // tn
    in_specs = [pl.BlockSpec((tm, k), lambda j, i: (i, 0)),
                _lead_spec(lead, (k, tn), lambda j, i: (0, j + off))]
    args = [x, w]
    if bias is not None:
        in_specs.append(pl.BlockSpec((1, tn), lambda j, i: (0, j)))
        args.append(bias)
    return pl.pallas_call(
        functools.partial(_mm_body, has_bias=bias is not None),
        grid=(n_out // tn, m // tm),
        in_specs=in_specs,
        out_specs=pl.BlockSpec((tm, tn), lambda j, i: (i, j)),
        out_shape=jax.ShapeDtypeStruct((m, n_out), out_dtype),
        scratch_shapes=[pltpu.VMEM((k, tn), BF16)],
        compiler_params=_cp(("arbitrary", "arbitrary")),
    )(*args)


def _swiglu_up_body(x_ref, wg_ref, wu_ref, o_ref, wgb_ref, wub_ref):
    @pl.when(pl.program_id(1) == 0)
    def _():
        wgb_ref[...] = wg_ref[...].astype(BF16)
        wub_ref[...] = wu_ref[...].astype(BF16)

    x = x_ref[...]
    g = jnp.dot(x, wgb_ref[...], preferred_element_type=F32)
    u = jnp.dot(x, wub_ref[...], preferred_element_type=F32)
    o_ref[...] = (g * _sigmoid(g) * u).astype(o_ref.dtype)


def _swiglu_up(x, wg, wu, lead, *, tm, tn):
    m, k = x.shape
    f = wg.shape[-1]
    wspec = _lead_spec(lead, (k, tn), lambda j, i: (0, j))
    return pl.pallas_call(
        _swiglu_up_body,
        grid=(f // tn, m // tm),
        in_specs=[pl.BlockSpec((tm, k), lambda j, i: (i, 0)), wspec, wspec],
        out_specs=pl.BlockSpec((tm, tn), lambda j, i: (i, j)),
        out_shape=jax.ShapeDtypeStruct((m, f), BF16),
        scratch_shapes=[pltpu.VMEM((k, tn), BF16), pltpu.VMEM((k, tn), BF16)],
        compiler_params=_cp(("arbitrary", "arbitrary")),
    )(x, wg, wu)


def _down_resid_body(a_ref, w_ref, x_ref, tab_ref, *rest, tm, tb, lc, expert):
    if expert is None:
        o_ref, wb_ref = rest
    else:
        rs_ref, o_ref, wb_ref = rest

    @pl.when(pl.program_id(1) == 0)
    def _():
        wb_ref[...] = w_ref[...].astype(BF16)

    y = jnp.dot(a_ref[...], wb_ref[...], preferred_element_type=F32)
    gate = _row_mod(tab_ref, pl.program_id(1), tm, tb, lc)
    if expert is not None:
        gate = gate * rs_ref[:, expert:expert + 1]
    o_ref[...] = x_ref[...] + gate * y


def _down_resid(a, w, lead, x, mod, gate_idx, *, tb, lc, tm, tn, rowscale=None, expert=None):
    m, k = a.shape
    d = x.shape[1]
    goff = gate_idx * (d // tn)
    in_specs = [pl.BlockSpec((tm, k), lambda j, i: (i, 0)),
                _lead_spec(lead, (k, tn), lambda j, i: (0, j)),
                pl.BlockSpec((tm, tn), lambda j, i: (i, j)),
                pl.BlockSpec((8, tn), lambda j, i: (0, goff + j))]
    args = [a, w, x, mod]
    if rowscale is not None:
        in_specs.append(pl.BlockSpec((tm, LANES), lambda j, i: (i, 0)))
        args.append(rowscale)
    return pl.pallas_call(
        functools.partial(_down_resid_body, tm=tm, tb=tb, lc=lc, expert=expert),
        grid=(d // tn, m // tm),
        in_specs=in_specs,
        out_specs=pl.BlockSpec((tm, tn), lambda j, i: (i, j)),
        out_shape=jax.ShapeDtypeStruct((m, d), F32),
        scratch_shapes=[pltpu.VMEM((k, tn), BF16)],
        input_output_aliases={2: 0},
        compiler_params=_cp(("arbitrary", "arbitrary")),
    )(*args)


def _merge_body(ys_ref, yr_ref, ws_ref, wr_ref, gs_ref, gr_ref, o_ref, wsb_ref, wrb_ref):
    @pl.when(pl.program_id(1) == 0)
    def _():
        wsb_ref[...] = ws_ref[...].astype(BF16)
        wrb_ref[...] = wr_ref[...].astype(BF16)

    ps = jnp.dot(ys_ref[...], wsb_ref[...], preferred_element_type=F32)
    pr = jnp.dot(yr_ref[...], wrb_ref[...], preferred_element_type=F32)
    o_ref[...] = (_sigmoid(gs_ref[...]) * ps + _sigmoid(gr_ref[...]) * pr).astype(o_ref.dtype)


def _merge(ys, yr, w_ps, w_pr, layer, zg, *, tm, tn):
    m = ys.shape[0]
    d = w_ps.shape[-1]
    nb = d // tn
    return pl.pallas_call(
        _merge_body,
        grid=(nb, m // tm),
        in_specs=[pl.BlockSpec((tm, S5_W), lambda j, i: (i, 0)),
                  pl.BlockSpec((tm, RW), lambda j, i: (i, 0)),
                  _lead_spec((layer,), (S5_W, tn), lambda j, i: (0, j)),
                  _lead_spec((layer,), (RW, tn), lambda j, i: (0, j)),
                  pl.BlockSpec((tm, tn), lambda j, i: (i, j)),
                  pl.BlockSpec((tm, tn), lambda j, i: (i, nb + j))],
        out_specs=pl.BlockSpec((tm, tn), lambda j, i: (i, j)),
        out_shape=jax.ShapeDtypeStruct((m, d), BF16),
        scratch_shapes=[pltpu.VMEM((S5_W, tn), BF16), pltpu.VMEM((RW, tn), BF16)],
        compiler_params=_cp(("arbitrary", "arbitrary")),
    )(ys, yr, w_ps, w_pr, zg, zg)


def _norm_mod_body(x_ref, gain_ref, sh_ref, sc_ref, o_ref, *, tm, tb, lc):
    x = x_ref[...]
    y = x * lax.rsqrt(jnp.mean(x * x, axis=-1, keepdims=True) + NORM_EPS) * gain_ref[...]
    i = pl.program_id(0)
    shift = _row_mod(sh_ref, i, tm, tb, lc)
    scale = _row_mod(sc_ref, i, tm, tb, lc)
    o_ref[...] = (y * (1.0 + scale) + shift).astype(o_ref.dtype)


def _norm_mod(x, gain, layer, mod, shift_idx, *, tb, lc, tm, out_dtype=BF16):
    m, d = x.shape
    return pl.pallas_call(
        functools.partial(_norm_mod_body, tm=tm, tb=tb, lc=lc),
        grid=(m // tm,),
        in_specs=[pl.BlockSpec((tm, d), lambda i: (i, 0)),
                  pl.BlockSpec((None, 1, d), lambda i: (layer, 0, 0)),
                  pl.BlockSpec((8, d), lambda i: (0, shift_idx)),
                  pl.BlockSpec((8, d), lambda i: (0, shift_idx + 1))],
        out_specs=pl.BlockSpec((tm, d), lambda i: (i, 0)),
        out_shape=jax.ShapeDtypeStruct((m, d), out_dtype),
        compiler_params=_cp(("arbitrary",)),
    )(x, gain, mod, mod)


def _final_norm_body(x_ref, gain_ref, o_ref):
    x = x_ref[...]
    o_ref[...] = x * lax.rsqrt(jnp.mean(x * x, axis=-1, keepdims=True) + NORM_EPS) * gain_ref[...]


def _final_norm(x3, gain, *, lc, tm):
    b, tb, d = x3.shape
    l = tb - lc
    skip = lc // tm
    return pl.pallas_call(
        _final_norm_body,
        grid=(b, l // tm),
        in_specs=[pl.BlockSpec((None, tm, d), lambda bi, t: (bi, skip + t, 0)),
                  pl.BlockSpec((1, d), lambda bi, t: (0, 0))],
        out_specs=pl.BlockSpec((None, tm, d), lambda bi, t: (bi, t, 0)),
        out_shape=jax.ShapeDtypeStruct((b, l, d), F32),
        compiler_params=_cp(("arbitrary", "arbitrary")),
    )(x3, gain)


def _router_body(x_ref, gain_ref, sh_ref, sc_ref, w_ref, b_ref, o_ref, *, tm, tb, lc):
    x = x_ref[...]
    y = x * lax.rsqrt(jnp.mean(x * x, axis=-1, keepdims=True) + NORM_EPS) * gain_ref[...]
    i = pl.program_id(0)
    h = y * (1.0 + _row_mod(sc_ref, i, tm, tb, lc)) + _row_mod(sh_ref, i, tm, tb, lc)
    logits = jnp.dot(h, w_ref[...], preferred_element_type=F32,
                     precision=lax.Precision.HIGHEST) + b_ref[...]
    lane = lax.broadcasted_iota(jnp.int32, logits.shape, 1).astype(F32)
    neg = jnp.float32(-jnp.inf)
    logits = jnp.where(lane < N_EXPERTS, logits, neg)
    v1 = jnp.max(logits, axis=-1, keepdims=True)
    i1 = jnp.min(jnp.where(logits == v1, lane, float(LANES)), axis=-1, keepdims=True)
    rest = jnp.where(lane == i1, neg, logits)
    v2 = jnp.max(rest, axis=-1, keepdims=True)
    i2 = jnp.min(jnp.where(rest == v2, lane, float(LANES)), axis=-1, keepdims=True)
    e2 = jnp.exp(v2 - v1)
    w1 = 1.0 / (1.0 + e2)
    w2 = e2 / (1.0 + e2)
    o_ref[...] = (jnp.where(lane == i1, w1, 0.0) + jnp.where(lane == i2, w2, 0.0)
                  + jnp.where(lane == N_EXPERTS, i1, 0.0) + jnp.where(lane == N_EXPERTS + 1, i2, 0.0)
                  + jnp.where(lane == N_EXPERTS + 2, w1, 0.0) + jnp.where(lane == N_EXPERTS + 3, w2, 0.0))


def _router_gates(x, gain, layer, mod, router_pad, bias_pad, *, tb, lc, tm):
    m, d = x.shape
    return pl.pallas_call(
        functools.partial(_router_body, tm=tm, tb=tb, lc=lc),
        grid=(m // tm,),
        in_specs=[pl.BlockSpec((tm, d), lambda i: (i, 0)),
                  pl.BlockSpec((None, 1, d), lambda i: (layer, 0, 0)),
                  pl.BlockSpec((8, d), lambda i: (0, 3)),
                  pl.BlockSpec((8, d), lambda i: (0, 4)),
                  pl.BlockSpec((d, LANES), lambda i: (0, 0)),
                  pl.BlockSpec((1, LANES), lambda i: (0, 0))],
        out_specs=pl.BlockSpec((tm, LANES), lambda i: (i, 0)),
        out_shape=jax.ShapeDtypeStruct((m, LANES), F32),
        compiler_params=_cp(("arbitrary",)),
    )(x, gain, mod, mod, router_pad, bias_pad)


def _s5_tables(lam_re, lam_im, log_step, b_re, b_im, c_re, c_im):
    hp = lax.Precision.HIGHEST
    t = S5_T
    lam_re, lam_im = lam_re.astype(F32), lam_im.astype(F32)
    b_re, b_im = b_re.astype(F32), b_im.astype(F32)
    c_re, c_im = c_re.astype(F32), c_im.astype(F32)
    step = jnp.exp(log_step.astype(F32))[..., None]
    tau = jnp.arange(t + 1, dtype=F32)[None, :, None, None]
    mag = jnp.exp((lam_re * step)[:, None] * tau)
    ang = (lam_im * step)[:, None] * tau
    pw_re, pw_im = mag * jnp.cos(ang), mag * jnp.sin(ang)
    nr, ni = pw_re[:, 1] - 1.0, pw_im[:, 1]
    den = lam_re * lam_re + lam_im * lam_im
    q_re = (nr * lam_re + ni * lam_im) / den
    q_im = (ni * lam_re - nr * lam_im) / den
    bb_re = q_re[..., None] * b_re - q_im[..., None] * b_im
    bb_im = q_re[..., None] * b_im + q_im[..., None] * b_re

    def cmul(ar, ai, br, bi):
        return ar * br - ai * bi, ar * bi + ai * br

    x_re, x_im = cmul(pw_re[:, :t, :, :, None], pw_im[:, :t, :, :, None], bb_re[:, None], bb_im[:, None])
    kd = (jnp.einsum('ghp,dtgpk->dtghk', c_re, x_re, precision=hp)
          - jnp.einsum('ghp,dtgpk->dtghk', c_im, x_im, precision=hp))
    kf, kb = kd[0], kd[1]
    kall = jnp.concatenate([kb[1:][::-1], (kf[0] + kb[0])[None], kf[1:]], axis=0).astype(BF16)
    idx = (jnp.arange(t)[None, :] - jnp.arange(t)[:, None]) + (t - 1)
    kbig = jnp.transpose(kall[idx], (2, 0, 4, 1, 3)).reshape(S5_G, t * S5_H, t * S5_H)

    def chunk_in(d, powers):
        gr, gi = cmul(pw_re[d][powers][..., None], pw_im[d][powers][..., None], bb_re[d][None], bb_im[d][None])
        pack = lambda g: jnp.transpose(g, (1, 0, 3, 2)).reshape(S5_G, t * S5_H, S5_P)
        return pack(gr), pack(gi)

    def chunk_out(d, powers):
        cr, ci = cmul(c_re[None], c_im[None], pw_re[d][powers][:, :, None, :], pw_im[d][powers][:, :, None, :])
        pack = lambda g: jnp.transpose(g, (1, 3, 0, 2)).reshape(S5_G, S5_P, t * S5_H)
        return pack(cr), -pack(ci)

    gin = jnp.stack(chunk_in(0, t - 1 - jnp.arange(t)) + chunk_in(1, jnp.arange(t)), axis=1)
    cout = jnp.stack(chunk_out(0, 1 + jnp.arange(t)) + chunk_out(1, t - jnp.arange(t)), axis=1)
    lam_t = jnp.stack([pw_re[0, t], pw_im[0, t], pw_re[1, t], pw_im[1, t]], axis=1)
    return kbig, gin.astype(BF16), cout.astype(BF16), lam_t[:, :, None, :]


S5_SG = LANES // S5_H


def _s5_body(u_ref, k_ref, gin_ref, cout_ref, lam_ref, o_ref, sel_s, g_s, h_s, *, nb, ncb, ncc):
    t = S5_T
    nc = nb * ncb
    tw = t * S5_H

    @pl.when(pl.program_id(0) == 0)
    def _():
        row = lax.broadcasted_iota(jnp.int32, (t * LANES, tw), 0)
        col = lax.broadcasted_iota(jnp.int32, (t * LANES, tw), 1)
        same_s = (row // LANES) == (col // S5_H)
        lane_off = (row % LANES) - (col % S5_H)
        for g in range(S5_SG):
            sel_s[g] = jnp.where(jnp.logical_and(same_s, lane_off == g * S5_H), 1.0, 0.0).astype(BF16)

    ucat = jnp.concatenate([u_ref[pl.ds(s, nc, stride=t), :] for s in range(t)], axis=1).astype(BF16)
    ug = []
    for g in range(S5_SG):
        ug.append(jnp.dot(ucat, sel_s[g], preferred_element_type=F32).astype(BF16))
        for q in range(4):
            g_s[g, q] = jnp.dot(ug[g], gin_ref[g, q], preferred_element_type=F32)
    nlc = ncb - ncc

    def step(n, carry):
        out = []
        nb_idx = jnp.where(n < ncc, ncc - 1 - n, ncc + (nlc - 1) - (n - ncc))
        for g in range(S5_SG):
            lfr, lfi, lbr, lbi = lam_ref[g, 0], lam_ref[g, 1], lam_ref[g, 2], lam_ref[g, 3]
            for b in range(nb):
                k = 4 * (g * nb + b)
                fr, fi, br, bi = carry[k:k + 4]
                rf = b * ncb + n
                rb = b * ncb + nb_idx
                h_s[g, 0, pl.ds(rf, 1), :] = fr
                h_s[g, 1, pl.ds(rf, 1), :] = fi
                h_s[g, 2, pl.ds(rb, 1), :] = br
                h_s[g, 3, pl.ds(rb, 1), :] = bi
                out += [lfr * fr - lfi * fi + g_s[g, 0, pl.ds(rf, 1), :],
                        lfr * fi + lfi * fr + g_s[g, 1, pl.ds(rf, 1), :],
                        lbr * br - lbi * bi + g_s[g, 2, pl.ds(rb, 1), :],
                        lbr * bi + lbi * br + g_s[g, 3, pl.ds(rb, 1), :]]
        return tuple(out)

    zero = jnp.zeros((1, S5_P), F32)
    lax.fori_loop(0, ncb, step, (zero,) * (4 * nb * S5_SG))
    ycat = jnp.zeros((nc, t * LANES), F32)
    for g in range(S5_SG):
        y = jnp.dot(ug[g], k_ref[g], preferred_element_type=F32)
        for q in range(4):
            y = y + jnp.dot(h_s[g, q].astype(BF16), cout_ref[g, q], preferred_element_type=F32)
        hi = y.astype(BF16)
        lo = (y - hi.astype(F32)).astype(BF16)
        ycat = ycat + _dot_nt(hi, sel_s[g]) + _dot_nt(lo, sel_s[g])
    for s in range(t):
        o_ref[pl.ds(s, nc, stride=t), :] = ycat[:, s * LANES:(s + 1) * LANES]


def _s5_mix(u, tables, layer, *, nb, ncb, ncc):
    kbig, gin, cout, lam_t = tables
    m = u.shape[0]
    nc = nb * ncb
    tw = S5_T * S5_H
    return pl.pallas_call(
        functools.partial(_s5_body, nb=nb, ncb=ncb, ncc=ncc),
        grid=(S5_G // S5_SG,),
        in_specs=[pl.BlockSpec((m, LANES), lambda i: (0, i)),
                  pl.BlockSpec((None, S5_SG, tw, tw), lambda i: (layer, i, 0, 0)),
                  pl.BlockSpec((None, S5_SG, 4, tw, S5_P), lambda i: (layer, i, 0, 0, 0)),
                  pl.BlockSpec((None, S5_SG, 4, S5_P, tw), lambda i: (layer, i, 0, 0, 0)),
                  pl.BlockSpec((None, S5_SG, 4, 1, S5_P), lambda i: (layer, i, 0, 0, 0))],
        out_specs=pl.BlockSpec((m, LANES), lambda i: (0, i)),
        out_shape=jax.ShapeDtypeStruct((m, S5_W), F32),
        scratch_shapes=[pltpu.VMEM((S5_SG, S5_T * LANES, tw), BF16),
                        pltpu.VMEM((S5_SG, 4, nc, S5_P), F32), pltpu.VMEM((S5_SG, 4, nc, S5_P), F32)],
        compiler_params=_cp(("arbitrary",)),
    )(u, kbig, gin, cout, lam_t)


def _s5_readout_body(y_ref, u_ref, d_ref, w_ref, o_ref, wb_ref):
    @pl.when(pl.program_id(0) == 0)
    def _():
        wb_ref[...] = w_ref[...].astype(BF16)

    y = _gelu_tanh(y_ref[...] + d_ref[...] * u_ref[...])
    z = jnp.dot(y.astype(BF16), wb_ref[...], preferred_element_type=F32)
    o_ref[...] = (y * _sigmoid(z)).astype(o_ref.dtype)


def _s5_readout(y, u, d_skip, w_glu, layer, *, tm):
    m = y.shape[0]
    return pl.pallas_call(
        _s5_readout_body,
        grid=(m // tm,),
        in_specs=[pl.BlockSpec((tm, S5_W), lambda i: (i, 0)),
                  pl.BlockSpec((tm, S5_W), lambda i: (i, 0)),
                  pl.BlockSpec((None, 1, S5_W), lambda i: (layer, 0, 0)),
                  pl.BlockSpec((None, S5_W, S5_W), lambda i: (layer, 0, 0))],
        out_specs=pl.BlockSpec((tm, S5_W), lambda i: (i, 0)),
        out_shape=jax.ShapeDtypeStruct((m, S5_W), BF16),
        scratch_shapes=[pltpu.VMEM((S5_W, S5_W), BF16)],
        compiler_params=_cp(("arbitrary",)),
    )(y, u, d_skip, w_glu)


def _rw_feat_body(z_ref, up_ref, dn_ref, mu_ref, wup_ref, aup_ref, gup_ref, vec_ref, ones_ref, tri_ref,
                  v_ref, g_ref, bonus_ref, ah_ref, rh_ref, bh_ref, kh_ref, bt_ref, kt_ref, pt_ref,
                  *, tt, tb, lc):
    i = pl.program_id(0)
    start = i * tt
    b = start // tb
    pos = (start - b * tb) + lax.broadcasted_iota(jnp.int32, (tt, 1), 0)
    is_ctx = pos < lc
    tl = pos - lc
    col = tl & (GRID_W - 1)
    l_lat = tb - lc

    z = z_ref[...]
    prev = pltpu.roll(z, 1, axis=0)
    nxt = pltpu.roll(z, tt - 1, axis=0)
    if tt > GRID_W:
        up = jnp.concatenate([up_ref[...], z[:tt - GRID_W]], axis=0)
        down = jnp.concatenate([z[GRID_W:], dn_ref[...]], axis=0)
    else:
        up, down = up_ref[...], dn_ref[...]
    m_prev = jnp.logical_or(jnp.logical_and(is_ctx, pos >= 1), jnp.logical_and(tl >= 0, col >= 1))
    m_next = jnp.logical_or(jnp.logical_and(is_ctx, pos <= lc - 2),
                            jnp.logical_and(tl >= 0, col <= GRID_W - 2))
    m_up = tl >= GRID_W
    m_down = jnp.logical_and(tl >= 0, tl < l_lat - GRID_W)
    prev = jnp.where(m_prev, prev, 0.0)
    nxt = jnp.where(m_next, nxt, 0.0)
    up = jnp.where(is_ctx, prev, jnp.where(m_up, up, 0.0))
    down = jnp.where(is_ctx, nxt, jnp.where(m_down, down, 0.0))
    l4 = lax.broadcasted_iota(jnp.int32, (1, ZR_COLS), 1) & 3
    shifted = jnp.where(l4 == 0, prev, jnp.where(l4 == 1, nxt, jnp.where(l4 == 2, up, down)))
    z = z + mu_ref[...] * (shifted - z)

    r = z[:, 0:RW]
    k = z[:, RW:2 * RW]
    v = z[:, 2 * RW:3 * RW]
    lora_w = jnp.tanh(z[:, 3 * RW:3 * RW + LANES]).astype(BF16)
    lora_a = z[:, 3 * RW + LANES:3 * RW + 2 * LANES].astype(BF16)
    lora_g = _sigmoid(z[:, 3 * RW + 2 * LANES:3 * RW + 4 * LANES]).astype(BF16)
    k_k, k_a, r_k = vec_ref[0:1, :], vec_ref[1:2, :], vec_ref[2:3, :]
    ones = ones_ref[...]

    kk = k * k_k
    kk = kk * lax.rsqrt(jnp.maximum(_split_dot(kk * kk, ones), 1e-24))
    ksum = jnp.zeros_like(k)
    for d in range(2):
        w0 = vec_ref[3 + d:4 + d, :]
        a0 = vec_ref[5 + d:6 + d, :]
        wl = w0 + jnp.dot(lora_w, wup_ref[d], preferred_element_type=F32)
        lw = -math.exp(-0.5) * _sigmoid(wl)
        lw2 = lw * LOG2_E
        a = _sigmoid(a0 + jnp.dot(lora_a, aup_ref[d], preferred_element_type=F32))
        k_d = k * (1.0 + (a - 1.0) * k_a)
        bb = kk * a
        ksum = ksum + k_d
        lw_hi = lw2.astype(BF16)
        lw_lo = (lw2 - lw_hi.astype(F32)).astype(BF16)
        cl = (jnp.dot(tri_ref[d, 0], lw_hi, preferred_element_type=F32)
              + jnp.dot(tri_ref[d, 0], lw_lo, preferred_element_type=F32))
        rem = (jnp.dot(tri_ref[d, 1], lw_hi, preferred_element_type=F32)
               + jnp.dot(tri_ref[d, 1], lw_lo, preferred_element_type=F32))
        p_in = jnp.exp2(cl)
        p_inv = jnp.exp2(-cl)
        p_rem = jnp.exp2(rem)
        ah_ref[d] = (-kk * jnp.exp2(cl - lw2)).astype(BF16)
        rh_ref[d] = (r * p_in).astype(BF16)
        bh_ref[d] = (bb * p_inv).astype(BF16)
        kh_ref[d] = (k_d * p_inv).astype(BF16)
        bt_ref[d] = (bb * p_rem).astype(BF16)
        kt_ref[d] = (k_d * p_rem).astype(BF16)
        last = 0 if d == 1 else WKV_CHUNK - 1
        for cidx in range(tt // WKV_CHUNK):
            row = cidx * WKV_CHUNK + last
            pt_ref[d, cidx] = p_in[row:row + 1, :]
    v_ref[...] = v.astype(BF16)
    g_ref[...] = jnp.dot(lora_g, gup_ref[...], preferred_element_type=F32)
    bonus_ref[...] = _split_dot(r * ksum * r_k, ones) * v


def _rw_features(zr, layer, mu_pad, wup, aup, gup, vecs, ones_blk, tri, *, tb, lc, tt):
    m = zr.shape[0]
    nh = tt // GRID_W
    nblk64 = m // GRID_W
    ncht = tt // WKV_CHUNK
    full = pl.BlockSpec((tt, RW), lambda i: (i, 0))
    per_dir = pl.BlockSpec((2, tt, RW), lambda i: (0, i, 0))
    one = jax.ShapeDtypeStruct((m, RW), F32)
    two = jax.ShapeDtypeStruct((2, m, RW), BF16)
    return pl.pallas_call(
        functools.partial(_rw_feat_body, tt=tt, tb=tb, lc=lc),
        grid=(m // tt,),
        in_specs=[pl.BlockSpec((tt, ZR_COLS), lambda i: (i, 0)),
                  pl.BlockSpec((GRID_W, ZR_COLS), lambda i: (jnp.maximum(i * nh - 1, 0), 0)),
                  pl.BlockSpec((GRID_W, ZR_COLS), lambda i: (jnp.minimum((i + 1) * nh, nblk64 - 1), 0)),
                  pl.BlockSpec((None, 1, ZR_COLS), lambda i: (layer, 0, 0)),
                  pl.BlockSpec((None, 2, LANES, RW), lambda i: (layer, 0, 0, 0)),
                  pl.BlockSpec((None, 2, LANES, RW), lambda i: (layer, 0, 0, 0)),
                  pl.BlockSpec((None, 2 * LANES, RW), lambda i: (layer, 0, 0)),
                  pl.BlockSpec((None, 8, RW), lambda i: (layer, 0, 0)),
                  pl.BlockSpec((RW, RW), lambda i: (0, 0)),
                  pl.BlockSpec((2, 2, tt, tt), lambda i: (0, 0, 0, 0))],
        out_specs=[full, full, full, per_dir, per_dir, per_dir, per_dir, per_dir, per_dir,
                   pl.BlockSpec((2, ncht, 1, RW), lambda i: (0, i, 0, 0))],
        out_shape=[jax.ShapeDtypeStruct((m, RW), BF16), one, one, two, two, two, two, two, two,
                   jax.ShapeDtypeStruct((2, m // WKV_CHUNK, 1, RW), F32)],
        compiler_params=_cp(("arbitrary",)),
    )(zr, zr, zr, mu_pad, wup, aup, gup, vecs, ones_blk, tri)


def _chunk_tri(tt):
    t = jnp.arange(tt)[:, None]
    s = jnp.arange(tt)[None, :]
    same = (t // WKV_CHUNK) == (s // WKV_CHUNK)
    fwd = jnp.stack([same & (s <= t), same & (s > t)])
    bwd = jnp.stack([same & (s >= t), same & (s < t)])
    return jnp.stack([fwd, bwd]).astype(BF16)


def _dot_nt(a, b):
    return lax.dot_general(a, b, (((1,), (1,)), ((), ())), preferred_element_type=F32)


def _dot_tn(a, b):
    return lax.dot_general(a, b, (((0,), (0,)), ((), ())), preferred_element_type=F32)


def _wkv_body(*refs):
    c = WKV_CHUNK
    ins = (refs[0:8], refs[8:16])
    y_refs = refs[16:18]
    s_ref = refs[18]

    @pl.when(pl.program_id(1) == 0)
    def _():
        s_ref[...] = jnp.zeros_like(s_ref)

    t_idx = lax.broadcasted_iota(jnp.int32, (2 * c, LANES), 0)
    lane = lax.broadcasted_iota(jnp.int32, (2 * c, LANES), 1)
    diff = (t_idx & (c - 1)) - (lane & (c - 1))
    tri = (jnp.where(t_idx < c, diff, diff + 1) > 0, jnp.where(t_idx < c, -diff, 1 - diff) > 0)
    same_head = (t_idx // c) == (lane // c)
    lo = lax.broadcasted_iota(jnp.int32, (c, LANES), 1) < c

    def bd(x):
        zero = jnp.zeros_like(x)
        return jnp.concatenate([jnp.where(lo, x, zero), jnp.where(lo, zero, x)], axis=0)

    chains = [(d, p) for d in range(2) for p in range(HEADS // 2)]
    n = range(len(chains))
    sl = [slice(p * LANES, (p + 1) * LANES) for _, p in chains]
    ah, rh, bh, kh, bt, kt, vv, pt = ([ins[d][q] for d, _ in chains] for q in range(8))
    ar = [jnp.concatenate([ah[i][:, sl[i]], rh[i][:, sl[i]]], axis=0) for i in n]
    s_old = [s_ref[d, p] for d, p in chains]
    pbk = [_dot_nt(ar[i], jnp.concatenate([bd(bh[i][:, sl[i]]), bd(kh[i][:, sl[i]])], axis=0)) for i in n]
    ps = [_dot_nt(ar[i], s_old[i].astype(BF16)) for i in n]
    pb = [jnp.where(tri[chains[i][0]], pbk[i][:, :LANES], 0.0) for i in n]
    pk = [jnp.where(tri[chains[i][0]], pbk[i][:, LANES:], 0.0) for i in n]
    bdv = [bd(vv[i][:, sl[i]]) for i in n]
    x = [ps[i][:c] + jnp.dot(pk[i][:c].astype(BF16), bdv[i], preferred_element_type=F32) for i in n]
    lp = [pb[i][:c].astype(BF16) for i in n]
    for it in range(6):
        if it < 5:
            prod = [jnp.dot(lp[i], jnp.concatenate([bd(x[i].astype(BF16)), bd(lp[i])], axis=1),
                            preferred_element_type=F32) for i in n]
            x = [x[i] + prod[i][:, :LANES] for i in n]
            lp = [prod[i][:, LANES:].astype(BF16) for i in n]
        else:
            x = [x[i] + jnp.dot(lp[i], bd(x[i].astype(BF16)), preferred_element_type=F32) for i in n]
    u = [x[i].astype(BF16) for i in n]
    for i in n:
        y_refs[chains[i][0]][:, sl[i]] = ps[i][c:] + jnp.dot(
            jnp.concatenate([pb[i][c:], pk[i][c:]], axis=1).astype(BF16),
            jnp.concatenate([bd(u[i]), bdv[i]], axis=0), preferred_element_type=F32)
    upd = [_dot_tn(jnp.concatenate([u[i], vv[i][:, sl[i]]], axis=0),
                   jnp.concatenate([bt[i][:, sl[i]], kt[i][:, sl[i]]], axis=0)) for i in n]
    for i, (d, p) in enumerate(chains):
        s_ref[d, p] = s_old[i] * pt[i][:, sl[i]] + jnp.where(same_head, upd[i], 0.0)


def _wkv_scan(ah, rh, bh, kh, bt, kt, v, pt, *, nb, tb, lc):
    m = v.shape[0]
    c = WKV_CHUNK
    ncb, ncc = tb // c, lc // c
    nlc = ncb - ncc

    def chunk(d, b, n):
        if d == 0:
            return b * ncb + n
        return b * ncb + jnp.where(n < ncc, ncc - 1 - n, ncc + (nlc - 1) - (n - ncc))

    def specs(d):
        per_dir = pl.BlockSpec((None, c, RW), lambda b, n: (d, chunk(d, b, n), 0))
        return [per_dir] * 6 + [pl.BlockSpec((c, RW), lambda b, n: (chunk(d, b, n), 0)),
                                pl.BlockSpec((None, None, 1, RW), lambda b, n: (d, chunk(d, b, n), 0, 0))]

    out = jax.ShapeDtypeStruct((m, RW), F32)
    args = (ah, rh, bh, kh, bt, kt, v, pt)
    return pl.pallas_call(
        _wkv_body,
        grid=(nb, ncb),
        in_specs=specs(0) + specs(1),
        out_specs=[pl.BlockSpec((c, RW), lambda b, n: (chunk(0, b, n), 0)),
                   pl.BlockSpec((c, RW), lambda b, n: (chunk(1, b, n), 0))],
        out_shape=[out, out],
        scratch_shapes=[pltpu.VMEM((2, HEADS // 2, LANES, LANES), F32)],
        compiler_params=_cp(("arbitrary", "arbitrary")),
    )(*args, *args)


def _rw_readout_body(yf_ref, yb_ref, bonus_ref, g_ref, gn_ref, ones_ref, o_ref):
    y = yf_ref[...] + yb_ref[...]
    ones = ones_ref[...]
    mean = _split_dot(y, ones) * (1.0 / HEAD)
    yc = y - mean
    var = _split_dot(yc * yc, ones) * (1.0 / HEAD)
    yn = yc * lax.rsqrt(var + GN_EPS) * gn_ref[0:1, :] + gn_ref[1:2, :]
    o_ref[...] = ((yn + bonus_ref[...]) * g_ref[...]).astype(o_ref.dtype)


def _rw_readout(y_f, y_b, bonus, g, gn, layer, ones_blk, *, tm):
    m = y_f.shape[0]
    full = pl.BlockSpec((tm, RW), lambda i: (i, 0))
    return pl.pallas_call(
        _rw_readout_body,
        grid=(m // tm,),
        in_specs=[full, full, full, full,
                  pl.BlockSpec((None, 8, RW), lambda i: (layer, 0, 0)),
                  pl.BlockSpec((RW, RW), lambda i: (0, 0))],
        out_specs=full,
        out_shape=jax.ShapeDtypeStruct((m, RW), BF16),
        compiler_params=_cp(("arbitrary",)),
    )(y_f, y_b, bonus, g, gn, ones_blk)


MOE_TILE = 1024


def _moe_plan(route, tme):
    m = route.shape[0]
    n_asg = 2 * m
    idx = route[:, N_EXPERTS:N_EXPERTS + 2].astype(jnp.int32)
    e_flat = idx.T.reshape(n_asg)
    w_flat = route[:, N_EXPERTS + 2:N_EXPERTS + 4].T.reshape(n_asg)
    tok_flat = jnp.tile(jnp.arange(m, dtype=jnp.int32), 2)
    order = jnp.argsort(e_flat, stable=True).astype(jnp.int32)
    e_sorted, tok_sorted, w_sorted = e_flat[order], tok_flat[order], w_flat[order]
    experts = jnp.arange(N_EXPERTS, dtype=jnp.int32)
    cnt = jnp.sum((e_flat[:, None] == experts[None, :]).astype(jnp.int32), axis=0)
    start = jnp.cumsum(cnt) - cnt
    padded = ((cnt + tme - 1) // tme) * tme
    off_end = jnp.cumsum(padded)
    off = off_end - padded
    n_rows = ((n_asg + tme - 1) // tme) * tme + N_EXPERTS * tme
    n_tiles = n_rows // tme
    tile_e = jnp.sum((jnp.arange(n_tiles, dtype=jnp.int32)[:, None] * tme >= off_end[None, :]).astype(jnp.int32),
                     axis=1)
    tile_e = jnp.minimum(tile_e, N_EXPERTS - 1)
    first = jnp.concatenate([jnp.ones((1,), jnp.int32), (tile_e[1:] != tile_e[:-1]).astype(jnp.int32)])
    n_act = (off_end[-1] // tme).reshape(1)
    e_row = jnp.repeat(tile_e, tme)
    q = jnp.arange(n_rows, dtype=jnp.int32) - off[e_row]
    valid = q < cnt[e_row]
    s_row = jnp.clip(start[e_row] + q, 0, n_asg - 1)
    src_tok = jnp.where(valid, tok_sorted[s_row], 0)
    row_w = jnp.where(valid, w_sorted[s_row], 0.0)[:, None]
    dest_sorted = off[e_sorted] + (jnp.arange(n_asg, dtype=jnp.int32) - start[e_sorted])
    pos = jnp.zeros((n_asg,), jnp.int32).at[order].set(dest_sorted, unique_indices=True)
    return dict(src_tok=src_tok, row_w=row_w, pos=pos, tile_e=tile_e, first=first, n_act=n_act,
                n_rows=n_rows, n_tiles=n_tiles)


def _gather_body(idx_ref, nact_ref, src_ref, o_ref, sem, *, rows):
    i = pl.program_id(0)
    base = i * rows

    @pl.when(i < nact_ref[0])
    def _():
        def issue(r, carry):
            pltpu.make_async_copy(src_ref.at[idx_ref[base + r]], o_ref.at[r], sem).start()
            return carry

        lax.fori_loop(0, rows, issue, 0, unroll=8)
        pltpu.make_async_copy(o_ref, o_ref, sem).wait()

    @pl.when(i >= nact_ref[0])
    def _():
        o_ref[...] = jnp.zeros_like(o_ref)


def _gather_rows(src3, idx, n_act, *, rows):
    n_rows = idx.shape[0]
    blk = (rows,) + src3.shape[1:]
    return pl.pallas_call(
        functools.partial(_gather_body, rows=rows),
        grid_spec=pltpu.PrefetchScalarGridSpec(
            num_scalar_prefetch=2, grid=(n_rows // rows,),
            in_specs=[pl.BlockSpec(memory_space=pl.ANY)],
            out_specs=pl.BlockSpec(blk, lambda i, idx, na: (i, 0, 0)),
            scratch_shapes=[pltpu.SemaphoreType.DMA(())]),
        out_shape=jax.ShapeDtypeStruct((n_rows,) + src3.shape[1:], src3.dtype),
        compiler_params=_cp(("arbitrary",)),
    )(idx, n_act, src3)


def _moe_up_body(te_ref, first_ref, nact_ref, x_ref, wg_ref, wu_ref, o_ref, wgb_ref, wub_ref):
    i = pl.program_id(1)

    @pl.when(i < nact_ref[0])
    def _():
        @pl.when(first_ref[i] == 1)
        def _():
            wgb_ref[...] = wg_ref[...].astype(BF16)
            wub_ref[...] = wu_ref[...].astype(BF16)

        x = x_ref[...]
        g = jnp.dot(x, wgb_ref[...], preferred_element_type=F32)
        u = jnp.dot(x, wub_ref[...], preferred_element_type=F32)
        o_ref[...] = (g * _sigmoid(g) * u).astype(o_ref.dtype)

    @pl.when(i >= nact_ref[0])
    def _():
        o_ref[...] = jnp.zeros_like(o_ref)


def _moe_up(xg, wg, wu, layer, plan, *, tme, tn):
    p, k = xg.shape
    f = wg.shape[-1]

    def row(i, nact):
        return jnp.minimum(i, nact[0] - 1)

    wspec = pl.BlockSpec((None, None, k, tn), lambda j, i, te, fi, na: (layer, te[row(i, na)], 0, j))
    return pl.pallas_call(
        _moe_up_body,
        grid_spec=pltpu.PrefetchScalarGridSpec(
            num_scalar_prefetch=3, grid=(f // tn, p // tme),
            in_specs=[pl.BlockSpec((tme, k), lambda j, i, te, fi, na: (row(i, na), 0)), wspec, wspec],
            out_specs=pl.BlockSpec((tme, tn), lambda j, i, te, fi, na: (i, j)),
            scratch_shapes=[pltpu.VMEM((k, tn), BF16), pltpu.VMEM((k, tn), BF16)]),
        out_shape=jax.ShapeDtypeStruct((p, f), BF16),
        compiler_params=_cp(("arbitrary", "arbitrary")),
    )(plan["tile_e"], plan["first"], plan["n_act"], xg, wg, wu)


def _moe_down_body(te_ref, first_ref, nact_ref, a_ref, w_ref, rw_ref, o_ref, wb_ref):
    i = pl.program_id(1)

    @pl.when(i < nact_ref[0])
    def _():
        @pl.when(first_ref[i] == 1)
        def _():
            wb_ref[...] = w_ref[...].astype(BF16)

        o_ref[...] = rw_ref[...] * jnp.dot(a_ref[...], wb_ref[...], preferred_element_type=F32)

    @pl.when(i >= nact_ref[0])
    def _():
        o_ref[...] = jnp.zeros_like(o_ref)


def _moe_down(act, wd, layer, plan, *, tme, tn):
    p, k = act.shape
    d = wd.shape[-1]

    def row(i, nact):
        return jnp.minimum(i, nact[0] - 1)

    return pl.pallas_call(
        _moe_down_body,
        grid_spec=pltpu.PrefetchScalarGridSpec(
            num_scalar_prefetch=3, grid=(d // tn, p // tme),
            in_specs=[pl.BlockSpec((tme, k), lambda j, i, te, fi, na: (row(i, na), 0)),
                      pl.BlockSpec((None, None, k, tn), lambda j, i, te, fi, na: (layer, te[row(i, na)], 0, j)),
                      pl.BlockSpec((tme, 1), lambda j, i, te, fi, na: (row(i, na), 0))],
            out_specs=pl.BlockSpec((tme, tn), lambda j, i, te, fi, na: (i, j)),
            scratch_shapes=[pltpu.VMEM((k, tn), BF16)]),
        out_shape=jax.ShapeDtypeStruct((p, d), F32),
        compiler_params=_cp(("arbitrary", "arbitrary")),
    )(plan["tile_e"], plan["first"], plan["n_act"], act, wd, plan["row_w"])


def _moe_combine_body(pos_ref, x_ref, y_ref, tab_ref, o_ref, buf, sem, *, tc, m, tb, lc):
    i = pl.program_id(0)
    base = i * tc

    def issue(r, carry):
        pltpu.make_async_copy(y_ref.at[pos_ref[base + r]], buf.at[0, r], sem).start()
        pltpu.make_async_copy(y_ref.at[pos_ref[m + base + r]], buf.at[1, r], sem).start()
        return carry

    lax.fori_loop(0, tc, issue, 0)
    pltpu.make_async_copy(buf, buf, sem).wait()
    gate = _row_mod(tab_ref, i, tc, tb, lc)
    o_ref[...] = x_ref[...] + gate * (buf[0] + buf[1])


def _moe_combine(x, y, pos, mod, gate_idx, *, tb, lc, tc):
    m, d = x.shape
    return pl.pallas_call(
        functools.partial(_moe_combine_body, tc=tc, m=m, tb=tb, lc=lc),
        grid_spec=pltpu.PrefetchScalarGridSpec(
            num_scalar_prefetch=1, grid=(m // tc,),
            in_specs=[pl.BlockSpec((tc, d), lambda i, pos: (i, 0)),
                      pl.BlockSpec(memory_space=pl.ANY),
                      pl.BlockSpec((8, d), lambda i, pos: (0, gate_idx))],
            out_specs=pl.BlockSpec((tc, d), lambda i, pos: (i, 0)),
            scratch_shapes=[pltpu.VMEM((2, tc, d), F32), pltpu.SemaphoreType.DMA(())]),
        out_shape=jax.ShapeDtypeStruct((m, d), F32),
        input_output_aliases={1: 0},
        compiler_params=_cp(("arbitrary",)),
    )(pos, x, y, mod)


def _pad_rows(x, rows):
    return jnp.concatenate([x, jnp.zeros((rows - x.shape[0],) + x.shape[1:], x.dtype)], axis=0)


def _pick_tile(n, target):
    best = 8
    for t in range(8, target + 1, 8):
        if n % t == 0:
            best = t
    return best


def kernel(x, c, ctx, c_ctx, w_ada, b_ada, norm1, norm2, norm_f, w_in, s5_lambda_re, s5_lambda_im, s5_log_step, s5_b_re, s5_b_im, s5_c_re, s5_c_im, s5_d, s5_w_glu, rw_mu, rw_w0, rw_w_up, rw_a0, rw_a_up, rw_g_up, rw_k_k, rw_k_a, rw_r_k, rw_gn_w, rw_gn_b, w_proj_s5, w_proj_rw, w_out, ffn_w_gate, ffn_w_up, ffn_w_down, moe_router, moe_router_bias, moe_w_gate, moe_w_up, moe_w_down):
    nb, l, d = x.shape
    lc = ctx.shape[1]
    tb = lc + l
    m = nb * tb
    depth = w_in.shape[0]
    tt = lc
    assert lc % GRID_W == 0 and l % tt == 0 and l % GRID_W == 0 and nb * HEADS * 4 == LANES
    assert tb % S5_T == 0 and lc % S5_T == 0 and tt % WKV_CHUNK == 0
    tm = _pick_tile(tb, 1088)
    tr = _pick_tile(tb, 544)
    ncb, ncc = tb // S5_T, lc // S5_T

    xs = jnp.concatenate([ctx, x], axis=1).reshape(m, d)
    silu_rows = _pad_rows(jnp.concatenate([c_ctx[None], c], axis=0), 8)
    silu_rows = (silu_rows * _sigmoid(silu_rows)).astype(BF16)

    ones_blk = (jnp.arange(RW)[:, None] // HEAD == jnp.arange(RW)[None, :] // HEAD).astype(BF16)
    tri = _chunk_tri(tt)
    mu_pad = jnp.pad(rw_mu, ((0, 0), (0, ZR_COLS - rw_mu.shape[1])))[:, None, :]
    zeros64 = jnp.zeros((depth, 1, DECAY_LORA, RW), F32)
    wup = jnp.concatenate([jnp.concatenate([rw_w_up[:, :1], zeros64], axis=2),
                           jnp.concatenate([zeros64, rw_w_up[:, 1:]], axis=2)], axis=1).astype(BF16)
    aup = jnp.concatenate([jnp.concatenate([rw_a_up[:, :1], zeros64], axis=2),
                           jnp.concatenate([zeros64, rw_a_up[:, 1:]], axis=2)], axis=1).astype(BF16)
    gup = jnp.pad(rw_g_up, ((0, 0), (0, 2 * LANES - GATE_LORA), (0, 0))).astype(BF16)
    vecs = jnp.stack([rw_k_k, rw_k_a, rw_r_k.reshape(depth, RW), rw_w0[:, 0], rw_w0[:, 1],
                      rw_a0[:, 0], rw_a0[:, 1], jnp.zeros_like(rw_k_k)], axis=1)
    gn = jnp.pad(jnp.stack([rw_gn_w, rw_gn_b], axis=1), ((0, 0), (0, 6), (0, 0)))
    d_skip = s5_d.reshape(depth, 1, S5_W)
    norm1, norm2 = norm1[:, None, :], norm2[:, None, :]
    w_gates = w_in[:, :, S5_W + 3 * RW + LORA_COLS:]
    tables = jax.vmap(_s5_tables)(s5_lambda_re, s5_lambda_im, s5_log_step, s5_b_re, s5_b_im, s5_c_re, s5_c_im)

    for i in range(depth):
        mod = _matmul(silu_rows, w_ada, (i,), 6 * d, tm=8, tn=1024, bias=b_ada[i][None])

        h = _norm_mod(xs, norm1, i, mod, 0, tb=tb, lc=lc, tm=tr)
        u = _matmul(h, w_in, (i,), S5_W, tm=tm, tn=512)
        zr = _matmul(h, w_in, (i,), ZR_COLS, col_off=S5_W, tm=tm, tn=512)
        zg = _matmul(h, w_gates, (i,), 2 * d, tm=tm, tn=1024)

        ys = _s5_mix(u, tables, i, nb=nb, ncb=ncb, ncc=ncc)
        y_s5 = _s5_readout(ys, u, d_skip, s5_w_glu, i, tm=tt)

        v, g, bonus, ah, rh, bh, kh, bt, kt, pt = _rw_features(zr, i, mu_pad, wup, aup, gup, vecs, ones_blk, tri,
                                                               tb=tb, lc=lc, tt=tt)
        y_f, y_b = _wkv_scan(ah, rh, bh, kh, bt, kt, v, pt, nb=nb, tb=tb, lc=lc)
        y_rw = _rw_readout(y_f, y_b, bonus, g, gn, i, ones_blk, tm=tr)

        mrg = _merge(y_s5, y_rw, w_proj_s5, w_proj_rw, i, zg, tm=tm, tn=1024)
        xs = _down_resid(mrg, w_out, (i,), xs, mod, 2, tb=tb, lc=lc, tm=tm, tn=1024)

        j = i // 2
        h2 = _norm_mod(xs, norm2, i, mod, 3, tb=tb, lc=lc, tm=tr)
        if i % 2 == 0:
            act = _swiglu_up(h2, ffn_w_gate, ffn_w_up, (j,), tm=tm, tn=512)
            xs = _down_resid(act, ffn_w_down, (j,), xs, mod, 5, tb=tb, lc=lc, tm=tr, tn=512)
        else:
            router_pad = jnp.pad(moe_router[j], ((0, 0), (0, LANES - N_EXPERTS)))
            bias_pad = jnp.pad(moe_router_bias[j], (0, LANES - N_EXPERTS))[None]
            route = _router_gates(xs, norm2, i, mod, router_pad, bias_pad, tb=tb, lc=lc, tm=tt)
            plan = _moe_plan(route, MOE_TILE)
            xg = _gather_rows(h2.reshape(m, d // LANES, LANES), plan["src_tok"], plan["n_act"], rows=MOE_TILE)
            act = _moe_up(xg.reshape(plan["n_rows"], d), moe_w_gate, moe_w_up, j, plan, tme=MOE_TILE, tn=256)
            y_exp = _moe_down(act, moe_w_down, j, plan, tme=MOE_TILE, tn=512)
            xs = _moe_combine(xs, y_exp, plan["pos"], mod, 5, tb=tb, lc=lc, tc=tt)

    return _final_norm(xs.reshape(nb, tb, d), norm_f[None], lc=lc, tm=tt)
```

```python
import functools
import math

import jax
import jax.numpy as jnp
from jax import lax
from jax.experimental import pallas as pl
from jax.experimental.pallas import tpu as pltpu

F32 = jnp.float32
BF16 = jnp.bfloat16

GRID_W = 64
HEADS = 16
HEAD = 64
RW = HEADS * HEAD
S5_G = 32
S5_H = 16
S5_P = 64
S5_W = S5_G * S5_H
S5_T = 16
WKV_CHUNK = 64
DECAY_LORA = 64
ICLR_LORA = 64
GATE_LORA = 160
LORA_COLS = 2 * DECAY_LORA + 2 * ICLR_LORA + GATE_LORA
LORA_PAD = 512
ZR_COLS = 3 * RW + LORA_PAD
NORM_EPS = 1e-6
GN_EPS = 64e-5
N_EXPERTS = 8
LANES = 128
LOG2_E = 1.4426950408889634
VMEM_LIMIT = 56 * 1024 * 1024


def _cp(sem):
    return pltpu.CompilerParams(dimension_semantics=sem, vmem_limit_bytes=VMEM_LIMIT)


def _sigmoid(x):
    return 0.5 * jnp.tanh(0.5 * x) + 0.5


def _gelu_tanh(x):
    return 0.5 * x * (1.0 + jnp.tanh(math.sqrt(2.0 / math.pi) * (x + 0.044715 * (x * x * x))))


def _row_mod(tab_ref, tile, tm, tb, lc):
    start = tile * tm
    b = start // tb
    pos = (start - b * tb) + lax.broadcasted_iota(jnp.int32, (tm, 1), 0)
    ctx_row = tab_ref[0:1, :]
    lat_row = tab_ref[pl.ds(1 + b, 1), :]
    return jnp.where(pos < lc, ctx_row, lat_row)


def _split_dot(x, w_bf16):
    hi = x.astype(BF16)
    lo = (x - hi.astype(F32)).astype(BF16)
    return (jnp.dot(hi, w_bf16, preferred_element_type=F32)
            + jnp.dot(lo, w_bf16, preferred_element_type=F32))


def _lead_spec(lead, shape, fn):
    return pl.BlockSpec((None,) * len(lead) + shape, lambda j, i: tuple(lead) + fn(j, i))


def _mm_body(x_ref, w_ref, *rest, has_bias):
    if has_bias:
        b_ref, o_ref, wb_ref = rest
    else:
        o_ref, wb_ref = rest

    @pl.when(pl.program_id(1) == 0)
    def _():
        wb_ref[...] = w_ref[...].astype(BF16)

    acc = jnp.dot(x_ref[...], wb_ref[...], preferred_element_type=F32)
    if has_bias:
        acc = acc + b_ref[...]
    o_ref[...] = acc.astype(o_ref.dtype)


def _matmul(x, w, lead, n_out, *, col_off=0, tm, tn, out_dtype=F32, bias=None):
    m, k = x.shape
    off = col_off // tn
    in_specs = [pl.BlockSpec((tm, k), lambda j, i: (i, 0)),
                _lead_spec(lead, (k, tn), lambda j, i: (0, j + off))]
    args = [x, w]
    if bias is not None:
        in_specs.append(pl.BlockSpec((1, tn), lambda j, i: (0, j)))
        args.append(bias)
    return pl.pallas_call(
        functools.partial(_mm_body, has_bias=bias is not None),
        grid=(n_out // tn, m // tm),
        in_specs=in_specs,
        out_specs=pl.BlockSpec((tm, tn), lambda j, i: (i, j)),
        out_shape=jax.ShapeDtypeStruct((m, n_out), out_dtype),
        scratch_shapes=[pltpu.VMEM((k, tn), BF16)],
        compiler_params=_cp(("arbitrary", "arbitrary")),
    )(*args)


def _swiglu_up_body(x_ref, wg_ref, wu_ref, o_ref, wgb_ref, wub_ref):
    @pl.when(pl.program_id(1) == 0)
    def _():
        wgb_ref[...] = wg_ref[...].astype(BF16)
        wub_ref[...] = wu_ref[...].astype(BF16)

    x = x_ref[...]
    g = jnp.dot(x, wgb_ref[...], preferred_element_type=F32)
    u = jnp.dot(x, wub_ref[...], preferred_element_type=F32)
    o_ref[...] = (g * _sigmoid(g) * u).astype(o_ref.dtype)


def _swiglu_up(x, wg, wu, lead, *, tm, tn):
    m, k = x.shape
    f = wg.shape[-1]
    wspec = _lead_spec(lead, (k, tn), lambda j, i: (0, j))
    return pl.pallas_call(
        _swiglu_up_body,
        grid=(f // tn, m // tm),
        in_specs=[pl.BlockSpec((tm, k), lambda j, i: (i, 0)), wspec, wspec],
        out_specs=pl.BlockSpec((tm, tn), lambda j, i: (i, j)),
        out_shape=jax.ShapeDtypeStruct((m, f), BF16),
        scratch_shapes=[pltpu.VMEM((k, tn), BF16), pltpu.VMEM((k, tn), BF16)],
        compiler_params=_cp(("arbitrary", "arbitrary")),
    )(x, wg, wu)


def _down_resid_body(a_ref, w_ref, x_ref, tab_ref, *rest, tm, tb, lc, expert):
    if expert is None:
        o_ref, wb_ref = rest
    else:
        rs_ref, o_ref, wb_ref = rest

    @pl.when(pl.program_id(1) == 0)
    def _():
        wb_ref[...] = w_ref[...].astype(BF16)

    y = jnp.dot(a_ref[...], wb_ref[...], preferred_element_type=F32)
    gate = _row_mod(tab_ref, pl.program_id(1), tm, tb, lc)
    if expert is not None:
        gate = gate * rs_ref[:, expert:expert + 1]
    o_ref[...] = x_ref[...] + gate * y


def _down_resid(a, w, lead, x, mod, gate_idx, *, tb, lc, tm, tn, rowscale=None, expert=None):
    m, k = a.shape
    d = x.shape[1]
    goff = gate_idx * (d // tn)
    in_specs = [pl.BlockSpec((tm, k), lambda j, i: (i, 0)),
                _lead_spec(lead, (k, tn), lambda j, i: (0, j)),
                pl.BlockSpec((tm, tn), lambda j, i: (i, j)),
                pl.BlockSpec((8, tn), lambda j, i: (0, goff + j))]
    args = [a, w, x, mod]
    if rowscale is not None:
        in_specs.append(pl.BlockSpec((tm, LANES), lambda j, i: (i, 0)))
        args.append(rowscale)
    return pl.pallas_call(
        functools.partial(_down_resid_body, tm=tm, tb=tb, lc=lc, expert=expert),
        grid=(d // tn, m // tm),
        in_specs=in_specs,
        out_specs=pl.BlockSpec((tm, tn), lambda j, i: (i, j)),
        out_shape=jax.ShapeDtypeStruct((m, d), F32),
        scratch_shapes=[pltpu.VMEM((k, tn), BF16)],
        input_output_aliases={2: 0},
        compiler_params=_cp(("arbitrary", "arbitrary")),
    )(*args)


def _merge_body(ys_ref, yr_ref, ws_ref, wr_ref, gs_ref, gr_ref, o_ref, wsb_ref, wrb_ref):
    @pl.when(pl.program_id(1) == 0)
    def _():
        wsb_ref[...] = ws_ref[...].astype(BF16)
        wrb_ref[...] = wr_ref[...].astype(BF16)

    ps = jnp.dot(ys_ref[...], wsb_ref[...], preferred_element_type=F32)
    pr = jnp.dot(yr_ref[...], wrb_ref[...], preferred_element_type=F32)
    o_ref[...] = (_sigmoid(gs_ref[...]) * ps + _sigmoid(gr_ref[...]) * pr).astype(o_ref.dtype)


def _merge(ys, yr, w_ps, w_pr, layer, zg, *, tm, tn):
    m = ys.shape[0]
    d = w_ps.shape[-1]
    nb = d // tn
    return pl.pallas_call(
        _merge_body,
        grid=(nb, m // tm),
        in_specs=[pl.BlockSpec((tm, S5_W), lambda j, i: (i, 0)),
                  pl.BlockSpec((tm, RW), lambda j, i: (i, 0)),
                  _lead_spec((layer,), (S5_W, tn), lambda j, i: (0, j)),
                  _lead_spec((layer,), (RW, tn), lambda j, i: (0, j)),
                  pl.BlockSpec((tm, tn), lambda j, i: (i, j)),
                  pl.BlockSpec((tm, tn), lambda j, i: (i, nb + j))],
        out_specs=pl.BlockSpec((tm, tn), lambda j, i: (i, j)),
        out_shape=jax.ShapeDtypeStruct((m, d), BF16),
        scratch_shapes=[pltpu.VMEM((S5_W, tn), BF16), pltpu.VMEM((RW, tn), BF16)],
        compiler_params=_cp(("arbitrary", "arbitrary")),
    )(ys, yr, w_ps, w_pr, zg, zg)


def _norm_mod_body(x_ref, gain_ref, sh_ref, sc_ref, o_ref, *, tm, tb, lc):
    x = x_ref[...]
    y = x * lax.rsqrt(jnp.mean(x * x, axis=-1, keepdims=True) + NORM_EPS) * gain_ref[...]
    i = pl.program_id(0)
    shift = _row_mod(sh_ref, i, tm, tb, lc)
    scale = _row_mod(sc_ref, i, tm, tb, lc)
    o_ref[...] = (y * (1.0 + scale) + shift).astype(o_ref.dtype)


def _norm_mod(x, gain, layer, mod, shift_idx, *, tb, lc, tm, out_dtype=BF16):
    m, d = x.shape
    return pl.pallas_call(
        functools.partial(_norm_mod_body, tm=tm, tb=tb, lc=lc),
        grid=(m // tm,),
        in_specs=[pl.BlockSpec((tm, d), lambda i: (i, 0)),
                  pl.BlockSpec((None, 1, d), lambda i: (layer, 0, 0)),
                  pl.BlockSpec((8, d), lambda i: (0, shift_idx)),
                  pl.BlockSpec((8, d), lambda i: (0, shift_idx + 1))],
        out_specs=pl.BlockSpec((tm, d), lambda i: (i, 0)),
        out_shape=jax.ShapeDtypeStruct((m, d), out_dtype),
        compiler_params=_cp(("arbitrary",)),
    )(x, gain, mod, mod)


def _final_norm_body(x_ref, gain_ref, o_ref):
    x = x_ref[...]
    o_ref[...] = x * lax.rsqrt(jnp.mean(x * x, axis=-1, keepdims=True) + NORM_EPS) * gain_ref[...]


def _final_norm(x3, gain, *, lc, tm):
    b, tb, d = x3.shape
    l = tb - lc
    skip = lc // tm
    return pl.pallas_call(
        _final_norm_body,
        grid=(b, l // tm),
        in_specs=[pl.BlockSpec((None, tm, d), lambda bi, t: (bi, skip + t, 0)),
                  pl.BlockSpec((1, d), lambda bi, t: (0, 0))],
        out_specs=pl.BlockSpec((None, tm, d), lambda bi, t: (bi, t, 0)),
        out_shape=jax.ShapeDtypeStruct((b, l, d), F32),
        compiler_params=_cp(("arbitrary", "arbitrary")),
    )(x3, gain)


def _router_body(x_ref, gain_ref, sh_ref, sc_ref, w_ref, b_ref, h_ref, o_ref, *, tm, tb, lc):
    x = x_ref[...]
    y = x * lax.rsqrt(jnp.mean(x * x, axis=-1, keepdims=True) + NORM_EPS) * gain_ref[...]
    i = pl.program_id(0)
    h = y * (1.0 + _row_mod(sc_ref, i, tm, tb, lc)) + _row_mod(sh_ref, i, tm, tb, lc)
    h_ref[...] = h.astype(h_ref.dtype)
    logits = jnp.dot(h, w_ref[...], preferred_element_type=F32,
                     precision=lax.Precision.HIGHEST) + b_ref[...]
    lane = lax.broadcasted_iota(jnp.int32, logits.shape, 1).astype(F32)
    neg = jnp.float32(-jnp.inf)
    logits = jnp.where(lane < N_EXPERTS, logits, neg)
    v1 = jnp.max(logits, axis=-1, keepdims=True)
    i1 = jnp.min(jnp.where(logits == v1, lane, float(LANES)), axis=-1, keepdims=True)
    rest = jnp.where(lane == i1, neg, logits)
    v2 = jnp.max(rest, axis=-1, keepdims=True)
    i2 = jnp.min(jnp.where(rest == v2, lane, float(LANES)), axis=-1, keepdims=True)
    e2 = jnp.exp(v2 - v1)
    w1 = 1.0 / (1.0 + e2)
    w2 = e2 / (1.0 + e2)
    o_ref[...] = (jnp.where(lane == i1, w1, 0.0) + jnp.where(lane == i2, w2, 0.0)
                  + jnp.where(lane == N_EXPERTS, i1, 0.0) + jnp.where(lane == N_EXPERTS + 1, i2, 0.0)
                  + jnp.where(lane == N_EXPERTS + 2, w1, 0.0) + jnp.where(lane == N_EXPERTS + 3, w2, 0.0))


def _router_gates(x, gain, layer, mod, router_pad, bias_pad, *, tb, lc, tm):
    m, d = x.shape
    return pl.pallas_call(
        functools.partial(_router_body, tm=tm, tb=tb, lc=lc),
        grid=(m // tm,),
        in_specs=[pl.BlockSpec((tm, d), lambda i: (i, 0)),
                  pl.BlockSpec((None, 1, d), lambda i: (layer, 0, 0)),
                  pl.BlockSpec((8, d), lambda i: (0, 3)),
                  pl.BlockSpec((8, d), lambda i: (0, 4)),
                  pl.BlockSpec((d, LANES), lambda i: (0, 0)),
                  pl.BlockSpec((1, LANES), lambda i: (0, 0))],
        out_specs=[pl.BlockSpec((tm, d), lambda i: (i, 0)), pl.BlockSpec((tm, LANES), lambda i: (i, 0))],
        out_shape=[jax.ShapeDtypeStruct((m, d), BF16), jax.ShapeDtypeStruct((m, LANES), F32)],
        compiler_params=_cp(("arbitrary",)),
    )(x, gain, mod, mod, router_pad, bias_pad)


def _s5_tables(lam_re, lam_im, log_step, b_re, b_im, c_re, c_im):
    hp = lax.Precision.HIGHEST
    t = S5_T
    lam_re, lam_im = lam_re.astype(F32), lam_im.astype(F32)
    b_re, b_im = b_re.astype(F32), b_im.astype(F32)
    c_re, c_im = c_re.astype(F32), c_im.astype(F32)
    step = jnp.exp(log_step.astype(F32))[..., None]
    tau = jnp.arange(t + 1, dtype=F32)[None, :, None, None]
    mag = jnp.exp((lam_re * step)[:, None] * tau)
    ang = (lam_im * step)[:, None] * tau
    pw_re, pw_im = mag * jnp.cos(ang), mag * jnp.sin(ang)
    nr, ni = pw_re[:, 1] - 1.0, pw_im[:, 1]
    den = lam_re * lam_re + lam_im * lam_im
    q_re = (nr * lam_re + ni * lam_im) / den
    q_im = (ni * lam_re - nr * lam_im) / den
    bb_re = q_re[..., None] * b_re - q_im[..., None] * b_im
    bb_im = q_re[..., None] * b_im + q_im[..., None] * b_re

    def cmul(ar, ai, br, bi):
        return ar * br - ai * bi, ar * bi + ai * br

    x_re, x_im = cmul(pw_re[:, :t, :, :, None], pw_im[:, :t, :, :, None], bb_re[:, None], bb_im[:, None])
    kd = (jnp.einsum('ghp,dtgpk->dtghk', c_re, x_re, precision=hp)
          - jnp.einsum('ghp,dtgpk->dtghk', c_im, x_im, precision=hp))
    kf, kb = kd[0], kd[1]
    kall = jnp.concatenate([kb[1:][::-1], (kf[0] + kb[0])[None], kf[1:]], axis=0).astype(BF16)
    idx = (jnp.arange(t)[None, :] - jnp.arange(t)[:, None]) + (t - 1)
    kbig = jnp.transpose(kall[idx], (2, 0, 4, 1, 3)).reshape(S5_G, t * S5_H, t * S5_H)

    def chunk_in(d, powers):
        gr, gi = cmul(pw_re[d][powers][..., None], pw_im[d][powers][..., None], bb_re[d][None], bb_im[d][None])
        pack = lambda g: jnp.transpose(g, (1, 0, 3, 2)).reshape(S5_G, t * S5_H, S5_P)
        return pack(gr), pack(gi)

    def chunk_out(d, powers):
        cr, ci = cmul(c_re[None], c_im[None], pw_re[d][powers][:, :, None, :], pw_im[d][powers][:, :, None, :])
        pack = lambda g: jnp.transpose(g, (1, 3, 0, 2)).reshape(S5_G, S5_P, t * S5_H)
        return pack(cr), -pack(ci)

    gin = jnp.stack(chunk_in(0, t - 1 - jnp.arange(t)) + chunk_in(1, jnp.arange(t)), axis=1)
    cout = jnp.stack(chunk_out(0, 1 + jnp.arange(t)) + chunk_out(1, t - jnp.arange(t)), axis=1)
    lam_t = jnp.stack([pw_re[0, t], pw_im[0, t], pw_re[1, t], pw_im[1, t]], axis=1)
    return kbig, gin.astype(BF16), cout.astype(BF16), lam_t[:, :, None, :]


S5_SG = LANES // S5_H


def _s5_body(u_ref, k_ref, gin_ref, cout_ref, lam_ref, o_ref, sel_s, g_s, h_s, *, nb, ncb, ncc):
    t = S5_T
    nc = nb * ncb
    tw = t * S5_H

    @pl.when(pl.program_id(0) == 0)
    def _():
        row = lax.broadcasted_iota(jnp.int32, (t * LANES, tw), 0)
        col = lax.broadcasted_iota(jnp.int32, (t * LANES, tw), 1)
        same_s = (row // LANES) == (col // S5_H)
        lane_off = (row % LANES) - (col % S5_H)
        for g in range(S5_SG):
            sel_s[g] = jnp.where(jnp.logical_and(same_s, lane_off == g * S5_H), 1.0, 0.0).astype(BF16)

    ucat = jnp.concatenate([u_ref[pl.ds(s, nc, stride=t), :] for s in range(t)], axis=1).astype(BF16)
    ug = []
    for g in range(S5_SG):
        ug.append(jnp.dot(ucat, sel_s[g], preferred_element_type=F32).astype(BF16))
        for q in range(4):
            g_s[g, q] = jnp.dot(ug[g], gin_ref[g, q], preferred_element_type=F32)
    nlc = ncb - ncc

    def step(n, carry):
        out = []
        nb_idx = jnp.where(n < ncc, ncc - 1 - n, ncc + (nlc - 1) - (n - ncc))
        for g in range(S5_SG):
            lfr, lfi, lbr, lbi = lam_ref[g, 0], lam_ref[g, 1], lam_ref[g, 2], lam_ref[g, 3]
            for b in range(nb):
                k = 4 * (g * nb + b)
                fr, fi, br, bi = carry[k:k + 4]
                rf = b * ncb + n
                rb = b * ncb + nb_idx
                h_s[g, 0, pl.ds(rf, 1), :] = fr
                h_s[g, 1, pl.ds(rf, 1), :] = fi
                h_s[g, 2, pl.ds(rb, 1), :] = br
                h_s[g, 3, pl.ds(rb, 1), :] = bi
                out += [lfr * fr - lfi * fi + g_s[g, 0, pl.ds(rf, 1), :],
                        lfr * fi + lfi * fr + g_s[g, 1, pl.ds(rf, 1), :],
                        lbr * br - lbi * bi + g_s[g, 2, pl.ds(rb, 1), :],
                        lbr * bi + lbi * br + g_s[g, 3, pl.ds(rb, 1), :]]
        return tuple(out)

    zero = jnp.zeros((1, S5_P), F32)
    lax.fori_loop(0, ncb, step, (zero,) * (4 * nb * S5_SG))
    ycat = jnp.zeros((nc, t * LANES), F32)
    for g in range(S5_SG):
        y = jnp.dot(ug[g], k_ref[g], preferred_element_type=F32)
        for q in range(4):
            y = y + jnp.dot(h_s[g, q].astype(BF16), cout_ref[g, q], preferred_element_type=F32)
        hi = y.astype(BF16)
        lo = (y - hi.astype(F32)).astype(BF16)
        ycat = ycat + _dot_nt(hi, sel_s[g]) + _dot_nt(lo, sel_s[g])
    for s in range(t):
        o_ref[pl.ds(s, nc, stride=t), :] = ycat[:, s * LANES:(s + 1) * LANES]


def _s5_mix(u, tables, layer, *, nb, ncb, ncc):
    kbig, gin, cout, lam_t = tables
    m = u.shape[0]
    nc = nb * ncb
    tw = S5_T * S5_H
    return pl.pallas_call(
        functools.partial(_s5_body, nb=nb, ncb=ncb, ncc=ncc),
        grid=(S5_G // S5_SG,),
        in_specs=[pl.BlockSpec((m, LANES), lambda i: (0, i)),
                  pl.BlockSpec((None, S5_SG, tw, tw), lambda i: (layer, i, 0, 0)),
                  pl.BlockSpec((None, S5_SG, 4, tw, S5_P), lambda i: (layer, i, 0, 0, 0)),
                  pl.BlockSpec((None, S5_SG, 4, S5_P, tw), lambda i: (layer, i, 0, 0, 0)),
                  pl.BlockSpec((None, S5_SG, 4, 1, S5_P), lambda i: (layer, i, 0, 0, 0))],
        out_specs=pl.BlockSpec((m, LANES), lambda i: (0, i)),
        out_shape=jax.ShapeDtypeStruct((m, S5_W), F32),
        scratch_shapes=[pltpu.VMEM((S5_SG, S5_T * LANES, tw), BF16),
                        pltpu.VMEM((S5_SG, 4, nc, S5_P), F32), pltpu.VMEM((S5_SG, 4, nc, S5_P), F32)],
        compiler_params=_cp(("arbitrary",)),
    )(u, kbig, gin, cout, lam_t)


def _s5_readout_body(y_ref, u_ref, d_ref, w_ref, o_ref, wb_ref):
    @pl.when(pl.program_id(0) == 0)
    def _():
        wb_ref[...] = w_ref[...].astype(BF16)

    y = _gelu_tanh(y_ref[...] + d_ref[...] * u_ref[...])
    z = jnp.dot(y.astype(BF16), wb_ref[...], preferred_element_type=F32)
    o_ref[...] = (y * _sigmoid(z)).astype(o_ref.dtype)


def _s5_readout(y, u, d_skip, w_glu, layer, *, tm):
    m = y.shape[0]
    return pl.pallas_call(
        _s5_readout_body,
        grid=(m // tm,),
        in_specs=[pl.BlockSpec((tm, S5_W), lambda i: (i, 0)),
                  pl.BlockSpec((tm, S5_W), lambda i: (i, 0)),
                  pl.BlockSpec((None, 1, S5_W), lambda i: (layer, 0, 0)),
                  pl.BlockSpec((None, S5_W, S5_W), lambda i: (layer, 0, 0))],
        out_specs=pl.BlockSpec((tm, S5_W), lambda i: (i, 0)),
        out_shape=jax.ShapeDtypeStruct((m, S5_W), BF16),
        scratch_shapes=[pltpu.VMEM((S5_W, S5_W), BF16)],
        compiler_params=_cp(("arbitrary",)),
    )(y, u, d_skip, w_glu)


def _rw_feat_body(z_ref, up_ref, dn_ref, mu_ref, wup_ref, aup_ref, gup_ref, vec_ref, ones_ref, tri_ref,
                  v_ref, g_ref, bonus_ref, ah_ref, rh_ref, bh_ref, kh_ref, bt_ref, kt_ref, pt_ref,
                  *, tt, tb, lc):
    i = pl.program_id(0)
    start = i * tt
    b = start // tb
    pos = (start - b * tb) + lax.broadcasted_iota(jnp.int32, (tt, 1), 0)
    is_ctx = pos < lc
    tl = pos - lc
    col = tl & (GRID_W - 1)
    l_lat = tb - lc

    z = z_ref[...]
    prev = pltpu.roll(z, 1, axis=0)
    nxt = pltpu.roll(z, tt - 1, axis=0)
    if tt > GRID_W:
        up = jnp.concatenate([up_ref[...], z[:tt - GRID_W]], axis=0)
        down = jnp.concatenate([z[GRID_W:], dn_ref[...]], axis=0)
    else:
        up, down = up_ref[...], dn_ref[...]
    m_prev = jnp.logical_or(jnp.logical_and(is_ctx, pos >= 1), jnp.logical_and(tl >= 0, col >= 1))
    m_next = jnp.logical_or(jnp.logical_and(is_ctx, pos <= lc - 2),
                            jnp.logical_and(tl >= 0, col <= GRID_W - 2))
    m_up = tl >= GRID_W
    m_down = jnp.logical_and(tl >= 0, tl < l_lat - GRID_W)
    prev = jnp.where(m_prev, prev, 0.0)
    nxt = jnp.where(m_next, nxt, 0.0)
    up = jnp.where(is_ctx, prev, jnp.where(m_up, up, 0.0))
    down = jnp.where(is_ctx, nxt, jnp.where(m_down, down, 0.0))
    l4 = lax.broadcasted_iota(jnp.int32, (1, ZR_COLS), 1) & 3
    shifted = jnp.where(l4 == 0, prev, jnp.where(l4 == 1, nxt, jnp.where(l4 == 2, up, down)))
    z = z + mu_ref[...] * (shifted - z)

    r = z[:, 0:RW]
    k = z[:, RW:2 * RW]
    v = z[:, 2 * RW:3 * RW]
    lora_w = jnp.tanh(z[:, 3 * RW:3 * RW + LANES]).astype(BF16)
    lora_a = z[:, 3 * RW + LANES:3 * RW + 2 * LANES].astype(BF16)
    lora_g = _sigmoid(z[:, 3 * RW + 2 * LANES:3 * RW + 4 * LANES]).astype(BF16)
    k_k, k_a, r_k = vec_ref[0:1, :], vec_ref[1:2, :], vec_ref[2:3, :]
    ones = ones_ref[...]

    kk = k * k_k
    kk = kk * lax.rsqrt(jnp.maximum(_split_dot(kk * kk, ones), 1e-24))
    ksum = jnp.zeros_like(k)
    for d in range(2):
        w0 = vec_ref[3 + d:4 + d, :]
        a0 = vec_ref[5 + d:6 + d, :]
        wl = w0 + jnp.dot(lora_w, wup_ref[d], preferred_element_type=F32)
        lw = -math.exp(-0.5) * _sigmoid(wl)
        lw2 = lw * LOG2_E
        a = _sigmoid(a0 + jnp.dot(lora_a, aup_ref[d], preferred_element_type=F32))
        k_d = k * (1.0 + (a - 1.0) * k_a)
        bb = kk * a
        ksum = ksum + k_d
        lw_hi = lw2.astype(BF16)
        lw_lo = (lw2 - lw_hi.astype(F32)).astype(BF16)
        cl = (jnp.dot(tri_ref[d, 0], lw_hi, preferred_element_type=F32)
              + jnp.dot(tri_ref[d, 0], lw_lo, preferred_element_type=F32))
        rem = (jnp.dot(tri_ref[d, 1], lw_hi, preferred_element_type=F32)
               + jnp.dot(tri_ref[d, 1], lw_lo, preferred_element_type=F32))
        p_in = jnp.exp2(cl)
        p_inv = jnp.exp2(-cl)
        p_rem = jnp.exp2(rem)
        ah_ref[d] = (-kk * jnp.exp2(cl - lw2)).astype(BF16)
        rh_ref[d] = (r * p_in).astype(BF16)
        bh_ref[d] = (bb * p_inv).astype(BF16)
        kh_ref[d] = (k_d * p_inv).astype(BF16)
        bt_ref[d] = (bb * p_rem).astype(BF16)
        kt_ref[d] = (k_d * p_rem).astype(BF16)
        last = 0 if d == 1 else WKV_CHUNK - 1
        for cidx in range(tt // WKV_CHUNK):
            row = cidx * WKV_CHUNK + last
            pt_ref[d, cidx] = p_in[row:row + 1, :]
    v_ref[...] = v.astype(BF16)
    g_ref[...] = jnp.dot(lora_g, gup_ref[...], preferred_element_type=F32)
    bonus_ref[...] = _split_dot(r * ksum * r_k, ones) * v


def _rw_features(zr, layer, mu_pad, wup, aup, gup, vecs, ones_blk, tri, *, tb, lc, tt):
    m = zr.shape[0]
    nh = tt // GRID_W
    nblk64 = m // GRID_W
    ncht = tt // WKV_CHUNK
    full = pl.BlockSpec((tt, RW), lambda i: (i, 0))
    per_dir = pl.BlockSpec((2, tt, RW), lambda i: (0, i, 0))
    one = jax.ShapeDtypeStruct((m, RW), F32)
    two = jax.ShapeDtypeStruct((2, m, RW), BF16)
    return pl.pallas_call(
        functools.partial(_rw_feat_body, tt=tt, tb=tb, lc=lc),
        grid=(m // tt,),
        in_specs=[pl.BlockSpec((tt, ZR_COLS), lambda i: (i, 0)),
                  pl.BlockSpec((GRID_W, ZR_COLS), lambda i: (jnp.maximum(i * nh - 1, 0), 0)),
                  pl.BlockSpec((GRID_W, ZR_COLS), lambda i: (jnp.minimum((i + 1) * nh, nblk64 - 1), 0)),
                  pl.BlockSpec((None, 1, ZR_COLS), lambda i: (layer, 0, 0)),
                  pl.BlockSpec((None, 2, LANES, RW), lambda i: (layer, 0, 0, 0)),
                  pl.BlockSpec((None, 2, LANES, RW), lambda i: (layer, 0, 0, 0)),
                  pl.BlockSpec((None, 2 * LANES, RW), lambda i: (layer, 0, 0)),
                  pl.BlockSpec((None, 8, RW), lambda i: (layer, 0, 0)),
                  pl.BlockSpec((RW, RW), lambda i: (0, 0)),
                  pl.BlockSpec((2, 2, tt, tt), lambda i: (0, 0, 0, 0))],
        out_specs=[full, full, full, per_dir, per_dir, per_dir, per_dir, per_dir, per_dir,
                   pl.BlockSpec((2, ncht, 1, RW), lambda i: (0, i, 0, 0))],
        out_shape=[jax.ShapeDtypeStruct((m, RW), BF16), one, one, two, two, two, two, two, two,
                   jax.ShapeDtypeStruct((2, m // WKV_CHUNK, 1, RW), F32)],
        compiler_params=_cp(("arbitrary",)),
    )(zr, zr, zr, mu_pad, wup, aup, gup, vecs, ones_blk, tri)


def _chunk_tri(tt):
    t = jnp.arange(tt)[:, None]
    s = jnp.arange(tt)[None, :]
    same = (t // WKV_CHUNK) == (s // WKV_CHUNK)
    fwd = jnp.stack([same & (s <= t), same & (s > t)])
    bwd = jnp.stack([same & (s >= t), same & (s < t)])
    return jnp.stack([fwd, bwd]).astype(BF16)


def _dot_nt(a, b):
    return lax.dot_general(a, b, (((1,), (1,)), ((), ())), preferred_element_type=F32)


def _dot_tn(a, b):
    return lax.dot_general(a, b, (((0,), (0,)), ((), ())), preferred_element_type=F32)


def _wkv_body(*refs):
    c = WKV_CHUNK
    ins = (refs[0:8], refs[8:16])
    y_refs = refs[16:18]
    s_ref = refs[18]

    @pl.when(pl.program_id(1) == 0)
    def _():
        s_ref[...] = jnp.zeros_like(s_ref)

    t_idx = lax.broadcasted_iota(jnp.int32, (2 * c, LANES), 0)
    lane = lax.broadcasted_iota(jnp.int32, (2 * c, LANES), 1)
    diff = (t_idx & (c - 1)) - (lane & (c - 1))
    tri = (jnp.where(t_idx < c, diff, diff + 1) > 0, jnp.where(t_idx < c, -diff, 1 - diff) > 0)
    same_head = (t_idx // c) == (lane // c)
    lo = lax.broadcasted_iota(jnp.int32, (c, LANES), 1) < c

    def bd(x):
        zero = jnp.zeros_like(x)
        return jnp.concatenate([jnp.where(lo, x, zero), jnp.where(lo, zero, x)], axis=0)

    chains = [(d, p) for d in range(2) for p in range(HEADS // 2)]
    n = range(len(chains))
    sl = [slice(p * LANES, (p + 1) * LANES) for _, p in chains]
    ah, rh, bh, kh, bt, kt, vv, pt = ([ins[d][q] for d, _ in chains] for q in range(8))
    ar = [jnp.concatenate([ah[i][:, sl[i]], rh[i][:, sl[i]]], axis=0) for i in n]
    s_old = [s_ref[d, p] for d, p in chains]
    pbk = [_dot_nt(ar[i], jnp.concatenate([bd(bh[i][:, sl[i]]), bd(kh[i][:, sl[i]])], axis=0)) for i in n]
    ps = [_dot_nt(ar[i], s_old[i].astype(BF16)) for i in n]
    pb = [jnp.where(tri[chains[i][0]], pbk[i][:, :LANES], 0.0) for i in n]
    pk = [jnp.where(tri[chains[i][0]], pbk[i][:, LANES:], 0.0) for i in n]
    bdv = [bd(vv[i][:, sl[i]]) for i in n]
    x = [ps[i][:c] + jnp.dot(pk[i][:c].astype(BF16), bdv[i], preferred_element_type=F32) for i in n]
    lp = [pb[i][:c].astype(BF16) for i in n]
    for it in range(6):
        if it < 5:
            prod = [jnp.dot(lp[i], jnp.concatenate([bd(x[i].astype(BF16)), bd(lp[i])], axis=1),
                            preferred_element_type=F32) for i in n]
            x = [x[i] + prod[i][:, :LANES] for i in n]
            lp = [prod[i][:, LANES:].astype(BF16) for i in n]
        else:
            x = [x[i] + jnp.dot(lp[i], bd(x[i].astype(BF16)), preferred_element_type=F32) for i in n]
    u = [x[i].astype(BF16) for i in n]
    for i in n:
        y_refs[chains[i][0]][:, sl[i]] = ps[i][c:] + jnp.dot(
            jnp.concatenate([pb[i][c:], pk[i][c:]], axis=1).astype(BF16),
            jnp.concatenate([bd(u[i]), bdv[i]], axis=0), preferred_element_type=F32)
    upd = [_dot_tn(jnp.concatenate([u[i], vv[i][:, sl[i]]], axis=0),
                   jnp.concatenate([bt[i][:, sl[i]], kt[i][:, sl[i]]], axis=0)) for i in n]
    for i, (d, p) in enumerate(chains):
        s_ref[d, p] = s_old[i] * pt[i][:, sl[i]] + jnp.where(same_head, upd[i], 0.0)


def _wkv_scan(ah, rh, bh, kh, bt, kt, v, pt, *, nb, tb, lc):
    m = v.shape[0]
    c = WKV_CHUNK
    ncb, ncc = tb // c, lc // c
    nlc = ncb - ncc

    def chunk(d, b, n):
        if d == 0:
            return b * ncb + n
        return b * ncb + jnp.where(n < ncc, ncc - 1 - n, ncc + (nlc - 1) - (n - ncc))

    def specs(d):
        per_dir = pl.BlockSpec((None, c, RW), lambda b, n: (d, chunk(d, b, n), 0))
        return [per_dir] * 6 + [pl.BlockSpec((c, RW), lambda b, n: (chunk(d, b, n), 0)),
                                pl.BlockSpec((None, None, 1, RW), lambda b, n: (d, chunk(d, b, n), 0, 0))]

    out = jax.ShapeDtypeStruct((m, RW), F32)
    args = (ah, rh, bh, kh, bt, kt, v, pt)
    return pl.pallas_call(
        _wkv_body,
        grid=(nb, ncb),
        in_specs=specs(0) + specs(1),
        out_specs=[pl.BlockSpec((c, RW), lambda b, n: (chunk(0, b, n), 0)),
                   pl.BlockSpec((c, RW), lambda b, n: (chunk(1, b, n), 0))],
        out_shape=[out, out],
        scratch_shapes=[pltpu.VMEM((2, HEADS // 2, LANES, LANES), F32)],
        compiler_params=_cp(("arbitrary", "arbitrary")),
    )(*args, *args)


def _rw_readout_body(yf_ref, yb_ref, bonus_ref, g_ref, gn_ref, ones_ref, o_ref):
    y = yf_ref[...] + yb_ref[...]
    ones = ones_ref[...]
    mean = _split_dot(y, ones) * (1.0 / HEAD)
    yc = y - mean
    var = _split_dot(yc * yc, ones) * (1.0 / HEAD)
    yn = yc * lax.rsqrt(var + GN_EPS) * gn_ref[0:1, :] + gn_ref[1:2, :]
    o_ref[...] = ((yn + bonus_ref[...]) * g_ref[...]).astype(o_ref.dtype)


def _rw_readout(y_f, y_b, bonus, g, gn, layer, ones_blk, *, tm):
    m = y_f.shape[0]
    full = pl.BlockSpec((tm, RW), lambda i: (i, 0))
    return pl.pallas_call(
        _rw_readout_body,
        grid=(m // tm,),
        in_specs=[full, full, full, full,
                  pl.BlockSpec((None, 8, RW), lambda i: (layer, 0, 0)),
                  pl.BlockSpec((RW, RW), lambda i: (0, 0))],
        out_specs=full,
        out_shape=jax.ShapeDtypeStruct((m, RW), BF16),
        compiler_params=_cp(("arbitrary",)),
    )(y_f, y_b, bonus, g, gn, ones_blk)


MOE_TILE = 1024


def _moe_plan(route, tme):
    m = route.shape[0]
    n_asg = 2 * m
    idx = route[:, N_EXPERTS:N_EXPERTS + 2].astype(jnp.int32)
    e_flat = idx.T.reshape(n_asg)
    w_flat = route[:, N_EXPERTS + 2:N_EXPERTS + 4].T.reshape(n_asg)
    tok_flat = jnp.tile(jnp.arange(m, dtype=jnp.int32), 2)
    order = jnp.argsort(e_flat, stable=True).astype(jnp.int32)
    e_sorted, tok_sorted, w_sorted = e_flat[order], tok_flat[order], w_flat[order]
    experts = jnp.arange(N_EXPERTS, dtype=jnp.int32)
    cnt = jnp.sum((e_flat[:, None] == experts[None, :]).astype(jnp.int32), axis=0)
    start = jnp.cumsum(cnt) - cnt
    padded = ((cnt + tme - 1) // tme) * tme
    off_end = jnp.cumsum(padded)
    off = off_end - padded
    n_rows = ((n_asg + tme - 1) // tme) * tme + N_EXPERTS * tme
    n_tiles = n_rows // tme
    tile_e = jnp.sum((jnp.arange(n_tiles, dtype=jnp.int32)[:, None] * tme >= off_end[None, :]).astype(jnp.int32),
                     axis=1)
    tile_e = jnp.minimum(tile_e, N_EXPERTS - 1)
    first = jnp.concatenate([jnp.ones((1,), jnp.int32), (tile_e[1:] != tile_e[:-1]).astype(jnp.int32)])
    n_act = (off_end[-1] // tme).reshape(1)
    e_row = jnp.repeat(tile_e, tme)
    q = jnp.arange(n_rows, dtype=jnp.int32) - off[e_row]
    valid = q < cnt[e_row]
    s_row = jnp.clip(start[e_row] + q, 0, n_asg - 1)
    src_tok = jnp.where(valid, tok_sorted[s_row], 0)
    row_w = jnp.where(valid, w_sorted[s_row], 0.0)[:, None]
    dest_sorted = off[e_sorted] + (jnp.arange(n_asg, dtype=jnp.int32) - start[e_sorted])
    pos = jnp.zeros((n_asg,), jnp.int32).at[order].set(dest_sorted, unique_indices=True)
    return dict(src_tok=src_tok, row_w=row_w, pos=pos, tile_e=tile_e, first=first, n_act=n_act,
                n_rows=n_rows, n_tiles=n_tiles)


def _gather_body(idx_ref, nact_ref, src_ref, o_ref, sem, *, rows):
    i = pl.program_id(0)
    base = i * rows

    @pl.when(i < nact_ref[0])
    def _():
        def issue(r, carry):
            pltpu.make_async_copy(src_ref.at[idx_ref[base + r]], o_ref.at[r], sem).start()
            return carry

        lax.fori_loop(0, rows, issue, 0, unroll=8)
        pltpu.make_async_copy(o_ref, o_ref, sem).wait()

    @pl.when(i >= nact_ref[0])
    def _():
        o_ref[...] = jnp.zeros_like(o_ref)


def _gather_rows(src3, idx, n_act, *, rows):
    n_rows = idx.shape[0]
    blk = (rows,) + src3.shape[1:]
    return pl.pallas_call(
        functools.partial(_gather_body, rows=rows),
        grid_spec=pltpu.PrefetchScalarGridSpec(
            num_scalar_prefetch=2, grid=(n_rows // rows,),
            in_specs=[pl.BlockSpec(memory_space=pl.ANY)],
            out_specs=pl.BlockSpec(blk, lambda i, idx, na: (i, 0, 0)),
            scratch_shapes=[pltpu.SemaphoreType.DMA(())]),
        out_shape=jax.ShapeDtypeStruct((n_rows,) + src3.shape[1:], src3.dtype),
        compiler_params=_cp(("arbitrary",)),
    )(idx, n_act, src3)


def _moe_up_body(te_ref, first_ref, nact_ref, x_ref, wg_ref, wu_ref, o_ref, wgb_ref, wub_ref):
    i = pl.program_id(1)

    @pl.when(i < nact_ref[0])
    def _():
        @pl.when(first_ref[i] == 1)
        def _():
            wgb_ref[...] = wg_ref[...].astype(BF16)
            wub_ref[...] = wu_ref[...].astype(BF16)

        x = x_ref[...]
        g = jnp.dot(x, wgb_ref[...], preferred_element_type=F32)
        u = jnp.dot(x, wub_ref[...], preferred_element_type=F32)
        o_ref[...] = (g * _sigmoid(g) * u).astype(o_ref.dtype)

    @pl.when(i >= nact_ref[0])
    def _():
        o_ref[...] = jnp.zeros_like(o_ref)


def _moe_up(xg, wg, wu, layer, plan, *, tme, tn):
    p, k = xg.shape
    f = wg.shape[-1]

    def row(i, nact):
        return jnp.minimum(i, nact[0] - 1)

    wspec = pl.BlockSpec((None, None, k, tn), lambda j, i, te, fi, na: (layer, te[row(i, na)], 0, j))
    return pl.pallas_call(
        _moe_up_body,
        grid_spec=pltpu.PrefetchScalarGridSpec(
            num_scalar_prefetch=3, grid=(f // tn, p // tme),
            in_specs=[pl.BlockSpec((tme, k), lambda j, i, te, fi, na: (row(i, na), 0)), wspec, wspec],
            out_specs=pl.BlockSpec((tme, tn), lambda j, i, te, fi, na: (i, j)),
            scratch_shapes=[pltpu.VMEM((k, tn), BF16), pltpu.VMEM((k, tn), BF16)]),
        out_shape=jax.ShapeDtypeStruct((p, f), BF16),
        compiler_params=_cp(("arbitrary", "arbitrary")),
    )(plan["tile_e"], plan["first"], plan["n_act"], xg, wg, wu)


def _moe_down_body(te_ref, first_ref, nact_ref, a_ref, w_ref, rw_ref, o_ref, wb_ref):
    i = pl.program_id(1)

    @pl.when(i < nact_ref[0])
    def _():
        @pl.when(first_ref[i] == 1)
        def _():
            wb_ref[...] = w_ref[...].astype(BF16)

        o_ref[...] = rw_ref[...] * jnp.dot(a_ref[...], wb_ref[...], preferred_element_type=F32)

    @pl.when(i >= nact_ref[0])
    def _():
        o_ref[...] = jnp.zeros_like(o_ref)


def _moe_down(act, wd, layer, plan, *, tme, tn):
    p, k = act.shape
    d = wd.shape[-1]

    def row(i, nact):
        return jnp.minimum(i, nact[0] - 1)

    return pl.pallas_call(
        _moe_down_body,
        grid_spec=pltpu.PrefetchScalarGridSpec(
            num_scalar_prefetch=3, grid=(d // tn, p // tme),
            in_specs=[pl.BlockSpec((tme, k), lambda j, i, te, fi, na: (row(i, na), 0)),
                      pl.BlockSpec((None, None, k, tn), lambda j, i, te, fi, na: (layer, te[row(i, na)], 0, j)),
                      pl.BlockSpec((tme, 1), lambda j, i, te, fi, na: (row(i, na), 0))],
            out_specs=pl.BlockSpec((tme, tn), lambda j, i, te, fi, na: (i, j)),
            scratch_shapes=[pltpu.VMEM((k, tn), BF16)]),
        out_shape=jax.ShapeDtypeStruct((p, d), F32),
        compiler_params=_cp(("arbitrary", "arbitrary")),
    )(plan["tile_e"], plan["first"], plan["n_act"], act, wd, plan["row_w"])


def _moe_combine_body(pos_ref, x_ref, y_ref, tab_ref, o_ref, buf, sem, *, tc, m, tb, lc):
    i = pl.program_id(0)
    base = i * tc

    def issue(r, carry):
        pltpu.make_async_copy(y_ref.at[pos_ref[base + r]], buf.at[0, r], sem).start()
        pltpu.make_async_copy(y_ref.at[pos_ref[m + base + r]], buf.at[1, r], sem).start()
        return carry

    lax.fori_loop(0, tc, issue, 0)
    pltpu.make_async_copy(buf, buf, sem).wait()
    gate = _row_mod(tab_ref, i, tc, tb, lc)
    o_ref[...] = x_ref[...] + gate * (buf[0] + buf[1])


def _moe_combine(x, y, pos, mod, gate_idx, *, tb, lc, tc):
    m, d = x.shape
    return pl.pallas_call(
        functools.partial(_moe_combine_body, tc=tc, m=m, tb=tb, lc=lc),
        grid_spec=pltpu.PrefetchScalarGridSpec(
            num_scalar_prefetch=1, grid=(m // tc,),
            in_specs=[pl.BlockSpec((tc, d), lambda i, pos: (i, 0)),
                      pl.BlockSpec(memory_space=pl.ANY),
                      pl.BlockSpec((8, d), lambda i, pos: (0, gate_idx))],
            out_specs=pl.BlockSpec((tc, d), lambda i, pos: (i, 0)),
            scratch_shapes=[pltpu.VMEM((2, tc, d), F32), pltpu.SemaphoreType.DMA(())]),
        out_shape=jax.ShapeDtypeStruct((m, d), F32),
        input_output_aliases={1: 0},
        compiler_params=_cp(("arbitrary",)),
    )(pos, x, y, mod)


def _pad_rows(x, rows):
    return jnp.concatenate([x, jnp.zeros((rows - x.shape[0],) + x.shape[1:], x.dtype)], axis=0)


def _pick_tile(n, target):
    best = 8
    for t in range(8, target + 1, 8):
        if n % t == 0:
            best = t
    return best


def kernel(x, c, ctx, c_ctx, w_ada, b_ada, norm1, norm2, norm_f, w_in, s5_lambda_re, s5_lambda_im, s5_log_step, s5_b_re, s5_b_im, s5_c_re, s5_c_im, s5_d, s5_w_glu, rw_mu, rw_w0, rw_w_up, rw_a0, rw_a_up, rw_g_up, rw_k_k, rw_k_a, rw_r_k, rw_gn_w, rw_gn_b, w_proj_s5, w_proj_rw, w_out, ffn_w_gate, ffn_w_up, ffn_w_down, moe_router, moe_router_bias, moe_w_gate, moe_w_up, moe_w_down):
    nb, l, d = x.shape
    lc = ctx.shape[1]
    tb = lc + l
    m = nb * tb
    depth = w_in.shape[0]
    tt = lc
    assert lc % GRID_W == 0 and l % tt == 0 and l % GRID_W == 0 and nb * HEADS * 4 == LANES
    assert tb % S5_T == 0 and lc % S5_T == 0 and tt % WKV_CHUNK == 0
    tm = _pick_tile(tb, 1088)
    tr = _pick_tile(tb, 544)
    ncb, ncc = tb // S5_T, lc // S5_T

    xs = jnp.concatenate([ctx, x], axis=1).reshape(m, d)
    silu_rows = _pad_rows(jnp.concatenate([c_ctx[None], c], axis=0), 8)
    silu_rows = (silu_rows * _sigmoid(silu_rows)).astype(BF16)

    ones_blk = (jnp.arange(RW)[:, None] // HEAD == jnp.arange(RW)[None, :] // HEAD).astype(BF16)
    tri = _chunk_tri(tt)
    mu_pad = jnp.pad(rw_mu, ((0, 0), (0, ZR_COLS - rw_mu.shape[1])))[:, None, :]
    zeros64 = jnp.zeros((depth, 1, DECAY_LORA, RW), F32)
    wup = jnp.concatenate([jnp.concatenate([rw_w_up[:, :1], zeros64], axis=2),
                           jnp.concatenate([zeros64, rw_w_up[:, 1:]], axis=2)], axis=1).astype(BF16)
    aup = jnp.concatenate([jnp.concatenate([rw_a_up[:, :1], zeros64], axis=2),
                           jnp.concatenate([zeros64, rw_a_up[:, 1:]], axis=2)], axis=1).astype(BF16)
    gup = jnp.pad(rw_g_up, ((0, 0), (0, 2 * LANES - GATE_LORA), (0, 0))).astype(BF16)
    vecs = jnp.stack([rw_k_k, rw_k_a, rw_r_k.reshape(depth, RW), rw_w0[:, 0], rw_w0[:, 1],
                      rw_a0[:, 0], rw_a0[:, 1], jnp.zeros_like(rw_k_k)], axis=1)
    gn = jnp.pad(jnp.stack([rw_gn_w, rw_gn_b], axis=1), ((0, 0), (0, 6), (0, 0)))
    d_skip = s5_d.reshape(depth, 1, S5_W)
    norm1, norm2 = norm1[:, None, :], norm2[:, None, :]
    w_gates = w_in[:, :, S5_W + 3 * RW + LORA_COLS:].astype(BF16)
    tables = jax.vmap(_s5_tables)(s5_lambda_re, s5_lambda_im, s5_log_step, s5_b_re, s5_b_im, s5_c_re, s5_c_im)

    for i in range(depth):
        mod = _matmul(silu_rows, w_ada, (i,), 6 * d, tm=8, tn=1024, bias=b_ada[i][None])

        h = _norm_mod(xs, norm1, i, mod, 0, tb=tb, lc=lc, tm=tr)
        u = _matmul(h, w_in, (i,), S5_W, tm=tm, tn=512)
        zr = _matmul(h, w_in, (i,), ZR_COLS, col_off=S5_W, tm=tm, tn=512)
        zg = _matmul(h, w_gates, (i,), 2 * d, tm=tm, tn=1024)

        ys = _s5_mix(u, tables, i, nb=nb, ncb=ncb, ncc=ncc)
        y_s5 = _s5_readout(ys, u, d_skip, s5_w_glu, i, tm=tt)

        v, g, bonus, ah, rh, bh, kh, bt, kt, pt = _rw_features(zr, i, mu_pad, wup, aup, gup, vecs, ones_blk, tri,
                                                               tb=tb, lc=lc, tt=tt)
        y_f, y_b = _wkv_scan(ah, rh, bh, kh, bt, kt, v, pt, nb=nb, tb=tb, lc=lc)
        y_rw = _rw_readout(y_f, y_b, bonus, g, gn, i, ones_blk, tm=tr)

        mrg = _merge(y_s5, y_rw, w_proj_s5, w_proj_rw, i, zg, tm=tm, tn=1024)
        xs = _down_resid(mrg, w_out, (i,), xs, mod, 2, tb=tb, lc=lc, tm=tm, tn=1024)

        j = i // 2
        if i % 2 == 0:
            h2 = _norm_mod(xs, norm2, i, mod, 3, tb=tb, lc=lc, tm=tr)
            act = _swiglu_up(h2, ffn_w_gate, ffn_w_up, (j,), tm=tm, tn=512)
            xs = _down_resid(act, ffn_w_down, (j,), xs, mod, 5, tb=tb, lc=lc, tm=tr, tn=512)
        else:
            router_pad = jnp.pad(moe_router[j], ((0, 0), (0, LANES - N_EXPERTS)))
            bias_pad = jnp.pad(moe_router_bias[j], (0, LANES - N_EXPERTS))[None]
            h2, route = _router_gates(xs, norm2, i, mod, router_pad, bias_pad, tb=tb, lc=lc, tm=tt)
            plan = _moe_plan(route, MOE_TILE)
            xg = _gather_rows(h2.reshape(m, d // LANES, LANES), plan["src_tok"], plan["n_act"], rows=MOE_TILE)
            act = _moe_up(xg.reshape(plan["n_rows"], d), moe_w_gate, moe_w_up, j, plan, tme=MOE_TILE, tn=256)
            y_exp = _moe_down(act, moe_w_down, j, plan, tme=MOE_TILE, tn=512)
            xs = _moe_combine(xs, y_exp, plan["pos"], mod, 5, tb=tb, lc=lc, tc=tt)

    return _final_norm(xs.reshape(nb, tb, d), norm_f[None], lc=lc, tm=tt)
```

```python
import functools
import math

import jax
import jax.numpy as jnp
from jax import lax
from jax.experimental import pallas as pl
from jax.experimental.pallas import tpu as pltpu

F32 = jnp.float32
BF16 = jnp.bfloat16

GRID_W = 64
HEADS = 16
HEAD = 64
RW = HEADS * HEAD
S5_G = 32
S5_H = 16
S5_P = 64
S5_W = S5_G * S5_H
S5_T = 16
WKV_CHUNK = 64
DECAY_LORA = 64
ICLR_LORA = 64
GATE_LORA = 160
LORA_COLS = 2 * DECAY_LORA + 2 * ICLR_LORA + GATE_LORA
LORA_PAD = 512
ZR_COLS = 3 * RW + LORA_PAD
NORM_EPS = 1e-6
GN_EPS = 64e-5
N_EXPERTS = 8
LANES = 128
LOG2_E = 1.4426950408889634
VMEM_LIMIT = 56 * 1024 * 1024


def _cp(sem):
    return pltpu.CompilerParams(dimension_semantics=sem, vmem_limit_bytes=VMEM_LIMIT)


def _sigmoid(x):
    return 0.5 * jnp.tanh(0.5 * x) + 0.5


def _gelu_tanh(x):
    return 0.5 * x * (1.0 + jnp.tanh(math.sqrt(2.0 / math.pi) * (x + 0.044715 * (x * x * x))))


def _row_mod(tab_ref, tile, tm, tb, lc):
    start = tile * tm
    b = start // tb
    pos = (start - b * tb) + lax.broadcasted_iota(jnp.int32, (tm, 1), 0)
    ctx_row = tab_ref[0:1, :]
    lat_row = tab_ref[pl.ds(1 + b, 1), :]
    return jnp.where(pos < lc, ctx_row, lat_row)


def _split_dot(x, w_bf16):
    hi = x.astype(BF16)
    lo = (x - hi.astype(F32)).astype(BF16)
    return (jnp.dot(hi, w_bf16, preferred_element_type=F32)
            + jnp.dot(lo, w_bf16, preferred_element_type=F32))


def _lead_spec(lead, shape, fn):
    return pl.BlockSpec((None,) * len(lead) + shape, lambda j, i: tuple(lead) + fn(j, i))


def _mm_body(x_ref, w_ref, *rest, has_bias):
    if has_bias:
        b_ref, o_ref, wb_ref = rest
    else:
        o_ref, wb_ref = rest

    @pl.when(pl.program_id(1) == 0)
    def _():
        wb_ref[...] = w_ref[...].astype(BF16)

    acc = jnp.dot(x_ref[...], wb_ref[...], preferred_element_type=F32)
    if has_bias:
        acc = acc + b_ref[...]
    o_ref[...] = acc.astype(o_ref.dtype)


def _matmul(x, w, lead, n_out, *, col_off=0, tm, tn, out_dtype=F32, bias=None):
    m, k = x.shape
    off = col_off // tn
    in_specs = [pl.BlockSpec((tm, k), lambda j, i: (i, 0)),
                _lead_spec(lead, (k, tn), lambda j, i: (0, j + off))]
    args = [x, w]
    if bias is not None:
        in_specs.append(pl.BlockSpec((1, tn), lambda j, i: (0, j)))
        args.append(bias)
    return pl.pallas_call(
        functools.partial(_mm_body, has_bias=bias is not None),
        grid=(n_out // tn, m // tm),
        in_specs=in_specs,
        out_specs=pl.BlockSpec((tm, tn), lambda j, i: (i, j)),
        out_shape=jax.ShapeDtypeStruct((m, n_out), out_dtype),
        scratch_shapes=[pltpu.VMEM((k, tn), BF16)],
        compiler_params=_cp(("arbitrary", "arbitrary")),
    )(*args)


def _swiglu_up_body(x_ref, wg_ref, wu_ref, o_ref, wgb_ref, wub_ref):
    @pl.when(pl.program_id(1) == 0)
    def _():
        wgb_ref[...] = wg_ref[...].astype(BF16)
        wub_ref[...] = wu_ref[...].astype(BF16)

    x = x_ref[...]
    g = jnp.dot(x, wgb_ref[...], preferred_element_type=F32)
    u = jnp.dot(x, wub_ref[...], preferred_element_type=F32)
    o_ref[...] = (g * _sigmoid(g) * u).astype(o_ref.dtype)


def _swiglu_up(x, wg, wu, lead, *, tm, tn):
    m, k = x.shape
    f = wg.shape[-1]
    wspec = _lead_spec(lead, (k, tn), lambda j, i: (0, j))
    return pl.pallas_call(
        _swiglu_up_body,
        grid=(f // tn, m // tm),
        in_specs=[pl.BlockSpec((tm, k), lambda j, i: (i, 0)), wspec, wspec],
        out_specs=pl.BlockSpec((tm, tn), lambda j, i: (i, j)),
        out_shape=jax.ShapeDtypeStruct((m, f), BF16),
        scratch_shapes=[pltpu.VMEM((k, tn), BF16), pltpu.VMEM((k, tn), BF16)],
        compiler_params=_cp(("arbitrary", "arbitrary")),
    )(x, wg, wu)


def _down_resid_body(a_ref, w_ref, x_ref, tab_ref, *rest, tm, tb, lc, expert):
    if expert is None:
        o_ref, wb_ref = rest
    else:
        rs_ref, o_ref, wb_ref = rest

    @pl.when(pl.program_id(1) == 0)
    def _():
        wb_ref[...] = w_ref[...].astype(BF16)

    y = jnp.dot(a_ref[...], wb_ref[...], preferred_element_type=F32)
    gate = _row_mod(tab_ref, pl.program_id(1), tm, tb, lc)
    if expert is not None:
        gate = gate * rs_ref[:, expert:expert + 1]
    o_ref[...] = x_ref[...] + gate * y


def _down_resid(a, w, lead, x, mod, gate_idx, *, tb, lc, tm, tn, rowscale=None, expert=None):
    m, k = a.shape
    d = x.shape[1]
    goff = gate_idx * (d // tn)
    in_specs = [pl.BlockSpec((tm, k), lambda j, i: (i, 0)),
                _lead_spec(lead, (k, tn), lambda j, i: (0, j)),
                pl.BlockSpec((tm, tn), lambda j, i: (i, j)),
                pl.BlockSpec((8, tn), lambda j, i: (0, goff + j))]
    args = [a, w, x, mod]
    if rowscale is not None:
        in_specs.append(pl.BlockSpec((tm, LANES), lambda j, i: (i, 0)))
        args.append(rowscale)
    return pl.pallas_call(
        functools.partial(_down_resid_body, tm=tm, tb=tb, lc=lc, expert=expert),
        grid=(d // tn, m // tm),
        in_specs=in_specs,
        out_specs=pl.BlockSpec((tm, tn), lambda j, i: (i, j)),
        out_shape=jax.ShapeDtypeStruct((m, d), F32),
        scratch_shapes=[pltpu.VMEM((k, tn), BF16)],
        input_output_aliases={2: 0},
        compiler_params=_cp(("arbitrary", "arbitrary")),
    )(*args)


def _merge_body(ys_ref, yr_ref, ws_ref, wr_ref, gs_ref, gr_ref, o_ref, wsb_ref, wrb_ref):
    @pl.when(pl.program_id(1) == 0)
    def _():
        wsb_ref[...] = ws_ref[...].astype(BF16)
        wrb_ref[...] = wr_ref[...].astype(BF16)

    ps = jnp.dot(ys_ref[...], wsb_ref[...], preferred_element_type=F32)
    pr = jnp.dot(yr_ref[...], wrb_ref[...], preferred_element_type=F32)
    o_ref[...] = (_sigmoid(gs_ref[...]) * ps + _sigmoid(gr_ref[...]) * pr).astype(o_ref.dtype)


def _merge(ys, yr, w_ps, w_pr, layer, zg, *, tm, tn):
    m = ys.shape[0]
    d = w_ps.shape[-1]
    nb = d // tn
    return pl.pallas_call(
        _merge_body,
        grid=(nb, m // tm),
        in_specs=[pl.BlockSpec((tm, S5_W), lambda j, i: (i, 0)),
                  pl.BlockSpec((tm, RW), lambda j, i: (i, 0)),
                  _lead_spec((layer,), (S5_W, tn), lambda j, i: (0, j)),
                  _lead_spec((layer,), (RW, tn), lambda j, i: (0, j)),
                  pl.BlockSpec((tm, tn), lambda j, i: (i, j)),
                  pl.BlockSpec((tm, tn), lambda j, i: (i, nb + j))],
        out_specs=pl.BlockSpec((tm, tn), lambda j, i: (i, j)),
        out_shape=jax.ShapeDtypeStruct((m, d), BF16),
        scratch_shapes=[pltpu.VMEM((S5_W, tn), BF16), pltpu.VMEM((RW, tn), BF16)],
        compiler_params=_cp(("arbitrary", "arbitrary")),
    )(ys, yr, w_ps, w_pr, zg, zg)


def _norm_mod_body(x_ref, gain_ref, sh_ref, sc_ref, o_ref, *, tm, tb, lc):
    x = x_ref[...]
    y = x * lax.rsqrt(jnp.mean(x * x, axis=-1, keepdims=True) + NORM_EPS) * gain_ref[...]
    i = pl.program_id(0)
    shift = _row_mod(sh_ref, i, tm, tb, lc)
    scale = _row_mod(sc_ref, i, tm, tb, lc)
    o_ref[...] = (y * (1.0 + scale) + shift).astype(o_ref.dtype)


def _norm_mod(x, gain, layer, mod, shift_idx, *, tb, lc, tm, out_dtype=BF16):
    m, d = x.shape
    return pl.pallas_call(
        functools.partial(_norm_mod_body, tm=tm, tb=tb, lc=lc),
        grid=(m // tm,),
        in_specs=[pl.BlockSpec((tm, d), lambda i: (i, 0)),
                  pl.BlockSpec((None, 1, d), lambda i: (layer, 0, 0)),
                  pl.BlockSpec((8, d), lambda i: (0, shift_idx)),
                  pl.BlockSpec((8, d), lambda i: (0, shift_idx + 1))],
        out_specs=pl.BlockSpec((tm, d), lambda i: (i, 0)),
        out_shape=jax.ShapeDtypeStruct((m, d), out_dtype),
        compiler_params=_cp(("arbitrary",)),
    )(x, gain, mod, mod)


def _final_norm_body(x_ref, gain_ref, o_ref):
    x = x_ref[...]
    o_ref[...] = x * lax.rsqrt(jnp.mean(x * x, axis=-1, keepdims=True) + NORM_EPS) * gain_ref[...]


def _final_norm(x3, gain, *, lc, tm):
    b, tb, d = x3.shape
    l = tb - lc
    skip = lc // tm
    return pl.pallas_call(
        _final_norm_body,
        grid=(b, l // tm),
        in_specs=[pl.BlockSpec((None, tm, d), lambda bi, t: (bi, skip + t, 0)),
                  pl.BlockSpec((1, d), lambda bi, t: (0, 0))],
        out_specs=pl.BlockSpec((None, tm, d), lambda bi, t: (bi, t, 0)),
        out_shape=jax.ShapeDtypeStruct((b, l, d), F32),
        compiler_params=_cp(("arbitrary", "arbitrary")),
    )(x3, gain)


def _router_body(x_ref, gain_ref, sh_ref, sc_ref, w_ref, b_ref, h_ref, o_ref, *, tm, tb, lc):
    x = x_ref[...]
    y = x * lax.rsqrt(jnp.mean(x * x, axis=-1, keepdims=True) + NORM_EPS) * gain_ref[...]
    i = pl.program_id(0)
    h = y * (1.0 + _row_mod(sc_ref, i, tm, tb, lc)) + _row_mod(sh_ref, i, tm, tb, lc)
    h_ref[...] = h.astype(h_ref.dtype)
    logits = jnp.dot(h, w_ref[...], preferred_element_type=F32,
                     precision=lax.Precision.HIGHEST) + b_ref[...]
    lane = lax.broadcasted_iota(jnp.int32, logits.shape, 1).astype(F32)
    neg = jnp.float32(-jnp.inf)
    logits = jnp.where(lane < N_EXPERTS, logits, neg)
    v1 = jnp.max(logits, axis=-1, keepdims=True)
    i1 = jnp.min(jnp.where(logits == v1, lane, float(LANES)), axis=-1, keepdims=True)
    rest = jnp.where(lane == i1, neg, logits)
    v2 = jnp.max(rest, axis=-1, keepdims=True)
    i2 = jnp.min(jnp.where(rest == v2, lane, float(LANES)), axis=-1, keepdims=True)
    e2 = jnp.exp(v2 - v1)
    w1 = 1.0 / (1.0 + e2)
    w2 = e2 / (1.0 + e2)
    o_ref[...] = (jnp.where(lane == i1, w1, 0.0) + jnp.where(lane == i2, w2, 0.0)
                  + jnp.where(lane == N_EXPERTS, i1, 0.0) + jnp.where(lane == N_EXPERTS + 1, i2, 0.0)
                  + jnp.where(lane == N_EXPERTS + 2, w1, 0.0) + jnp.where(lane == N_EXPERTS + 3, w2, 0.0))


def _router_gates(x, gain, layer, mod, router_pad, bias_pad, *, tb, lc, tm):
    m, d = x.shape
    return pl.pallas_call(
        functools.partial(_router_body, tm=tm, tb=tb, lc=lc),
        grid=(m // tm,),
        in_specs=[pl.BlockSpec((tm, d), lambda i: (i, 0)),
                  pl.BlockSpec((None, 1, d), lambda i: (layer, 0, 0)),
                  pl.BlockSpec((8, d), lambda i: (0, 3)),
                  pl.BlockSpec((8, d), lambda i: (0, 4)),
                  pl.BlockSpec((d, LANES), lambda i: (0, 0)),
                  pl.BlockSpec((1, LANES), lambda i: (0, 0))],
        out_specs=[pl.BlockSpec((tm, d), lambda i: (i, 0)), pl.BlockSpec((tm, LANES), lambda i: (i, 0))],
        out_shape=[jax.ShapeDtypeStruct((m, d), BF16), jax.ShapeDtypeStruct((m, LANES), F32)],
        compiler_params=_cp(("arbitrary",)),
    )(x, gain, mod, mod, router_pad, bias_pad)


def _s5_tables(lam_re, lam_im, log_step, b_re, b_im, c_re, c_im):
    hp = lax.Precision.HIGHEST
    t = S5_T
    lam_re, lam_im = lam_re.astype(F32), lam_im.astype(F32)
    b_re, b_im = b_re.astype(F32), b_im.astype(F32)
    c_re, c_im = c_re.astype(F32), c_im.astype(F32)
    step = jnp.exp(log_step.astype(F32))[..., None]
    tau = jnp.arange(t + 1, dtype=F32)[None, :, None, None]
    mag = jnp.exp((lam_re * step)[:, None] * tau)
    ang = (lam_im * step)[:, None] * tau
    pw_re, pw_im = mag * jnp.cos(ang), mag * jnp.sin(ang)
    nr, ni = pw_re[:, 1] - 1.0, pw_im[:, 1]
    den = lam_re * lam_re + lam_im * lam_im
    q_re = (nr * lam_re + ni * lam_im) / den
    q_im = (ni * lam_re - nr * lam_im) / den
    bb_re = q_re[..., None] * b_re - q_im[..., None] * b_im
    bb_im = q_re[..., None] * b_im + q_im[..., None] * b_re

    def cmul(ar, ai, br, bi):
        return ar * br - ai * bi, ar * bi + ai * br

    x_re, x_im = cmul(pw_re[:, :t, :, :, None], pw_im[:, :t, :, :, None], bb_re[:, None], bb_im[:, None])
    kd = (jnp.einsum('ghp,dtgpk->dtghk', c_re, x_re, precision=hp)
          - jnp.einsum('ghp,dtgpk->dtghk', c_im, x_im, precision=hp))
    kf, kb = kd[0], kd[1]
    kall = jnp.concatenate([kb[1:][::-1], (kf[0] + kb[0])[None], kf[1:]], axis=0).astype(BF16)
    idx = (jnp.arange(t)[None, :] - jnp.arange(t)[:, None]) + (t - 1)
    kbig = jnp.transpose(kall[idx], (2, 0, 4, 1, 3)).reshape(S5_G, t * S5_H, t * S5_H)

    def chunk_in(d, powers):
        gr, gi = cmul(pw_re[d][powers][..., None], pw_im[d][powers][..., None], bb_re[d][None], bb_im[d][None])
        pack = lambda g: jnp.transpose(g, (1, 0, 3, 2)).reshape(S5_G, t * S5_H, S5_P)
        return pack(gr), pack(gi)

    def chunk_out(d, powers):
        cr, ci = cmul(c_re[None], c_im[None], pw_re[d][powers][:, :, None, :], pw_im[d][powers][:, :, None, :])
        pack = lambda g: jnp.transpose(g, (1, 3, 0, 2)).reshape(S5_G, S5_P, t * S5_H)
        return pack(cr), -pack(ci)

    gin = jnp.stack(chunk_in(0, t - 1 - jnp.arange(t)) + chunk_in(1, jnp.arange(t)), axis=1)
    cout = jnp.stack(chunk_out(0, 1 + jnp.arange(t)) + chunk_out(1, t - jnp.arange(t)), axis=1)
    lam_t = jnp.stack([pw_re[0, t], pw_im[0, t], pw_re[1, t], pw_im[1, t]], axis=1)
    return kbig, gin.astype(BF16), cout.astype(BF16), lam_t[:, :, None, :]


S5_SG = LANES // S5_H


def _s5_body(u_ref, k_ref, gin_ref, cout_ref, lam_ref, o_ref, sel_s, g_s, h_s, *, nb, ncb, ncc):
    t = S5_T
    nc = nb * ncb
    tw = t * S5_H

    @pl.when(pl.program_id(0) == 0)
    def _():
        row = lax.broadcasted_iota(jnp.int32, (t * LANES, tw), 0)
        col = lax.broadcasted_iota(jnp.int32, (t * LANES, tw), 1)
        same_s = (row // LANES) == (col // S5_H)
        lane_off = (row % LANES) - (col % S5_H)
        for g in range(S5_SG):
            sel_s[g] = jnp.where(jnp.logical_and(same_s, lane_off == g * S5_H), 1.0, 0.0).astype(BF16)

    ucat = jnp.concatenate([u_ref[pl.ds(s, nc, stride=t), :] for s in range(t)], axis=1).astype(BF16)
    ug = []
    for g in range(S5_SG):
        ug.append(jnp.dot(ucat, sel_s[g], preferred_element_type=F32).astype(BF16))
        for q in range(4):
            g_s[g, q] = jnp.dot(ug[g], gin_ref[g, q], preferred_element_type=F32)
    nlc = ncb - ncc

    def step(n, carry):
        out = []
        nb_idx = jnp.where(n < ncc, ncc - 1 - n, ncc + (nlc - 1) - (n - ncc))
        for g in range(S5_SG):
            lfr, lfi, lbr, lbi = lam_ref[g, 0], lam_ref[g, 1], lam_ref[g, 2], lam_ref[g, 3]
            for b in range(nb):
                k = 4 * (g * nb + b)
                fr, fi, br, bi = carry[k:k + 4]
                rf = b * ncb + n
                rb = b * ncb + nb_idx
                h_s[g, 0, pl.ds(rf, 1), :] = fr
                h_s[g, 1, pl.ds(rf, 1), :] = fi
                h_s[g, 2, pl.ds(rb, 1), :] = br
                h_s[g, 3, pl.ds(rb, 1), :] = bi
                out += [lfr * fr - lfi * fi + g_s[g, 0, pl.ds(rf, 1), :],
                        lfr * fi + lfi * fr + g_s[g, 1, pl.ds(rf, 1), :],
                        lbr * br - lbi * bi + g_s[g, 2, pl.ds(rb, 1), :],
                        lbr * bi + lbi * br + g_s[g, 3, pl.ds(rb, 1), :]]
        return tuple(out)

    zero = jnp.zeros((1, S5_P), F32)
    lax.fori_loop(0, ncb, step, (zero,) * (4 * nb * S5_SG))
    ycat = jnp.zeros((nc, t * LANES), F32)
    for g in range(S5_SG):
        y = jnp.dot(ug[g], k_ref[g], preferred_element_type=F32)
        for q in range(4):
            y = y + jnp.dot(h_s[g, q].astype(BF16), cout_ref[g, q], preferred_element_type=F32)
        hi = y.astype(BF16)
        lo = (y - hi.astype(F32)).astype(BF16)
        ycat = ycat + _dot_nt(hi, sel_s[g]) + _dot_nt(lo, sel_s[g])
    for s in range(t):
        o_ref[pl.ds(s, nc, stride=t), :] = ycat[:, s * LANES:(s + 1) * LANES]


def _s5_mix(u, tables, layer, *, nb, ncb, ncc):
    kbig, gin, cout, lam_t = tables
    m = u.shape[0]
    nc = nb * ncb
    tw = S5_T * S5_H
    return pl.pallas_call(
        functools.partial(_s5_body, nb=nb, ncb=ncb, ncc=ncc),
        grid=(S5_G // S5_SG,),
        in_specs=[pl.BlockSpec((m, LANES), lambda i: (0, i)),
                  pl.BlockSpec((None, S5_SG, tw, tw), lambda i: (layer, i, 0, 0)),
                  pl.BlockSpec((None, S5_SG, 4, tw, S5_P), lambda i: (layer, i, 0, 0, 0)),
                  pl.BlockSpec((None, S5_SG, 4, S5_P, tw), lambda i: (layer, i, 0, 0, 0)),
                  pl.BlockSpec((None, S5_SG, 4, 1, S5_P), lambda i: (layer, i, 0, 0, 0))],
        out_specs=pl.BlockSpec((m, LANES), lambda i: (0, i)),
        out_shape=jax.ShapeDtypeStruct((m, S5_W), F32),
        scratch_shapes=[pltpu.VMEM((S5_SG, S5_T * LANES, tw), BF16),
                        pltpu.VMEM((S5_SG, 4, nc, S5_P), F32), pltpu.VMEM((S5_SG, 4, nc, S5_P), F32)],
        compiler_params=_cp(("arbitrary",)),
    )(u, kbig, gin, cout, lam_t)


def _s5_readout_body(y_ref, u_ref, d_ref, w_ref, o_ref, wb_ref):
    @pl.when(pl.program_id(0) == 0)
    def _():
        wb_ref[...] = w_ref[...].astype(BF16)

    y = _gelu_tanh(y_ref[...] + d_ref[...] * u_ref[...])
    z = jnp.dot(y.astype(BF16), wb_ref[...], preferred_element_type=F32)
    o_ref[...] = (y * _sigmoid(z)).astype(o_ref.dtype)


def _s5_readout(y, u, d_skip, w_glu, layer, *, tm):
    m = y.shape[0]
    return pl.pallas_call(
        _s5_readout_body,
        grid=(m // tm,),
        in_specs=[pl.BlockSpec((tm, S5_W), lambda i: (i, 0)),
                  pl.BlockSpec((tm, S5_W), lambda i: (i, 0)),
                  pl.BlockSpec((None, 1, S5_W), lambda i: (layer, 0, 0)),
                  pl.BlockSpec((None, S5_W, S5_W), lambda i: (layer, 0, 0))],
        out_specs=pl.BlockSpec((tm, S5_W), lambda i: (i, 0)),
        out_shape=jax.ShapeDtypeStruct((m, S5_W), BF16),
        scratch_shapes=[pltpu.VMEM((S5_W, S5_W), BF16)],
        compiler_params=_cp(("arbitrary",)),
    )(y, u, d_skip, w_glu)


def _rw_feat_body(z_ref, up_ref, dn_ref, mu_ref, wup_ref, aup_ref, gup_ref, vec_ref, ones_ref, tri_ref,
                  v_ref, g_ref, bonus_ref, ah_ref, rh_ref, bh_ref, kh_ref, bt_ref, kt_ref, pt_ref,
                  *, tt, tb, lc):
    i = pl.program_id(0)
    start = i * tt
    b = start // tb
    pos = (start - b * tb) + lax.broadcasted_iota(jnp.int32, (tt, 1), 0)
    is_ctx = pos < lc
    tl = pos - lc
    col = tl & (GRID_W - 1)
    l_lat = tb - lc

    zoff = z_ref.shape[1] - ZR_COLS
    z = z_ref[:, zoff:]
    prev = pltpu.roll(z, 1, axis=0)
    nxt = pltpu.roll(z, tt - 1, axis=0)
    if tt > GRID_W:
        up = jnp.concatenate([up_ref[:, zoff:], z[:tt - GRID_W]], axis=0)
        down = jnp.concatenate([z[GRID_W:], dn_ref[:, zoff:]], axis=0)
    else:
        up, down = up_ref[:, zoff:], dn_ref[:, zoff:]
    m_prev = jnp.logical_or(jnp.logical_and(is_ctx, pos >= 1), jnp.logical_and(tl >= 0, col >= 1))
    m_next = jnp.logical_or(jnp.logical_and(is_ctx, pos <= lc - 2),
                            jnp.logical_and(tl >= 0, col <= GRID_W - 2))
    m_up = tl >= GRID_W
    m_down = jnp.logical_and(tl >= 0, tl < l_lat - GRID_W)
    prev = jnp.where(m_prev, prev, 0.0)
    nxt = jnp.where(m_next, nxt, 0.0)
    up = jnp.where(is_ctx, prev, jnp.where(m_up, up, 0.0))
    down = jnp.where(is_ctx, nxt, jnp.where(m_down, down, 0.0))
    l4 = lax.broadcasted_iota(jnp.int32, (1, ZR_COLS), 1) & 3
    shifted = jnp.where(l4 == 0, prev, jnp.where(l4 == 1, nxt, jnp.where(l4 == 2, up, down)))
    z = z + mu_ref[...] * (shifted - z)

    r = z[:, 0:RW]
    k = z[:, RW:2 * RW]
    v = z[:, 2 * RW:3 * RW]
    lora_w = jnp.tanh(z[:, 3 * RW:3 * RW + LANES]).astype(BF16)
    lora_a = z[:, 3 * RW + LANES:3 * RW + 2 * LANES].astype(BF16)
    lora_g = _sigmoid(z[:, 3 * RW + 2 * LANES:3 * RW + 4 * LANES]).astype(BF16)
    k_k, k_a, r_k = vec_ref[0:1, :], vec_ref[1:2, :], vec_ref[2:3, :]
    ones = ones_ref[...]

    kk = k * k_k
    kk = kk * lax.rsqrt(jnp.maximum(_split_dot(kk * kk, ones), 1e-24))
    ksum = jnp.zeros_like(k)
    for d in range(2):
        w0 = vec_ref[3 + d:4 + d, :]
        a0 = vec_ref[5 + d:6 + d, :]
        wl = w0 + jnp.dot(lora_w, wup_ref[d], preferred_element_type=F32)
        lw = -math.exp(-0.5) * _sigmoid(wl)
        lw2 = lw * LOG2_E
        a = _sigmoid(a0 + jnp.dot(lora_a, aup_ref[d], preferred_element_type=F32))
        k_d = k * (1.0 + (a - 1.0) * k_a)
        bb = kk * a
        ksum = ksum + k_d
        lw_hi = lw2.astype(BF16)
        lw_lo = (lw2 - lw_hi.astype(F32)).astype(BF16)
        cl = (jnp.dot(tri_ref[d, 0], lw_hi, preferred_element_type=F32)
              + jnp.dot(tri_ref[d, 0], lw_lo, preferred_element_type=F32))
        rem = (jnp.dot(tri_ref[d, 1], lw_hi, preferred_element_type=F32)
               + jnp.dot(tri_ref[d, 1], lw_lo, preferred_element_type=F32))
        p_in = jnp.exp2(cl)
        p_inv = jnp.exp2(-cl)
        p_rem = jnp.exp2(rem)
        ah_ref[d] = (-kk * jnp.exp2(cl - lw2)).astype(BF16)
        rh_ref[d] = (r * p_in).astype(BF16)
        bh_ref[d] = (bb * p_inv).astype(BF16)
        kh_ref[d] = (k_d * p_inv).astype(BF16)
        bt_ref[d] = (bb * p_rem).astype(BF16)
        kt_ref[d] = (k_d * p_rem).astype(BF16)
        last = 0 if d == 1 else WKV_CHUNK - 1
        for cidx in range(tt // WKV_CHUNK):
            row = cidx * WKV_CHUNK + last
            pt_ref[d, cidx] = p_in[row:row + 1, :]
    v_ref[...] = v.astype(BF16)
    g_ref[...] = jnp.dot(lora_g, gup_ref[...], preferred_element_type=F32)
    bonus_ref[...] = _split_dot(r * ksum * r_k, ones) * v


def _rw_features(zr, layer, mu_pad, wup, aup, gup, vecs, ones_blk, tri, *, tb, lc, tt):
    m, zc = zr.shape
    nh = tt // GRID_W
    nblk64 = m // GRID_W
    ncht = tt // WKV_CHUNK
    full = pl.BlockSpec((tt, RW), lambda i: (i, 0))
    per_dir = pl.BlockSpec((2, tt, RW), lambda i: (0, i, 0))
    one = jax.ShapeDtypeStruct((m, RW), F32)
    two = jax.ShapeDtypeStruct((2, m, RW), BF16)
    return pl.pallas_call(
        functools.partial(_rw_feat_body, tt=tt, tb=tb, lc=lc),
        grid=(m // tt,),
        in_specs=[pl.BlockSpec((tt, zc), lambda i: (i, 0)),
                  pl.BlockSpec((GRID_W, zc), lambda i: (jnp.maximum(i * nh - 1, 0), 0)),
                  pl.BlockSpec((GRID_W, zc), lambda i: (jnp.minimum((i + 1) * nh, nblk64 - 1), 0)),
                  pl.BlockSpec((None, 1, ZR_COLS), lambda i: (layer, 0, 0)),
                  pl.BlockSpec((None, 2, LANES, RW), lambda i: (layer, 0, 0, 0)),
                  pl.BlockSpec((None, 2, LANES, RW), lambda i: (layer, 0, 0, 0)),
                  pl.BlockSpec((None, 2 * LANES, RW), lambda i: (layer, 0, 0)),
                  pl.BlockSpec((None, 8, RW), lambda i: (layer, 0, 0)),
                  pl.BlockSpec((RW, RW), lambda i: (0, 0)),
                  pl.BlockSpec((2, 2, tt, tt), lambda i: (0, 0, 0, 0))],
        out_specs=[full, full, full, per_dir, per_dir, per_dir, per_dir, per_dir, per_dir,
                   pl.BlockSpec((2, ncht, 1, RW), lambda i: (0, i, 0, 0))],
        out_shape=[jax.ShapeDtypeStruct((m, RW), BF16), one, one, two, two, two, two, two, two,
                   jax.ShapeDtypeStruct((2, m // WKV_CHUNK, 1, RW), F32)],
        compiler_params=_cp(("arbitrary",)),
    )(zr, zr, zr, mu_pad, wup, aup, gup, vecs, ones_blk, tri)


def _chunk_tri(tt):
    t = jnp.arange(tt)[:, None]
    s = jnp.arange(tt)[None, :]
    same = (t // WKV_CHUNK) == (s // WKV_CHUNK)
    fwd = jnp.stack([same & (s <= t), same & (s > t)])
    bwd = jnp.stack([same & (s >= t), same & (s < t)])
    return jnp.stack([fwd, bwd]).astype(BF16)


def _dot_nt(a, b):
    return lax.dot_general(a, b, (((1,), (1,)), ((), ())), preferred_element_type=F32)


def _dot_tn(a, b):
    return lax.dot_general(a, b, (((0,), (0,)), ((), ())), preferred_element_type=F32)


def _wkv_body(*refs):
    c = WKV_CHUNK
    ins = (refs[0:8], refs[8:16])
    y_refs = refs[16:18]
    s_ref = refs[18]

    @pl.when(pl.program_id(1) == 0)
    def _():
        s_ref[...] = jnp.zeros_like(s_ref)

    t_idx = lax.broadcasted_iota(jnp.int32, (2 * c, LANES), 0)
    lane = lax.broadcasted_iota(jnp.int32, (2 * c, LANES), 1)
    diff = (t_idx & (c - 1)) - (lane & (c - 1))
    tri = (jnp.where(t_idx < c, diff, diff + 1) > 0, jnp.where(t_idx < c, -diff, 1 - diff) > 0)
    same_head = (t_idx // c) == (lane // c)
    lo = lax.broadcasted_iota(jnp.int32, (c, LANES), 1) < c

    def bd(x):
        zero = jnp.zeros_like(x)
        return jnp.concatenate([jnp.where(lo, x, zero), jnp.where(lo, zero, x)], axis=0)

    chains = [(d, p) for d in range(2) for p in range(HEADS // 2)]
    n = range(len(chains))
    sl = [slice(p * LANES, (p + 1) * LANES) for _, p in chains]
    ah, rh, bh, kh, bt, kt, vv, pt = ([ins[d][q] for d, _ in chains] for q in range(8))
    ar = [jnp.concatenate([ah[i][:, sl[i]], rh[i][:, sl[i]]], axis=0) for i in n]
    s_old = [s_ref[d, p] for d, p in chains]
    pbk = [_dot_nt(ar[i], jnp.concatenate([bd(bh[i][:, sl[i]]), bd(kh[i][:, sl[i]])], axis=0)) for i in n]
    ps = [_dot_nt(ar[i], s_old[i].astype(BF16)) for i in n]
    pb = [jnp.where(tri[chains[i][0]], pbk[i][:, :LANES], 0.0) for i in n]
    pk = [jnp.where(tri[chains[i][0]], pbk[i][:, LANES:], 0.0) for i in n]
    bdv = [bd(vv[i][:, sl[i]]) for i in n]
    x = [ps[i][:c] + jnp.dot(pk[i][:c].astype(BF16), bdv[i], preferred_element_type=F32) for i in n]
    lp = [pb[i][:c].astype(BF16) for i in n]
    for it in range(6):
        if it < 5:
            prod = [jnp.dot(lp[i], jnp.concatenate([bd(x[i].astype(BF16)), bd(lp[i])], axis=1),
                            preferred_element_type=F32) for i in n]
            x = [x[i] + prod[i][:, :LANES] for i in n]
            lp = [prod[i][:, LANES:].astype(BF16) for i in n]
        else:
            x = [x[i] + jnp.dot(lp[i], bd(x[i].astype(BF16)), preferred_element_type=F32) for i in n]
    u = [x[i].astype(BF16) for i in n]
    for i in n:
        y_refs[chains[i][0]][:, sl[i]] = ps[i][c:] + jnp.dot(
            jnp.concatenate([pb[i][c:], pk[i][c:]], axis=1).astype(BF16),
            jnp.concatenate([bd(u[i]), bdv[i]], axis=0), preferred_element_type=F32)
    upd = [_dot_tn(jnp.concatenate([u[i], vv[i][:, sl[i]]], axis=0),
                   jnp.concatenate([bt[i][:, sl[i]], kt[i][:, sl[i]]], axis=0)) for i in n]
    for i, (d, p) in enumerate(chains):
        s_ref[d, p] = s_old[i] * pt[i][:, sl[i]] + jnp.where(same_head, upd[i], 0.0)


def _wkv_scan(ah, rh, bh, kh, bt, kt, v, pt, *, nb, tb, lc):
    m = v.shape[0]
    c = WKV_CHUNK
    ncb, ncc = tb // c, lc // c
    nlc = ncb - ncc

    def chunk(d, b, n):
        if d == 0:
            return b * ncb + n
        return b * ncb + jnp.where(n < ncc, ncc - 1 - n, ncc + (nlc - 1) - (n - ncc))

    def specs(d):
        per_dir = pl.BlockSpec((None, c, RW), lambda b, n: (d, chunk(d, b, n), 0))
        return [per_dir] * 6 + [pl.BlockSpec((c, RW), lambda b, n: (chunk(d, b, n), 0)),
                                pl.BlockSpec((None, None, 1, RW), lambda b, n: (d, chunk(d, b, n), 0, 0))]

    out = jax.ShapeDtypeStruct((m, RW), F32)
    args = (ah, rh, bh, kh, bt, kt, v, pt)
    return pl.pallas_call(
        _wkv_body,
        grid=(nb, ncb),
        in_specs=specs(0) + specs(1),
        out_specs=[pl.BlockSpec((c, RW), lambda b, n: (chunk(0, b, n), 0)),
                   pl.BlockSpec((c, RW), lambda b, n: (chunk(1, b, n), 0))],
        out_shape=[out, out],
        scratch_shapes=[pltpu.VMEM((2, HEADS // 2, LANES, LANES), F32)],
        compiler_params=_cp(("arbitrary", "arbitrary")),
    )(*args, *args)


def _rw_readout_body(yf_ref, yb_ref, bonus_ref, g_ref, gn_ref, ones_ref, o_ref):
    y = yf_ref[...] + yb_ref[...]
    ones = ones_ref[...]
    mean = _split_dot(y, ones) * (1.0 / HEAD)
    yc = y - mean
    var = _split_dot(yc * yc, ones) * (1.0 / HEAD)
    yn = yc * lax.rsqrt(var + GN_EPS) * gn_ref[0:1, :] + gn_ref[1:2, :]
    o_ref[...] = ((yn + bonus_ref[...]) * g_ref[...]).astype(o_ref.dtype)


def _rw_readout(y_f, y_b, bonus, g, gn, layer, ones_blk, *, tm):
    m = y_f.shape[0]
    full = pl.BlockSpec((tm, RW), lambda i: (i, 0))
    return pl.pallas_call(
        _rw_readout_body,
        grid=(m // tm,),
        in_specs=[full, full, full, full,
                  pl.BlockSpec((None, 8, RW), lambda i: (layer, 0, 0)),
                  pl.BlockSpec((RW, RW), lambda i: (0, 0))],
        out_specs=full,
        out_shape=jax.ShapeDtypeStruct((m, RW), BF16),
        compiler_params=_cp(("arbitrary",)),
    )(y_f, y_b, bonus, g, gn, ones_blk)


MOE_TILE = 1024


def _moe_plan(route, tme):
    m = route.shape[0]
    n_asg = 2 * m
    idx = route[:, N_EXPERTS:N_EXPERTS + 2].astype(jnp.int32)
    e_flat = idx.T.reshape(n_asg)
    w_flat = route[:, N_EXPERTS + 2:N_EXPERTS + 4].T.reshape(n_asg)
    tok_flat = jnp.tile(jnp.arange(m, dtype=jnp.int32), 2)
    order = jnp.argsort(e_flat, stable=True).astype(jnp.int32)
    e_sorted, tok_sorted, w_sorted = e_flat[order], tok_flat[order], w_flat[order]
    experts = jnp.arange(N_EXPERTS, dtype=jnp.int32)
    cnt = jnp.sum((e_flat[:, None] == experts[None, :]).astype(jnp.int32), axis=0)
    start = jnp.cumsum(cnt) - cnt
    padded = ((cnt + tme - 1) // tme) * tme
    off_end = jnp.cumsum(padded)
    off = off_end - padded
    n_rows = ((n_asg + tme - 1) // tme) * tme + N_EXPERTS * tme
    n_tiles = n_rows // tme
    tile_e = jnp.sum((jnp.arange(n_tiles, dtype=jnp.int32)[:, None] * tme >= off_end[None, :]).astype(jnp.int32),
                     axis=1)
    tile_e = jnp.minimum(tile_e, N_EXPERTS - 1)
    first = jnp.concatenate([jnp.ones((1,), jnp.int32), (tile_e[1:] != tile_e[:-1]).astype(jnp.int32)])
    n_act = (off_end[-1] // tme).reshape(1)
    e_row = jnp.repeat(tile_e, tme)
    q = jnp.arange(n_rows, dtype=jnp.int32) - off[e_row]
    valid = q < cnt[e_row]
    s_row = jnp.clip(start[e_row] + q, 0, n_asg - 1)
    src_tok = jnp.where(valid, tok_sorted[s_row], 0)
    row_w = jnp.where(valid, w_sorted[s_row], 0.0)[:, None]
    dest_sorted = off[e_sorted] + (jnp.arange(n_asg, dtype=jnp.int32) - start[e_sorted])
    pos = jnp.zeros((n_asg,), jnp.int32).at[order].set(dest_sorted, unique_indices=True)
    return dict(src_tok=src_tok, row_w=row_w, pos=pos, tile_e=tile_e, first=first, n_act=n_act,
                n_rows=n_rows, n_tiles=n_tiles)


def _gather_body(idx_ref, nact_ref, src_ref, o_ref, sem, *, rows):
    i = pl.program_id(0)
    base = i * rows

    @pl.when(i < nact_ref[0])
    def _():
        def issue(r, carry):
            pltpu.make_async_copy(src_ref.at[idx_ref[base + r]], o_ref.at[r], sem).start()
            return carry

        lax.fori_loop(0, rows, issue, 0, unroll=8)
        pltpu.make_async_copy(o_ref, o_ref, sem).wait()

    @pl.when(i >= nact_ref[0])
    def _():
        o_ref[...] = jnp.zeros_like(o_ref)


def _gather_rows(src3, idx, n_act, *, rows):
    n_rows = idx.shape[0]
    blk = (rows,) + src3.shape[1:]
    return pl.pallas_call(
        functools.partial(_gather_body, rows=rows),
        grid_spec=pltpu.PrefetchScalarGridSpec(
            num_scalar_prefetch=2, grid=(n_rows // rows,),
            in_specs=[pl.BlockSpec(memory_space=pl.ANY)],
            out_specs=pl.BlockSpec(blk, lambda i, idx, na: (i, 0, 0)),
            scratch_shapes=[pltpu.SemaphoreType.DMA(())]),
        out_shape=jax.ShapeDtypeStruct((n_rows,) + src3.shape[1:], src3.dtype),
        compiler_params=_cp(("arbitrary",)),
    )(idx, n_act, src3)


def _moe_up_body(te_ref, first_ref, nact_ref, x_ref, wg_ref, wu_ref, o_ref, wgb_ref, wub_ref):
    i = pl.program_id(1)

    @pl.when(i < nact_ref[0])
    def _():
        @pl.when(first_ref[i] == 1)
        def _():
            wgb_ref[...] = wg_ref[...].astype(BF16)
            wub_ref[...] = wu_ref[...].astype(BF16)

        x = x_ref[...]
        g = jnp.dot(x, wgb_ref[...], preferred_element_type=F32)
        u = jnp.dot(x, wub_ref[...], preferred_element_type=F32)
        o_ref[...] = (g * _sigmoid(g) * u).astype(o_ref.dtype)

    @pl.when(i >= nact_ref[0])
    def _():
        o_ref[...] = jnp.zeros_like(o_ref)


def _moe_up(xg, wg, wu, layer, plan, *, tme, tn):
    p, k = xg.shape
    f = wg.shape[-1]

    def row(i, nact):
        return jnp.minimum(i, nact[0] - 1)

    wspec = pl.BlockSpec((None, None, k, tn), lambda j, i, te, fi, na: (layer, te[row(i, na)], 0, j))
    return pl.pallas_call(
        _moe_up_body,
        grid_spec=pltpu.PrefetchScalarGridSpec(
            num_scalar_prefetch=3, grid=(f // tn, p // tme),
            in_specs=[pl.BlockSpec((tme, k), lambda j, i, te, fi, na: (row(i, na), 0)), wspec, wspec],
            out_specs=pl.BlockSpec((tme, tn), lambda j, i, te, fi, na: (i, j)),
            scratch_shapes=[pltpu.VMEM((k, tn), BF16), pltpu.VMEM((k, tn), BF16)]),
        out_shape=jax.ShapeDtypeStruct((p, f), BF16),
        compiler_params=_cp(("arbitrary", "arbitrary")),
    )(plan["tile_e"], plan["first"], plan["n_act"], xg, wg, wu)


def _moe_down_body(te_ref, first_ref, nact_ref, a_ref, w_ref, rw_ref, o_ref, wb_ref):
    i = pl.program_id(1)

    @pl.when(i < nact_ref[0])
    def _():
        @pl.when(first_ref[i] == 1)
        def _():
            wb_ref[...] = w_ref[...].astype(BF16)

        o_ref[...] = rw_ref[...] * jnp.dot(a_ref[...], wb_ref[...], preferred_element_type=F32)

    @pl.when(i >= nact_ref[0])
    def _():
        o_ref[...] = jnp.zeros_like(o_ref)


def _moe_down(act, wd, layer, plan, *, tme, tn):
    p, k = act.shape
    d = wd.shape[-1]

    def row(i, nact):
        return jnp.minimum(i, nact[0] - 1)

    return pl.pallas_call(
        _moe_down_body,
        grid_spec=pltpu.PrefetchScalarGridSpec(
            num_scalar_prefetch=3, grid=(d // tn, p // tme),
            in_specs=[pl.BlockSpec((tme, k), lambda j, i, te, fi, na: (row(i, na), 0)),
                      pl.BlockSpec((None, None, k, tn), lambda j, i, te, fi, na: (layer, te[row(i, na)], 0, j)),
                      pl.BlockSpec((tme, 1), lambda j, i, te, fi, na: (row(i, na), 0))],
            out_specs=pl.BlockSpec((tme, tn), lambda j, i, te, fi, na: (i, j)),
            scratch_shapes=[pltpu.VMEM((k, tn), BF16)]),
        out_shape=jax.ShapeDtypeStruct((p, d), F32),
        compiler_params=_cp(("arbitrary", "arbitrary")),
    )(plan["tile_e"], plan["first"], plan["n_act"], act, wd, plan["row_w"])


def _moe_combine_body(pos_ref, x_ref, y_ref, tab_ref, o_ref, buf, sem, *, tc, m, tb, lc):
    i = pl.program_id(0)
    base = i * tc

    def issue(r, carry):
        pltpu.make_async_copy(y_ref.at[pos_ref[base + r]], buf.at[0, r], sem).start()
        pltpu.make_async_copy(y_ref.at[pos_ref[m + base + r]], buf.at[1, r], sem).start()
        return carry

    lax.fori_loop(0, tc, issue, 0)
    pltpu.make_async_copy(buf, buf, sem).wait()
    gate = _row_mod(tab_ref, i, tc, tb, lc)
    o_ref[...] = x_ref[...] + gate * (buf[0] + buf[1])


def _moe_combine(x, y, pos, mod, gate_idx, *, tb, lc, tc):
    m, d = x.shape
    return pl.pallas_call(
        functools.partial(_moe_combine_body, tc=tc, m=m, tb=tb, lc=lc),
        grid_spec=pltpu.PrefetchScalarGridSpec(
            num_scalar_prefetch=1, grid=(m // tc,),
            in_specs=[pl.BlockSpec((tc, d), lambda i, pos: (i, 0)),
                      pl.BlockSpec(memory_space=pl.ANY),
                      pl.BlockSpec((8, d), lambda i, pos: (0, gate_idx))],
            out_specs=pl.BlockSpec((tc, d), lambda i, pos: (i, 0)),
            scratch_shapes=[pltpu.VMEM((2, tc, d), F32), pltpu.SemaphoreType.DMA(())]),
        out_shape=jax.ShapeDtypeStruct((m, d), F32),
        input_output_aliases={1: 0},
        compiler_params=_cp(("arbitrary",)),
    )(pos, x, y, mod)


def _pad_rows(x, rows):
    return jnp.concatenate([x, jnp.zeros((rows - x.shape[0],) + x.shape[1:], x.dtype)], axis=0)


def _pick_tile(n, target):
    best = 8
    for t in range(8, target + 1, 8):
        if n % t == 0:
            best = t
    return best


def kernel(x, c, ctx, c_ctx, w_ada, b_ada, norm1, norm2, norm_f, w_in, s5_lambda_re, s5_lambda_im, s5_log_step, s5_b_re, s5_b_im, s5_c_re, s5_c_im, s5_d, s5_w_glu, rw_mu, rw_w0, rw_w_up, rw_a0, rw_a_up, rw_g_up, rw_k_k, rw_k_a, rw_r_k, rw_gn_w, rw_gn_b, w_proj_s5, w_proj_rw, w_out, ffn_w_gate, ffn_w_up, ffn_w_down, moe_router, moe_router_bias, moe_w_gate, moe_w_up, moe_w_down):
    nb, l, d = x.shape
    lc = ctx.shape[1]
    tb = lc + l
    m = nb * tb
    depth = w_in.shape[0]
    tt = lc
    assert lc % GRID_W == 0 and l % tt == 0 and l % GRID_W == 0 and nb * HEADS * 4 == LANES
    assert tb % S5_T == 0 and lc % S5_T == 0 and tt % WKV_CHUNK == 0
    tm = _pick_tile(tb, 1088)
    tr = _pick_tile(tb, 544)
    ncb, ncc = tb // S5_T, lc // S5_T

    xs = jnp.concatenate([ctx, x], axis=1).reshape(m, d)
    silu_rows = _pad_rows(jnp.concatenate([c_ctx[None], c], axis=0), 8)
    silu_rows = (silu_rows * _sigmoid(silu_rows)).astype(BF16)

    ones_blk = (jnp.arange(RW)[:, None] // HEAD == jnp.arange(RW)[None, :] // HEAD).astype(BF16)
    tri = _chunk_tri(tt)
    mu_pad = jnp.pad(rw_mu, ((0, 0), (0, ZR_COLS - rw_mu.shape[1])))[:, None, :]
    zeros64 = jnp.zeros((depth, 1, DECAY_LORA, RW), F32)
    wup = jnp.concatenate([jnp.concatenate([rw_w_up[:, :1], zeros64], axis=2),
                           jnp.concatenate([zeros64, rw_w_up[:, 1:]], axis=2)], axis=1).astype(BF16)
    aup = jnp.concatenate([jnp.concatenate([rw_a_up[:, :1], zeros64], axis=2),
                           jnp.concatenate([zeros64, rw_a_up[:, 1:]], axis=2)], axis=1).astype(BF16)
    gup = jnp.pad(rw_g_up, ((0, 0), (0, 2 * LANES - GATE_LORA), (0, 0))).astype(BF16)
    vecs = jnp.stack([rw_k_k, rw_k_a, rw_r_k.reshape(depth, RW), rw_w0[:, 0], rw_w0[:, 1],
                      rw_a0[:, 0], rw_a0[:, 1], jnp.zeros_like(rw_k_k)], axis=1)
    gn = jnp.pad(jnp.stack([rw_gn_w, rw_gn_b], axis=1), ((0, 0), (0, 6), (0, 0)))
    d_skip = s5_d.reshape(depth, 1, S5_W)
    norm1, norm2 = norm1[:, None, :], norm2[:, None, :]
    w_gates = w_in[:, :, S5_W + 3 * RW + LORA_COLS:].astype(BF16)
    tables = jax.vmap(_s5_tables)(s5_lambda_re, s5_lambda_im, s5_log_step, s5_b_re, s5_b_im, s5_c_re, s5_c_im)

    for i in range(depth):
        mod = _matmul(silu_rows, w_ada, (i,), 6 * d, tm=8, tn=1024, bias=b_ada[i][None])

        h = _norm_mod(xs, norm1, i, mod, 0, tb=tb, lc=lc, tm=tr)
        u = zr = _matmul(h, w_in, (i,), S5_W + ZR_COLS, tm=tm, tn=1024)
        zg = _matmul(h, w_gates, (i,), 2 * d, tm=tm, tn=1024)

        ys = _s5_mix(u, tables, i, nb=nb, ncb=ncb, ncc=ncc)
        y_s5 = _s5_readout(ys, u, d_skip, s5_w_glu, i, tm=tt)

        v, g, bonus, ah, rh, bh, kh, bt, kt, pt = _rw_features(zr, i, mu_pad, wup, aup, gup, vecs, ones_blk, tri,
                                                               tb=tb, lc=lc, tt=tt)
        y_f, y_b = _wkv_scan(ah, rh, bh, kh, bt, kt, v, pt, nb=nb, tb=tb, lc=lc)
        y_rw = _rw_readout(y_f, y_b, bonus, g, gn, i, ones_blk, tm=tr)

        mrg = _merge(y_s5, y_rw, w_proj_s5, w_proj_rw, i, zg, tm=tm, tn=1024)
        xs = _down_resid(mrg, w_out, (i,), xs, mod, 2, tb=tb, lc=lc, tm=tm, tn=1024)

        j = i // 2
        if i % 2 == 0:
            h2 = _norm_mod(xs, norm2, i, mod, 3, tb=tb, lc=lc, tm=tr)
            act = _swiglu_up(h2, ffn_w_gate, ffn_w_up, (j,), tm=tm, tn=512)
            xs = _down_resid(act, ffn_w_down, (j,), xs, mod, 5, tb=tb, lc=lc, tm=tr, tn=512)
        else:
            router_pad = jnp.pad(moe_router[j], ((0, 0), (0, LANES - N_EXPERTS)))
            bias_pad = jnp.pad(moe_router_bias[j], (0, LANES - N_EXPERTS))[None]
            h2, route = _router_gates(xs, norm2, i, mod, router_pad, bias_pad, tb=tb, lc=lc, tm=tt)
            plan = _moe_plan(route, MOE_TILE)
            xg = _gather_rows(h2.reshape(m, d // LANES, LANES), plan["src_tok"], plan["n_act"], rows=MOE_TILE)
            act = _moe_up(xg.reshape(plan["n_rows"], d), moe_w_gate, moe_w_up, j, plan, tme=MOE_TILE, tn=256)
            y_exp = _moe_down(act, moe_w_down, j, plan, tme=MOE_TILE, tn=512)
            xs = _moe_combine(xs, y_exp, plan["pos"], mod, 5, tb=tb, lc=lc, tc=tt)

    return _final_norm(xs.reshape(nb, tb, d), norm_f[None], lc=lc, tm=tt)
```
